```python
import math
import jax
import jax.numpy as jnp
from jax import lax
import numpy as np

D_MODEL = 1024
BATCH = 2
SEQ = 8192
DEPTH = 1
DEC_BATCH = 128
DEC_SEQ = 1
PAST_LEN = 8192
PAGE_SIZE = 128

D_RNN = D_MODEL
LRU_BLOCKS = 4
LRU_BW = D_RNN // LRU_BLOCKS
CONV_W = 4
LRU_C = 8.0
NSA_HEADS = 8
NSA_KV = 2
NSA_GRP = NSA_HEADS // NSA_KV
NSA_HD = 64
CMP_STRIDE = 16
CMP_LEN = 2 * CMP_STRIDE
CMP_HID = 2 * NSA_HD
SEL_BLK = 64
TOP_N = 16
WINDOW = 512
Q_BLOCK = 128
MEM_HEADS = 4
MEM_HD = 128
N_MEM = 256
REL_BUCKETS = 32
REL_MAX_DIST = 128
D_FF = -(-8 * D_MODEL // (3 * 256)) * 256
N_BRANCH = 3
EPS = 1e-6
NEG = -1e30
FORCE_SCORE = 1e6
SPLITS = (D_RNN, D_RNN, NSA_HEADS * NSA_HD, 3 * 2 * NSA_KV * NSA_HD, 3 * NSA_HEADS, MEM_HEADS * MEM_HD, N_BRANCH * D_MODEL)
N_IN = sum(SPLITS)

kernel_name = 'griffin_nsa_memory_hybrid_step'


def rmsnorm(x, g):
    xf = x.astype(jnp.float32)
    return xf * lax.rsqrt(jnp.mean(xf * xf, axis=-1, keepdims=True) + EPS) * g.astype(jnp.float32)


def masked_softmax(logits, mask):
    z = jnp.where(mask, logits.astype(jnp.float32), NEG)
    z = jnp.exp(z - jnp.max(z, axis=-1, keepdims=True)) * mask
    return z / jnp.maximum(jnp.sum(z, axis=-1, keepdims=True), 1e-30)


def t5_bucket(dist):
    n = jnp.maximum(dist, 0)
    exact = REL_BUCKETS // 2
    nf = jnp.maximum(n, 1).astype(jnp.float32)
    large = exact + (jnp.log(nf / exact) / math.log(REL_MAX_DIST / exact) * (REL_BUCKETS - exact)).astype(jnp.int32)
    return jnp.where(n < exact, n, jnp.minimum(large, REL_BUCKETS - 1))


def bias_qk(qpos, kpos, table):
    b = table.astype(jnp.float32)[t5_bucket(qpos[:, None] - kpos[None, :])]
    return b.reshape(b.shape[0], b.shape[1], NSA_KV, NSA_GRP).transpose(0, 2, 3, 1)


def compress(rows, pe, w1, w2):
    B, L, G, D = rows.shape
    n_ch = L // CMP_STRIDE
    ch = rows[:, :n_ch * CMP_STRIDE].reshape(B, n_ch, CMP_STRIDE, G, D)
    ch = ch.transpose(0, 1, 3, 2, 4).reshape(B, n_ch, G, CMP_STRIDE * D)
    half = CMP_STRIDE * D
    pos_term = pe.reshape(-1) @ w1
    h = ch[:, :-1] @ w1[:half] + ch[:, 1:] @ w1[half:] + pos_term
    return jax.nn.gelu(h) @ w2


def overlap_matrix(n_cmp, n_sel):
    cs = jnp.arange(n_cmp)[:, None] * CMP_STRIDE
    ss = jnp.arange(n_sel)[None, :] * SEL_BLK
    return ((cs < ss + SEL_BLK) & (cs + CMP_LEN > ss)).astype(jnp.float32)


def nsa_attend(q, qpos, kc, vc, ks, vs, kw, vw, wpos, gates, table):
    B, Q = q.shape[:2]
    G, R, D = NSA_KV, NSA_GRP, NSA_HD
    scale = D ** -0.5
    qg = q.reshape(B, Q, G, R, D)
    NC = kc.shape[1]
    cpos = jnp.arange(NC, dtype=jnp.int32) * CMP_STRIDE + (CMP_LEN - 1)
    lc = jnp.einsum('bqgrd,bcgd->bqgrc', qg, kc) * scale + bias_qk(qpos, cpos, table)[None]
    pc = masked_softmax(lc, (cpos[None, :] <= qpos[:, None])[None, :, None, None, :])
    oc = jnp.einsum('bqgrc,bcgd->bqgrd', pc, vc)
    NS = ks.shape[2]
    ssum = jnp.einsum('bqgrc,cs->bqgs', pc, overlap_matrix(NC, NS))
    blk = jnp.arange(NS, dtype=jnp.int32)[None, :]
    cur = (qpos // SEL_BLK)[:, None]
    forced = ((blk == 0) | (blk == cur) | (blk == cur - 1))[None, :, None, :]
    valid = (blk <= cur)[None, :, None, :]
    score = jnp.where(forced, FORCE_SCORE, jnp.where(valid, ssum, -1.0))
    n_top = min(TOP_N, NS)
    _, idx = lax.top_k(score, n_top)
    idx_t = idx.transpose(0, 2, 1, 3)
    take = jax.vmap(jax.vmap(lambda blocks, ids: blocks[ids]))
    ksel = take(ks, idx_t)
    vsel = take(vs, idx_t)
    spos = idx_t[..., None] * SEL_BLK + jnp.arange(SEL_BLK, dtype=jnp.int32)
    tg = table.astype(jnp.float32).reshape(REL_BUCKETS, G, R).transpose(1, 0, 2)
    sb = tg[jnp.arange(G)[None, :, None, None, None], t5_bucket(qpos[None, None, :, None, None] - spos)]
    ls = jnp.einsum('bqgrd,bgqnsd->bgqrns', qg, ksel) * scale + sb.transpose(0, 1, 2, 5, 3, 4)
    smask = (spos <= qpos[None, None, :, None, None])[:, :, :, None]
    ps = masked_softmax(ls.reshape(B, G, Q, R, n_top * SEL_BLK), smask.reshape(B, G, Q, 1, n_top * SEL_BLK))
    osel = jnp.einsum('bgqrm,bgqmd->bqgrd', ps, vsel.reshape(B, G, Q, n_top * SEL_BLK, D))
    lw = jnp.einsum('bqgrd,bwgd->bqgrw', qg, kw) * scale + bias_qk(qpos, wpos, table)[None]
    dw = qpos[:, None] - wpos[None, :]
    wmask = (dw >= 0) & (dw < WINDOW) & (wpos[None, :] >= 0)
    pw = masked_softmax(lw, wmask[None, :, None, None, :])
    ow = jnp.einsum('bqgrw,bwgd->bqgrd', pw, vw)
    g = gates.reshape(B, Q, G, R, 3)
    o = g[..., 0:1] * oc + g[..., 1:2] * osel + g[..., 2:3] * ow
    return o.reshape(B, Q, G * R * D)


def project(x, p):
    B, T, _ = x.shape
    z = rmsnorm(x, p['norm_mix']) @ p['w_in']
    bounds, acc = [], 0
    for width in SPLITS[:-1]:
        acc += width
        bounds.append(acc)
    lru_x, lru_gate, q, kv, ng, mq, mg = jnp.split(z, bounds, axis=-1)
    q = rmsnorm(q.reshape(B, T, NSA_HEADS, NSA_HD), p['g_nsa_q'])
    kv = kv.reshape(B, T, 3, 2, NSA_KV, NSA_HD)
    cmp_kv = kv[:, :, 0]
    sel_kv = jnp.stack([rmsnorm(kv[:, :, 1, 0], p['g_nsa_k'][1]), kv[:, :, 1, 1]], axis=2)
    win_kv = jnp.stack([rmsnorm(kv[:, :, 2, 0], p['g_nsa_k'][2]), kv[:, :, 2, 1]], axis=2)
    nsa_gates = jax.nn.sigmoid(ng.reshape(B, T, NSA_HEADS, 3))
    mem_q = rmsnorm(mq.reshape(B, T, MEM_HEADS, MEM_HD), p['g_mem_q'])
    merge = jax.nn.sigmoid(mg.reshape(B, T, N_BRANCH, D_MODEL))
    return lru_x, lru_gate, q, cmp_kv, sel_kv, win_kv, nsa_gates, mem_q, merge


def lru_branch(xb, gate, conv_prev, h0, p):
    B, T, W = xb.shape
    xp = jnp.concatenate([conv_prev.astype(xb.dtype), xb], axis=1)
    xc = p['conv_b'] + xp[:, 0:T] * p['conv_w'][0]
    for k in range(1, CONV_W):
        xc = xc + xp[:, k:k + T] * p['conv_w'][k]
    xr = xc.reshape(B, T, LRU_BLOCKS, LRU_BW)
    r = jax.nn.sigmoid(jnp.einsum('btnc,ncd->btnd', xr, p['w_lru_a']).reshape(B, T, W).astype(jnp.float32) + p['b_lru_a'])
    i = jax.nn.sigmoid(jnp.einsum('btnc,ncd->btnd', xr, p['w_lru_i']).reshape(B, T, W).astype(jnp.float32) + p['b_lru_i'])
    log_a = -LRU_C * r * jax.nn.softplus(-p['lru_lambda'].astype(jnp.float32))
    a = jnp.exp(log_a)
    b = jnp.sqrt(-jnp.expm1(2.0 * log_a)) * (i * xc.astype(jnp.float32))
    b = b.at[:, 0].add(a[:, 0] * h0.astype(jnp.float32))
    _, h = lax.associative_scan(lambda e1, e2: (e1[0] * e2[0], e2[0] * e1[1] + e2[1]), (a, b), axis=1)
    return h * jax.nn.gelu(gate.astype(jnp.float32)), xp[:, T:], h[:, -1]


def mem_kv_proj(mem, p):
    B, M, _ = mem.shape
    kv = (rmsnorm(mem, p['norm_mem']) @ p['w_mem_kv']).reshape(B, M, 2, MEM_HEADS, MEM_HD)
    return jnp.stack([rmsnorm(kv[:, :, 0], p['g_mem_k']), kv[:, :, 1]], axis=2)


def mem_attend(mq, mkv):
    B, T = mq.shape[:2]
    logits = jnp.einsum('bthd,bmhd->bhtm', mq, mkv[:, :, 0].astype(jnp.float32)) * MEM_HD ** -0.5
    o = jnp.einsum('bhtm,bmhd->bthd', jax.nn.softmax(logits, axis=-1), mkv[:, :, 1].astype(jnp.float32))
    return o.reshape(B, T, MEM_HEADS * MEM_HD)


def merge_ffn(x, o_lru, o_nsa, o_mem, merge, p):
    z = (merge[:, :, 0] * (o_lru @ p['w_up_a']) + merge[:, :, 1] * (o_nsa @ p['w_up_b'])
         + merge[:, :, 2] * (o_mem @ p['w_up_m']))
    h = x + z @ p['w_o']
    f = rmsnorm(h, p['norm_ffn'])
    return h + (jax.nn.silu(f @ p['w_ffn_gate']) * (f @ p['w_ffn_up'])) @ p['w_ffn_down']


def prompt_layer(x, mem, table, p):
    B, T, _ = x.shape
    lru_x, lru_gate, q, cmp_kv, sel_kv, win_kv, ngate, mq, merge = project(x, p)
    o_lru, conv_new, h_new = lru_branch(lru_x, lru_gate, jnp.zeros((B, CONV_W - 1, D_RNN), lru_x.dtype),
                                        jnp.zeros((B, D_RNN), jnp.float32), p)
    kc = rmsnorm(compress(cmp_kv[:, :, 0], p['pe_cmp_k'], p['w_cmp_k1'], p['w_cmp_k2']), p['g_nsa_k'][0])
    vc = compress(cmp_kv[:, :, 1], p['pe_cmp_v'], p['w_cmp_v1'], p['w_cmp_v2'])
    ns = T // SEL_BLK
    ks = sel_kv[:, :, 0].reshape(B, ns, SEL_BLK, NSA_KV, NSA_HD).transpose(0, 3, 1, 2, 4)
    vs = sel_kv[:, :, 1].reshape(B, ns, SEL_BLK, NSA_KV, NSA_HD).transpose(0, 3, 1, 2, 4)
    w_pad = jnp.pad(win_kv, ((0, 0), (WINDOW, 0), (0, 0), (0, 0), (0, 0)))

    def block(bi):
        q0 = bi * Q_BLOCK
        qpos = q0 + jnp.arange(Q_BLOCK, dtype=jnp.int32)
        qb = lax.dynamic_slice_in_dim(q, q0, Q_BLOCK, axis=1)
        gb = lax.dynamic_slice_in_dim(ngate, q0, Q_BLOCK, axis=1)
        wkv = lax.dynamic_slice_in_dim(w_pad, q0, WINDOW + Q_BLOCK, axis=1)
        wpos = q0 - WINDOW + jnp.arange(WINDOW + Q_BLOCK, dtype=jnp.int32)
        return nsa_attend(qb, qpos, kc, vc, ks, vs, wkv[:, :, 0], wkv[:, :, 1], wpos, gb, table)

    o_nsa = lax.map(block, jnp.arange(T // Q_BLOCK, dtype=jnp.int32))
    o_nsa = o_nsa.transpose(1, 0, 2, 3).reshape(B, T, NSA_HEADS * NSA_HD)
    mkv = mem_kv_proj(mem, p)
    o_mem = mem_attend(mq, mkv)
    y = merge_ffn(x, o_lru, o_nsa, o_mem, merge, p)
    return y, cmp_kv, sel_kv, win_kv[:, -min(WINDOW, T):], mkv, h_new, conv_new


def sample_layer(x, cmp_pool, sel_pool, page_table, win_buf, mem_kv, h0, conv0, table, p):
    DB, S, _ = x.shape
    past = page_table.shape[1] * cmp_pool.shape[1]
    L = past + S
    lru_x, lru_gate, q, cmp_kv, sel_kv, win_kv, ngate, mq, merge = project(x, p)
    o_lru, conv_new, h_new = lru_branch(lru_x, lru_gate, conv0, h0, p)

    def gather_past(pool):
        return pool[page_table].reshape(DB, past, 2, NSA_KV, NSA_HD)

    cmp_all = jnp.concatenate([gather_past(cmp_pool), cmp_kv], axis=1)
    kc = rmsnorm(compress(cmp_all[:, :, 0], p['pe_cmp_k'], p['w_cmp_k1'], p['w_cmp_k2']), p['g_nsa_k'][0])
    vc = compress(cmp_all[:, :, 1], p['pe_cmp_v'], p['w_cmp_v1'], p['w_cmp_v2'])
    ns = -(-L // SEL_BLK)
    sel_all = jnp.concatenate([gather_past(sel_pool), sel_kv], axis=1)
    sel_all = jnp.pad(sel_all, ((0, 0), (0, ns * SEL_BLK - L), (0, 0), (0, 0), (0, 0)))
    ks = sel_all[:, :, 0].reshape(DB, ns, SEL_BLK, NSA_KV, NSA_HD).transpose(0, 3, 1, 2, 4)
    vs = sel_all[:, :, 1].reshape(DB, ns, SEL_BLK, NSA_KV, NSA_HD).transpose(0, 3, 1, 2, 4)
    wb = win_buf.shape[1]
    win_all = jnp.concatenate([win_buf, win_kv], axis=1)
    wpos = past - wb + jnp.arange(wb + S, dtype=jnp.int32)
    qpos = past + jnp.arange(S, dtype=jnp.int32)
    o_nsa = nsa_attend(q, qpos, kc, vc, ks, vs, win_all[:, :, 0], win_all[:, :, 1], wpos, ngate, table)
    o_mem = mem_attend(mq, mem_kv)
    y = merge_ffn(x, o_lru, o_nsa, o_mem, merge, p)
    return y, cmp_kv, sel_kv, win_all[:, -wb:], h_new, conv_new


def setup_inputs(seed: int = 0) -> dict:
    key = jax.random.key(seed)
    keys = iter(jax.random.split(key, 64))

    def nrm(shape, scale):
        return jax.random.normal(next(keys), shape, jnp.float32) * scale

    def gain(shape):
        return 1.0 + nrm(shape, 0.02)

    n_pages = PAST_LEN // PAGE_SIZE
    n_pool = (DEC_BATCH * n_pages * 5) // 4
    wb = min(WINDOW, PAST_LEN)
    page_table = jax.random.permutation(next(keys), n_pool)[:DEC_BATCH * n_pages].reshape(DEC_BATCH, n_pages).astype(jnp.int32)
    u = jax.random.uniform(next(keys), (DEPTH, D_RNN), jnp.float32, 0.9, 0.999)
    a0 = u ** (1.0 / LRU_C)
    lru_lambda = jnp.log(a0) - jnp.log1p(-a0)
    return {
        'x_prompt': nrm((BATCH, SEQ, D_MODEL), 1.0),
        'x_sample': nrm((DEC_BATCH, DEC_SEQ, D_MODEL), 1.0),
        'mem_prompt': nrm((BATCH, N_MEM, D_MODEL), 1.0),
        'cache_cmp_kv': nrm((DEPTH, n_pool, PAGE_SIZE, 2, NSA_KV, NSA_HD), 1.0),
        'cache_sel_kv': nrm((DEPTH, n_pool, PAGE_SIZE, 2, NSA_KV, NSA_HD), 1.0),
        'page_table': page_table,
        'cache_win_kv': nrm((DEPTH, DEC_BATCH, wb, 2, NSA_KV, NSA_HD), 1.0),
        'cache_mem_kv': nrm((DEPTH, DEC_BATCH, N_MEM, 2, MEM_HEADS, MEM_HD), 1.0),
        'state_lru_h': nrm((DEPTH, DEC_BATCH, D_RNN), 0.5),
        'state_conv': nrm((DEPTH, DEC_BATCH, CONV_W - 1, D_RNN), 1.0),
        'rel_bias': nrm((REL_BUCKETS, NSA_HEADS), 0.5),
        'norm_mix': gain((DEPTH, D_MODEL)),
        'w_in': nrm((DEPTH, D_MODEL, N_IN), D_MODEL ** -0.5),
        'conv_w': nrm((DEPTH, CONV_W, D_RNN), CONV_W ** -0.5),
        'conv_b': nrm((DEPTH, D_RNN), 0.01),
        'w_lru_a': nrm((DEPTH, LRU_BLOCKS, LRU_BW, LRU_BW), LRU_BW ** -0.5),
        'b_lru_a': nrm((DEPTH, D_RNN), 0.01),
        'w_lru_i': nrm((DEPTH, LRU_BLOCKS, LRU_BW, LRU_BW), LRU_BW ** -0.5),
        'b_lru_i': nrm((DEPTH, D_RNN), 0.01),
        'lru_lambda': lru_lambda,
        'g_nsa_q': gain((DEPTH, NSA_HD)),
        'g_nsa_k': gain((DEPTH, 3, NSA_HD)),
        'pe_cmp_k': nrm((DEPTH, CMP_LEN, NSA_HD), 0.5),
        'w_cmp_k1': nrm((DEPTH, CMP_LEN * NSA_HD, CMP_HID), (CMP_LEN * NSA_HD) ** -0.5),
        'w_cmp_k2': nrm((DEPTH, CMP_HID, NSA_HD), CMP_HID ** -0.5),
        'pe_cmp_v': nrm((DEPTH, CMP_LEN, NSA_HD), 0.5),
        'w_cmp_v1': nrm((DEPTH, CMP_LEN * NSA_HD, CMP_HID), (CMP_LEN * NSA_HD) ** -0.5),
        'w_cmp_v2': nrm((DEPTH, CMP_HID, NSA_HD), CMP_HID ** -0.5),
        'norm_mem': gain((DEPTH, D_MODEL)),
        'w_mem_kv': nrm((DEPTH, D_MODEL, 2 * MEM_HEADS * MEM_HD), D_MODEL ** -0.5),
        'g_mem_q': gain((DEPTH, MEM_HD)),
        'g_mem_k': gain((DEPTH, MEM_HD)),
        'w_up_a': nrm((DEPTH, D_RNN, D_MODEL), D_RNN ** -0.5),
        'w_up_b': nrm((DEPTH, NSA_HEADS * NSA_HD, D_MODEL), (NSA_HEADS * NSA_HD) ** -0.5),
        'w_up_m': nrm((DEPTH, MEM_HEADS * MEM_HD, D_MODEL), (MEM_HEADS * MEM_HD) ** -0.5),
        'w_o': nrm((DEPTH, D_MODEL, D_MODEL), D_MODEL ** -0.5),
        'norm_ffn': gain((DEPTH, D_MODEL)),
        'w_ffn_gate': nrm((DEPTH, D_MODEL, D_FF), D_MODEL ** -0.5),
        'w_ffn_up': nrm((DEPTH, D_MODEL, D_FF), D_MODEL ** -0.5),
        'w_ffn_down': nrm((DEPTH, D_FF, D_MODEL), D_FF ** -0.5),
    }


def reference(x_prompt, x_sample, mem_prompt, cache_cmp_kv, cache_sel_kv, page_table, cache_win_kv,
              cache_mem_kv, state_lru_h, state_conv, rel_bias, norm_mix, w_in, conv_w, conv_b, w_lru_a,
              b_lru_a, w_lru_i, b_lru_i, lru_lambda, g_nsa_q, g_nsa_k, pe_cmp_k, w_cmp_k1, w_cmp_k2,
              pe_cmp_v, w_cmp_v1, w_cmp_v2, norm_mem, w_mem_kv, g_mem_q, g_mem_k, w_up_a, w_up_b, w_up_m,
              w_o, norm_ffn, w_ffn_gate, w_ffn_up, w_ffn_down):
    yp, ys = x_prompt, x_sample
    cmp_p, cmp_s, sel_p, sel_s, win_p, win_s, mem_p, h_p, h_s, cv_p, cv_s = [], [], [], [], [], [], [], [], [], [], []
    for l in range(DEPTH):
        p = dict(norm_mix=norm_mix[l], w_in=w_in[l], conv_w=conv_w[l], conv_b=conv_b[l], w_lru_a=w_lru_a[l],
                 b_lru_a=b_lru_a[l], w_lru_i=w_lru_i[l], b_lru_i=b_lru_i[l], lru_lambda=lru_lambda[l],
                 g_nsa_q=g_nsa_q[l], g_nsa_k=g_nsa_k[l], pe_cmp_k=pe_cmp_k[l], w_cmp_k1=w_cmp_k1[l],
                 w_cmp_k2=w_cmp_k2[l], pe_cmp_v=pe_cmp_v[l], w_cmp_v1=w_cmp_v1[l], w_cmp_v2=w_cmp_v2[l],
                 norm_mem=norm_mem[l], w_mem_kv=w_mem_kv[l], g_mem_q=g_mem_q[l], g_mem_k=g_mem_k[l],
                 w_up_a=w_up_a[l], w_up_b=w_up_b[l], w_up_m=w_up_m[l], w_o=w_o[l], norm_ffn=norm_ffn[l],
                 w_ffn_gate=w_ffn_gate[l], w_ffn_up=w_ffn_up[l], w_ffn_down=w_ffn_down[l])
        yp, c1, s1, w1, m1, h1, v1 = prompt_layer(yp, mem_prompt, rel_bias, p)
        ys, c2, s2, w2, h2, v2 = sample_layer(ys, cache_cmp_kv[l], cache_sel_kv[l], page_table, cache_win_kv[l],
                                              cache_mem_kv[l], state_lru_h[l], state_conv[l], rel_bias, p)
        cmp_p.append(c1)
        cmp_s.append(c2)
        sel_p.append(s1)
        sel_s.append(s2)
        win_p.append(w1)
        win_s.append(w2)
        mem_p.append(m1)
        h_p.append(h1)
        h_s.append(h2)
        cv_p.append(v1)
        cv_s.append(v2)
    return (yp, ys, jnp.stack(cmp_p), jnp.stack(cmp_s), jnp.stack(sel_p), jnp.stack(sel_s),
            jnp.stack(win_p), jnp.stack(win_s), jnp.stack(mem_p), jnp.stack(h_p), jnp.stack(h_s),
            jnp.stack(cv_p), jnp.stack(cv_s))
```

```python
import functools
import math

import jax
import jax.numpy as jnp
from jax import lax
from jax.experimental import pallas as pl
from jax.experimental.pallas import tpu as pltpu

F32 = jnp.float32
BF16 = jnp.bfloat16
I32 = jnp.int32

EPS = 1e-6
NEG = -1e30
FORCE_SCORE = 1e6
LRU_C = 8.0
LRU_BLOCKS = 4
CONV_W = 4
NSA_HEADS = 8
NSA_KV = 2
NSA_GRP = NSA_HEADS // NSA_KV
NSA_HD = 64
CMP_STRIDE = 16
CMP_LEN = 2 * CMP_STRIDE
CMP_HID = 2 * NSA_HD
SEL_BLK = 64
TOP_N = 16
WINDOW = 512
Q_BLOCK = 128
MEM_HEADS = 4
MEM_HD = 128
REL_BUCKETS = 32
REL_MAX_DIST = 128
N_BRANCH = 3

LANES = 128
KV_W = 2 * NSA_KV * NSA_HD
CHUNK_W = CMP_STRIDE * KV_W
VMEM_LIMIT = 56 * 1024 * 1024


def _cparams(sem):
    return pltpu.CompilerParams(dimension_semantics=sem, vmem_limit_bytes=VMEM_LIMIT)


def _full(shape):
    n = len(shape)
    return pl.BlockSpec(shape, lambda *_: (0,) * n)


def _dot(a, b):
    return jnp.dot(a, b, preferred_element_type=F32)


def _dot_nt(a, b):
    return lax.dot_general(a, b, (((1,), (1,)), ((), ())), preferred_element_type=F32)


def _group_rms(z, ones_bd, gain, width):
    ss = _dot((z * z).astype(BF16), ones_bd)
    return z * lax.rsqrt(ss * (1.0 / width) + EPS) * gain


def _masked_softmax(z, mask):
    zm = jnp.where(mask, z, NEG)
    e = jnp.exp(zm - jnp.max(zm, axis=-1, keepdims=True)) * mask.astype(F32)
    return e / jnp.maximum(jnp.sum(e, axis=-1, keepdims=True), 1e-30)


def _t5_bucket(dist):
    n = jnp.maximum(dist, 0)
    exact = REL_BUCKETS // 2
    nf = jnp.maximum(n, 1).astype(F32)
    large = exact + (jnp.log(nf / exact) / math.log(REL_MAX_DIST / exact) * (REL_BUCKETS - exact)).astype(I32)
    return jnp.where(n < exact, n, jnp.minimum(large, REL_BUCKETS - 1))


def _block_diag_ones(n, width):
    i = jnp.arange(n) // width
    return (i[:, None] == i[None, :]).astype(BF16)


_D = 1024
_OFF_LX, _OFF_LG, _OFF_Q, _OFF_KV, _OFF_NG, _OFF_MQ, _OFF_MG, _OFF_END = (
    0, 1024, 2048, 2560, 3328, 3456, 3968, 7040)


def _proj_kernel(x_ref, g_ref, w_ref, bdq_ref, bdk_ref, bdm_ref, gq_ref, gks_ref, gkw_ref, gmq_ref,
                 lx_ref, lg_ref, q_ref, cmp_ref, sel_ref, win_ref, selb_ref, winb_ref, ng_ref, mq_ref, mg_ref):
    x = x_ref[...]
    xn = (x * lax.rsqrt(jnp.mean(x * x, axis=-1, keepdims=True) + EPS) * g_ref[...]).astype(BF16)

    def seg(a, b):
        return _dot(xn, w_ref[:, a:b])

    lx_ref[...] = seg(_OFF_LX, _OFF_LG)
    lg_ref[...] = jax.nn.gelu(seg(_OFF_LG, _OFF_Q))
    zq = seg(_OFF_Q, _OFF_KV)
    q_ref[...] = _group_rms(zq, bdq_ref[...], gq_ref[...], NSA_HD) * (NSA_HD ** -0.5)
    cmp_ref[...] = seg(_OFF_KV, _OFF_KV + KV_W)
    for off, gain_ref, o_ref, ob_ref in ((_OFF_KV + KV_W, gks_ref, sel_ref, selb_ref),
                                         (_OFF_KV + 2 * KV_W, gkw_ref, win_ref, winb_ref)):
        z = seg(off, off + KV_W)
        kn = _group_rms(z[:, :LANES], bdk_ref[...], gain_ref[...], NSA_HD)
        kv = jnp.concatenate([kn, z[:, LANES:]], axis=1)
        o_ref[...] = kv
        ob_ref[...] = kv.astype(BF16)
    ng_ref[...] = jax.nn.sigmoid(seg(_OFF_NG, _OFF_MQ))
    zm = seg(_OFF_MQ, _OFF_MG)
    mq_ref[...] = _group_rms(zm, bdm_ref[...], gmq_ref[...], MEM_HD).astype(BF16)
    mg_ref[...] = jax.nn.sigmoid(seg(_OFF_MG, _OFF_END))


def _project(x2d, prm, tm):
    n = x2d.shape[0]
    row = lambda w: pl.BlockSpec((tm, w), lambda i: (i, 0))
    consts = [prm['norm_mix'], prm['w_in'], prm['bd_q'], prm['bd_k'], prm['bd_m'],
              prm['gq'], prm['gk_sel'], prm['gk_win'], prm['gmq']]
    widths = [(_D, F32), (_D, F32), (512, F32), (KV_W, F32), (KV_W, F32), (KV_W, F32), (KV_W, BF16), (KV_W, BF16),
              (LANES, F32), (512, BF16), (3 * _D, F32)]
    return pl.pallas_call(
        _proj_kernel,
        grid=(n // tm,),
        in_specs=[row(_D)] + [_full(c.shape) for c in consts],
        out_specs=[row(w) for w, _ in widths],
        out_shape=[jax.ShapeDtypeStruct((n, w), dt) for w, dt in widths],
        compiler_params=_cparams(("parallel",)),
    )(x2d, *consts)


def _lru_gates(xc, wa_ref, wi_ref, ba_ref, bi_ref, lam_ref):
    bw = xc.shape[1] // LRU_BLOCKS
    ra, ri = [], []
    for n in range(LRU_BLOCKS):
        xr = xc[:, n * bw:(n + 1) * bw].astype(BF16)
        ra.append(_dot(xr, wa_ref[n]))
        ri.append(_dot(xr, wi_ref[n]))
    r = jax.nn.sigmoid(jnp.concatenate(ra, axis=1) + ba_ref[...])
    i = jax.nn.sigmoid(jnp.concatenate(ri, axis=1) + bi_ref[...])
    lam = -lam_ref[...]
    softplus = jnp.maximum(lam, 0.0) + jnp.log1p(jnp.exp(-jnp.abs(lam)))
    log_a = -LRU_C * r * softplus
    a = jnp.exp(log_a)
    b = jnp.sqrt(jnp.tanh(-log_a) * (a * a + 1.0)) * (i * xc)
    return a, b


def _lru_scan(a, b):
    tt = a.shape[0]
    row = lax.broadcasted_iota(I32, a.shape, 0)
    k = 1
    while k < tt:
        keep = row >= k
        b = b + a * jnp.where(keep, pltpu.roll(b, k, axis=0), 0.0)
        a = a * jnp.where(keep, pltpu.roll(a, k, axis=0), 1.0)
        k *= 2
    return a, b


def _lru_prompt_kernel(x_ref, gg_ref, cw_ref, cb_ref, wa_ref, wi_ref, ba_ref, bi_ref, lam_ref,
                       o_ref, h_ref, cv_ref, xbuf, hcar):
    t = pl.program_id(1)
    tt = x_ref.shape[1]

    @pl.when(t == 0)
    def _():
        xbuf[0:8, :] = jnp.zeros((8, xbuf.shape[1]), F32)
        hcar[...] = jnp.zeros(hcar.shape, F32)

    x = x_ref[0]
    xbuf[8:8 + tt, :] = x
    xc = cb_ref[...] + xbuf[5:5 + tt, :] * cw_ref[0:1, :]
    for k in range(1, CONV_W):
        xc = xc + xbuf[5 + k:5 + k + tt, :] * cw_ref[k:k + 1, :]
    a, b = _lru_gates(xc, wa_ref, wi_ref, ba_ref, bi_ref, lam_ref)
    ap, hs = _lru_scan(a, b)
    h = hs + ap * hcar[...]
    o_ref[0] = (h * gg_ref[0]).astype(BF16)
    hcar[...] = h[tt - 1:tt, :]
    xbuf[0:8, :] = x[tt - 8:tt, :]

    @pl.when(t == pl.num_programs(1) - 1)
    def _():
        h_ref[0] = h[tt - 1:tt, :]
        cv_ref[0] = x[tt - (CONV_W - 1):tt, :]


def _lru_prompt(lx, gg, prm, tt):
    b, t, w = lx.shape
    blk = pl.BlockSpec((1, tt, w), lambda i, j: (i, j, 0))
    consts = [prm['conv_w'], prm['conv_b'], prm['w_lru_a'], prm['w_lru_i'], prm['b_lru_a'], prm['b_lru_i'],
              prm['lru_lambda']]
    return pl.pallas_call(
        _lru_prompt_kernel,
        grid=(b, t // tt),
        in_specs=[blk, blk] + [_full(c.shape) for c in consts],
        out_specs=[blk, pl.BlockSpec((1, 1, w), lambda i, j: (i, 0, 0)),
                   pl.BlockSpec((1, CONV_W - 1, w), lambda i, j: (i, 0, 0))],
        out_shape=[jax.ShapeDtypeStruct((b, t, w), BF16), jax.ShapeDtypeStruct((b, 1, w), F32),
                   jax.ShapeDtypeStruct((b, CONV_W - 1, w), F32)],
        scratch_shapes=[pltpu.VMEM((tt + 8, w), F32), pltpu.VMEM((1, w), F32)],
        compiler_params=_cparams(("parallel", "arbitrary")),
    )(lx, gg, *consts)


def _lru_sample_kernel(x_ref, gg_ref, cv0_ref, h0_ref, cw_ref, cb_ref, wa_ref, wi_ref, ba_ref, bi_ref, lam_ref,
                       o_ref, h_ref, cv_ref):
    w = x_ref.shape[1]
    x = x_ref[...]
    xc = cb_ref[...] + x * cw_ref[CONV_W - 1:CONV_W, :]
    for k in range(CONV_W - 1):
        xc = xc + cv0_ref[:, k * w:(k + 1) * w] * cw_ref[k:k + 1, :]
    a, b = _lru_gates(xc, wa_ref, wi_ref, ba_ref, bi_ref, lam_ref)
    h = a * h0_ref[...] + b
    o_ref[...] = (h * gg_ref[...]).astype(BF16)
    h_ref[...] = h
    cv_ref[:, 0:(CONV_W - 2) * w] = cv0_ref[:, w:(CONV_W - 1) * w]
    cv_ref[:, (CONV_W - 2) * w:(CONV_W - 1) * w] = x


def _lru_sample(lx, gg, cv0, h0, prm):
    n, w = lx.shape
    consts = [prm['conv_w'], prm['conv_b'], prm['w_lru_a'], prm['w_lru_i'], prm['b_lru_a'], prm['b_lru_i'],
              prm['lru_lambda']]
    args = [lx, gg, cv0, h0] + consts
    return pl.pallas_call(
        _lru_sample_kernel,
        grid=(1,),
        in_specs=[_full(a.shape) for a in args],
        out_specs=[_full((n, w)), _full((n, w)), _full((n, (CONV_W - 1) * w))],
        out_shape=[jax.ShapeDtypeStruct((n, w), BF16), jax.ShapeDtypeStruct((n, w), F32),
                   jax.ShapeDtypeStruct((n, (CONV_W - 1) * w), F32)],
        compiler_params=_cparams(("arbitrary",)),
    )(*args)


def _compress_kernel(pt_ref, pool_ref, w1k_ref, w1v_ref, w1kp_ref, w1vp_ref, pek_ref, pev_ref, w2k_ref, w2v_ref,
                     bdk_ref, gk_ref, kc_ref, vc_ref, buf, sem, *, n_valid):
    b = pl.program_id(0)
    n_pages = pt_ref.shape[1]
    rows_pp = pool_ref.shape[1]
    ncp = kc_ref.shape[1]

    def page_copy(p):
        return pltpu.make_async_copy(pool_ref.at[pt_ref[b, p]], buf.at[pl.ds(p * rows_pp, rows_pp)], sem)

    def start(p, c):
        page_copy(p).start()
        return c

    def wait(p, c):
        page_copy(p).wait()
        return c

    lax.fori_loop(0, n_pages, start, 0)
    lax.fori_loop(0, n_pages, wait, 0)

    n_ch = n_pages * rows_pp
    row = lax.broadcasted_iota(I32, (n_ch, 1), 0)
    keep = row < n_valid
    outs = []
    for kv, w1_ref, w1p_ref, pe_ref, w2_ref in ((0, w1k_ref, w1kp_ref, pek_ref, w2k_ref),
                                                 (1, w1v_ref, w1vp_ref, pev_ref, w2v_ref)):
        xs = jnp.concatenate(
            [buf[:, r * KV_W + kv * LANES:r * KV_W + (kv + 1) * LANES].astype(BF16) for r in range(CMP_STRIDE)],
            axis=1)
        hh = _dot(xs, w1_ref[...])
        pos = _dot(pe_ref[...], w1p_ref[...])[0:1, :]
        pos = jnp.concatenate([pos, pos], axis=1)
        nh = NSA_KV * CMP_HID
        h = hh[:, :nh] + pltpu.roll(hh[:, nh:], n_ch - 1, axis=0) + pos
        tok = _dot(jax.nn.gelu(h).astype(BF16), w2_ref[...])
        outs.append(tok)
    kc = _group_rms(outs[0], bdk_ref[...], gk_ref[...], NSA_HD)
    kc = jnp.where(keep, kc, 0.0).astype(BF16)
    vc = jnp.where(keep, outs[1], 0.0).astype(BF16)
    if ncp > n_ch:
        pad = jnp.zeros((ncp - n_ch, kc.shape[1]), BF16)
        kc = jnp.concatenate([kc, pad], axis=0)
        vc = jnp.concatenate([vc, pad], axis=0)
    kc_ref[0] = kc
    vc_ref[0] = vc


def _compress(pool, table, prm, ncp):
    nb, n_pages = table.shape
    rows_pp = pool.shape[1]
    n_ch = n_pages * rows_pp
    consts = [prm['w1k_big'], prm['w1v_big'], prm['w1k'], prm['w1v'], prm['pe_k'], prm['pe_v'],
              prm['w2k_bd'], prm['w2v_bd'], prm['bd_k'], prm['gk_cmp']]
    out_blk = pl.BlockSpec((1, ncp, LANES), lambda i, pt: (i, 0, 0))
    gs = pltpu.PrefetchScalarGridSpec(
        num_scalar_prefetch=1,
        grid=(nb,),
        in_specs=[pl.BlockSpec(memory_space=pl.ANY)] + [pl.BlockSpec(c.shape, lambda i, pt, _n=len(c.shape): (0,) * _n)
                                                         for c in consts],
        out_specs=[out_blk, out_blk],
        scratch_shapes=[pltpu.VMEM((n_ch, CHUNK_W), F32), pltpu.SemaphoreType.DMA(())],
    )
    return pl.pallas_call(
        functools.partial(_compress_kernel, n_valid=n_ch - 1),
        grid_spec=gs,
        out_shape=[jax.ShapeDtypeStruct((nb, ncp, LANES), BF16)] * 2,
        compiler_params=_cparams(("arbitrary",)),
    )(table, pool, *consts)


_WIN_TILES = WINDOW // Q_BLOCK + 1
_CB_SHIFT = 16


def _bias_tiles_kernel(tab_ref, wb_ref, cb_ref):
    g = pl.program_id(0)
    wcols = wb_ref.shape[2]
    ncp = cb_ref.shape[2]
    i_w = lax.broadcasted_iota(I32, (Q_BLOCK, wcols), 0)
    c_w = lax.broadcasted_iota(I32, (Q_BLOCK, wcols), 1)
    d_w = (_WIN_TILES - 1) * Q_BLOCK + i_w - c_w
    i_c = lax.broadcasted_iota(I32, (Q_BLOCK, ncp), 0)
    u_c = lax.broadcasted_iota(I32, (Q_BLOCK, ncp), 1)
    d_c = i_c - (CMP_LEN - 1) - CMP_STRIDE * (u_c - _CB_SHIFT)
    for d, ref in ((d_w, wb_ref), (d_c, cb_ref)):
        bucket = _t5_bucket(d)
        for r in range(NSA_GRP):
            h = g * NSA_GRP + r
            acc = jnp.zeros(d.shape, F32)
            for bk in range(REL_BUCKETS):
                acc = jnp.where(bucket == bk, tab_ref[bk, h], acc)
            far = tab_ref[REL_BUCKETS - 1, h]
            ref[0, r * Q_BLOCK:(r + 1) * Q_BLOCK, :] = jnp.where(d >= 0, acc - far, 0.0)


def _bias_tiles(rel_bias, ncp):
    wcols = _WIN_TILES * Q_BLOCK
    rows = NSA_GRP * Q_BLOCK
    return pl.pallas_call(
        _bias_tiles_kernel,
        grid=(NSA_KV,),
        in_specs=[pl.BlockSpec(memory_space=pltpu.SMEM)],
        out_specs=[pl.BlockSpec((1, rows, wcols), lambda g: (g, 0, 0)),
                   pl.BlockSpec((1, rows, ncp), lambda g: (g, 0, 0))],
        out_shape=[jax.ShapeDtypeStruct((NSA_KV, rows, wcols), F32),
                   jax.ShapeDtypeStruct((NSA_KV, rows, ncp), F32)],
        compiler_params=_cparams(("arbitrary",)),
    )(rel_bias)


def _topk_select(score, n_top):
    lane = lax.broadcasted_iota(I32, score.shape, 1).astype(F32)
    big = float(score.shape[1])
    sel = jnp.zeros(score.shape, F32)
    slot = lax.broadcasted_iota(I32, (score.shape[0], LANES), 1)
    idx = jnp.zeros((score.shape[0], LANES), F32)
    for it in range(n_top):
        m = jnp.max(score, axis=-1, keepdims=True)
        first = jnp.min(jnp.where(score == m, lane, big), axis=-1, keepdims=True)
        hit = lane == first
        sel = jnp.where(hit, 1.0, sel)
        score = jnp.where(hit, -jnp.inf, score)
        idx = jnp.where(slot == it, first, idx)
    return sel, idx


_FAR_TK = 256


def _online_update(carry, s, mask, v):
    m, l, acc = carry
    zm = jnp.where(mask, s, NEG)
    m_new = jnp.maximum(m, jnp.max(zm, axis=-1, keepdims=True))
    alpha = jnp.exp(m - m_new)
    e = jnp.exp(zm - m_new) * mask.astype(F32)
    l = alpha * l + jnp.sum(e, axis=-1, keepdims=True)
    acc = alpha * acc + _dot(e.astype(BF16), v)
    return m_new, l, acc


def _nsa_prompt_kernel(q_ref, kc_ref, vc_ref, sel_ref, win_ref, gate_ref, wb_ref, cb_ref, ov_ref, o_ref, *, n_sel):
    bi = pl.program_id(1)
    g = pl.program_id(2)
    q0 = bi * Q_BLOCK
    rows = NSA_GRP * Q_BLOCK
    ncp = kc_ref.shape[1]
    nsp = ov_ref.shape[1]

    lane = lax.broadcasted_iota(I32, (Q_BLOCK, LANES), 1)
    in_g = (lane // NSA_HD) == g
    qf = q_ref[0]
    parts = []
    for r in range(NSA_GRP):
        blk = qf[:, (r // 2) * LANES:(r // 2 + 1) * LANES]
        blk = jnp.where((r % 2) == g, blk, pltpu.roll(blk, NSA_HD, axis=1))
        parts.append(jnp.where(in_g, blk, 0.0))
    qz = jnp.concatenate(parts, axis=0).astype(BF16)

    i_row = lax.broadcasted_iota(I32, (rows, 1), 0) % Q_BLOCK
    qpos = q0 + i_row

    lc = _dot_nt(qz, kc_ref[0])
    shift = (bi * (Q_BLOCK // CMP_STRIDE) + ncp - _CB_SHIFT) % ncp
    lc = lc + pltpu.roll(cb_ref[0], shift, axis=1)
    cpos = lax.broadcasted_iota(I32, (1, ncp), 1) * CMP_STRIDE + (CMP_LEN - 1)
    pc = _masked_softmax(lc, cpos <= qpos)
    oc = _dot(pc.astype(BF16), vc_ref[0])

    pcs = pc[0:Q_BLOCK]
    for r in range(1, NSA_GRP):
        pcs = pcs + pc[r * Q_BLOCK:(r + 1) * Q_BLOCK]
    hi = pcs.astype(BF16)
    lo = (pcs - hi.astype(F32)).astype(BF16)
    ssum = _dot(hi, ov_ref[...]) + _dot(lo, ov_ref[...])
    blk_id = lax.broadcasted_iota(I32, (Q_BLOCK, nsp), 1)
    cur = (q0 + lax.broadcasted_iota(I32, (Q_BLOCK, 1), 0)) // SEL_BLK
    forced = (blk_id == 0) | (blk_id == cur) | (blk_id == cur - 1)
    score = jnp.where(forced, FORCE_SCORE, jnp.where(blk_id <= cur, ssum, -1.0))
    selm, _ = _topk_select(score, min(TOP_N, n_sel))
    selm = selm.astype(BF16)

    def expand(first_blk, ncols):
        s_id = lax.broadcasted_iota(I32, (nsp, ncols), 0)
        c_id = lax.broadcasted_iota(I32, (nsp, ncols), 1)
        e = jnp.where(s_id == first_blk + c_id // SEL_BLK, 1.0, 0.0).astype(BF16)
        m = _dot(selm, e) > 0.5
        return jnp.concatenate([m] * NSA_GRP, axis=0)

    def far_body(t, carry):
        k0 = pl.multiple_of(t * _FAR_TK, _FAR_TK)
        kt = sel_ref[0, pl.ds(k0, _FAR_TK), 0:LANES]
        vt = sel_ref[0, pl.ds(k0, _FAR_TK), LANES:2 * LANES]
        s = _dot_nt(qz, kt)
        kpos = k0 + lax.broadcasted_iota(I32, (1, _FAR_TK), 1)
        mask = expand(t * (_FAR_TK // SEL_BLK), _FAR_TK) & (kpos < q0 - Q_BLOCK)
        return _online_update(carry, s, mask, vt)

    init = (jnp.full((rows, 1), NEG, F32), jnp.zeros((rows, 1), F32), jnp.zeros((rows, LANES), F32))
    carry = lax.fori_loop(0, bi // 2, far_body, init)

    p0 = pl.multiple_of(jnp.maximum(q0 - Q_BLOCK, 0), Q_BLOCK)
    d0 = pl.multiple_of(q0, Q_BLOCK)
    kn = jnp.concatenate([sel_ref[0, pl.ds(p0, Q_BLOCK), 0:LANES], sel_ref[0, pl.ds(d0, Q_BLOCK), 0:LANES]], axis=0)
    vn = jnp.concatenate([sel_ref[0, pl.ds(p0, Q_BLOCK), LANES:2 * LANES],
                          sel_ref[0, pl.ds(d0, Q_BLOCK), LANES:2 * LANES]], axis=0)
    wcols = wb_ref.shape[2]
    s = _dot_nt(qz, kn) + wb_ref[0, :, wcols - 2 * Q_BLOCK:wcols]
    kpos = q0 - Q_BLOCK + lax.broadcasted_iota(I32, (1, 2 * Q_BLOCK), 1)
    mask = expand(2 * bi - 2, 2 * Q_BLOCK) & (kpos <= qpos)
    _, l, acc = _online_update(carry, s, mask, vn)
    osel = acc / jnp.maximum(l, 1e-30)

    ks, vs = [], []
    for t in range(_WIN_TILES):
        st = pl.multiple_of(jnp.maximum(q0 - (_WIN_TILES - 1 - t) * Q_BLOCK, 0), Q_BLOCK)
        ks.append(win_ref[0, pl.ds(st, Q_BLOCK), 0:LANES])
        vs.append(win_ref[0, pl.ds(st, Q_BLOCK), LANES:2 * LANES])
    n_far = _WIN_TILES - 2
    s = jnp.concatenate([_dot_nt(qz, jnp.concatenate(ks[:n_far], axis=0)),
                         _dot_nt(qz, jnp.concatenate(ks[n_far:], axis=0))], axis=1) + wb_ref[0]
    wcol = lax.broadcasted_iota(I32, (1, wcols), 1)
    wpos = q0 - (_WIN_TILES - 1) * Q_BLOCK + wcol
    dw = qpos - wpos
    pw = _masked_softmax(s, (dw >= 0) & (dw < WINDOW) & (wpos >= 0))
    ow = _dot(pw.astype(BF16), jnp.concatenate(vs, axis=0))

    gates = gate_ref[0]
    heads = []
    for r in range(NSA_GRP):
        col = (g * NSA_GRP + r) * N_BRANCH
        sl = slice(r * Q_BLOCK, (r + 1) * Q_BLOCK)
        gsel = []
        for br in range(N_BRANCH):
            gsel.append(jnp.sum(jnp.where(lane == col + br, gates, 0.0), axis=-1, keepdims=True))
        heads.append(gsel[0] * oc[sl] + gsel[1] * osel[sl] + gsel[2] * ow[sl])
    out = []
    for j in range(NSA_GRP // 2):
        a, b = heads[2 * j], heads[2 * j + 1]
        a = jnp.where(g == 0, a, pltpu.roll(a, NSA_HD, axis=1))
        b = jnp.where(g == 1, b, pltpu.roll(b, NSA_HD, axis=1))
        out.append(jnp.where(lane < NSA_HD, a, b))
    o_ref[0] = jnp.concatenate(out, axis=1).astype(BF16)


def _nsa_prompt(q, kc, vc, selb, winb, gates, wb, cb, ov):
    b, t, _ = q.shape
    ncp = kc.shape[1]
    gw = NSA_GRP * NSA_HD
    return pl.pallas_call(
        functools.partial(_nsa_prompt_kernel, n_sel=t // SEL_BLK),
        grid=(b, t // Q_BLOCK, NSA_KV),
        in_specs=[pl.BlockSpec((1, Q_BLOCK, gw), lambda i, j, g: (i, j, g)),
                  pl.BlockSpec((1, ncp, LANES), lambda i, j, g: (i, 0, 0)),
                  pl.BlockSpec((1, ncp, LANES), lambda i, j, g: (i, 0, 0)),
                  pl.BlockSpec((1, t, KV_W), lambda i, j, g: (i, 0, 0)),
                  pl.BlockSpec((1, t, KV_W), lambda i, j, g: (i, 0, 0)),
                  pl.BlockSpec((1, Q_BLOCK, LANES), lambda i, j, g: (i, j, 0)),
                  pl.BlockSpec((1,) + wb.shape[1:], lambda i, j, g: (g, 0, 0)),
                  pl.BlockSpec((1,) + cb.shape[1:], lambda i, j, g: (g, 0, 0)),
                  pl.BlockSpec(ov.shape, lambda i, j, g: (0, 0))],
        out_specs=pl.BlockSpec((1, Q_BLOCK, gw), lambda i, j, g: (i, j, g)),
        out_shape=jax.ShapeDtypeStruct((b, t, NSA_KV * gw), BF16),
        compiler_params=_cparams(("parallel", "parallel", "arbitrary")),
    )(q, kc, vc, selb, winb, gates, wb, cb, ov)


def _mem_kv_kernel(m_ref, g_ref, w_ref, bd_ref, gk_ref, o_ref):
    x = m_ref[...]
    xn = (x * lax.rsqrt(jnp.mean(x * x, axis=-1, keepdims=True) + EPS) * g_ref[...]).astype(BF16)
    z = _dot(xn, w_ref[...])
    half = MEM_HEADS * MEM_HD
    kn = _group_rms(z[:, :half], bd_ref[...], gk_ref[...], MEM_HD)
    o_ref[...] = jnp.concatenate([kn, z[:, half:]], axis=1)


def _mem_kv(mem2d, prm):
    n = mem2d.shape[0]
    args = [mem2d, prm['norm_mem'], prm['w_mem_kv'], prm['bd_m'], prm['gmk']]
    w = 2 * MEM_HEADS * MEM_HD
    return pl.pallas_call(
        _mem_kv_kernel,
        grid=(1,),
        in_specs=[_full(a.shape) for a in args],
        out_specs=_full((n, w)),
        out_shape=jax.ShapeDtypeStruct((n, w), F32),
        compiler_params=_cparams(("arbitrary",)),
    )(*args)


def _mem_attn_kernel(q_ref, kv_ref, o_ref):
    half = MEM_HEADS * MEM_HD
    out = []
    for h in range(MEM_HEADS):
        sl = slice(h * MEM_HD, (h + 1) * MEM_HD)
        k = kv_ref[0, :, sl].astype(BF16)
        v = kv_ref[0, :, half + h * MEM_HD:half + (h + 1) * MEM_HD].astype(BF16)
        s = _dot_nt(q_ref[0, :, sl], k) * (MEM_HD ** -0.5)
        e = jnp.exp(s - jnp.max(s, axis=-1, keepdims=True))
        p = e / jnp.sum(e, axis=-1, keepdims=True)
        out.append(_dot(p.astype(BF16), v))
    o_ref[0] = jnp.concatenate(out, axis=1).astype(BF16)


def _mem_attn(mq, mkv, tq):
    b, t, w = mq.shape
    m = mkv.shape[1]
    return pl.pallas_call(
        _mem_attn_kernel,
        grid=(b, t // tq),
        in_specs=[pl.BlockSpec((1, tq, w), lambda i, j: (i, j, 0)),
                  pl.BlockSpec((1, m, 2 * w), lambda i, j: (i, 0, 0))],
        out_specs=pl.BlockSpec((1, tq, w), lambda i, j: (i, j, 0)),
        out_shape=jax.ShapeDtypeStruct((b, t, w), BF16),
        compiler_params=_cparams(("parallel", "parallel")),
    )(mq, mkv)


def _row_bias(dist, tab_ref):
    bucket = _t5_bucket(dist)
    acc = jnp.zeros((NSA_HEADS, dist.shape[1]), F32)
    for bk in range(REL_BUCKETS):
        acc = jnp.where(bucket == bk, tab_ref[:, bk:bk + 1], acc)
    return acc


def _sample_score_kernel(qz_ref, kc_ref, vc_ref, tab_ref, ov_ref, ex_ref, oc_ref, idx_ref, idxe_ref, *, past, n_sel):
    ncp = kc_ref.shape[1]
    nsp = ov_ref.shape[1]
    qz = qz_ref[0].astype(BF16)
    lc = _dot_nt(qz, kc_ref[0])
    cpos = lax.broadcasted_iota(I32, (1, ncp), 1) * CMP_STRIDE + (CMP_LEN - 1)
    lc = lc + _row_bias(past - cpos, tab_ref)
    pc = _masked_softmax(lc, jnp.broadcast_to(cpos <= past, lc.shape))
    oc_ref[0] = _dot(pc.astype(BF16), vc_ref[0])
    hrow = lax.broadcasted_iota(I32, (NSA_HEADS, 1), 0)
    rows = []
    for g in range(NSA_KV):
        rows.append(jnp.sum(jnp.where(hrow // NSA_GRP == g, pc, 0.0), axis=0, keepdims=True))
    rows.append(jnp.zeros((NSA_HEADS - NSA_KV, ncp), F32))
    pcs = jnp.concatenate(rows, axis=0)
    hi = pcs.astype(BF16)
    lo = (pcs - hi.astype(F32)).astype(BF16)
    ssum = _dot(hi, ov_ref[...]) + _dot(lo, ov_ref[...])
    blk_id = lax.broadcasted_iota(I32, (NSA_HEADS, nsp), 1)
    cur = past // SEL_BLK
    forced = (blk_id == 0) | (blk_id == cur) | (blk_id == cur - 1)
    score = jnp.where(forced, FORCE_SCORE, jnp.where(blk_id <= cur, ssum, -1.0))
    score = jnp.where(blk_id < n_sel, score, -jnp.inf)
    _, idx = _topk_select(score, min(TOP_N, n_sel))
    idx_ref[0] = idx.astype(I32)
    idxe_ref[0] = _dot(idx.astype(BF16), ex_ref[...])


def _sample_score(qz, kc, vc, tab8, ov, ex, past, n_sel):
    nb = qz.shape[0]
    ncp = kc.shape[1]
    blk8 = lambda w: pl.BlockSpec((1, NSA_HEADS, w), lambda i: (i, 0, 0))
    ncols = ex.shape[1]
    return pl.pallas_call(
        functools.partial(_sample_score_kernel, past=past, n_sel=n_sel),
        grid=(nb,),
        in_specs=[blk8(LANES), pl.BlockSpec((1, ncp, LANES), lambda i: (i, 0, 0)),
                  pl.BlockSpec((1, ncp, LANES), lambda i: (i, 0, 0)), _full(tab8.shape), _full(ov.shape),
                  _full(ex.shape)],
        out_specs=[blk8(LANES), blk8(LANES), blk8(ncols)],
        out_shape=[jax.ShapeDtypeStruct((nb, NSA_HEADS, LANES), F32),
                   jax.ShapeDtypeStruct((nb, NSA_HEADS, LANES), I32),
                   jax.ShapeDtypeStruct((nb, NSA_HEADS, ncols), F32)],
        compiler_params=_cparams(("parallel",)),
    )(qz, kc, vc, tab8, ov, ex)


def _sample_attn_kernel(pt_ref, idx_ref, pool_ref, qz_ref, idxe_ref, oc_ref, gate_ref, selnew_ref, winnew_ref,
                        win_ref, mq_ref, mem_ref, tab_ref, onsa_ref, omem_ref, wout_ref, buf, sem, *, past):
    b = pl.program_id(0)
    n_top = buf.shape[0] // NSA_KV
    cur = past // SEL_BLK
    blocks_pp = pool_ref.shape[1] // SEL_BLK

    def blk_copy(j):
        blk = jnp.minimum(idx_ref[b, j], cur - 1)
        page = pt_ref[b, blk // blocks_pp]
        r0 = pl.multiple_of((blk % blocks_pp) * SEL_BLK, SEL_BLK)
        return pltpu.make_async_copy(pool_ref.at[page, pl.ds(r0, SEL_BLK)], buf.at[j], sem)

    def start(j, c):
        blk_copy(j).start()
        return c

    def wait(j, c):
        blk_copy(j).wait()
        return c

    lax.fori_loop(0, NSA_KV * n_top, start, 0)

    qz = qz_ref[0]
    qzb = qz.astype(BF16)
    hrow = lax.broadcasted_iota(I32, (NSA_HEADS, 1), 0)
    lane = lax.broadcasted_iota(I32, (NSA_HEADS, KV_W), 1)
    v_lanes = (lane // NSA_HD) == (NSA_KV + hrow // NSA_GRP)
    tab0 = tab_ref[:, 0:1]

    xw = win_ref[0]
    wb_rows = xw.shape[0]
    xwb = xw.astype(BF16)
    wn = winnew_ref[0]
    j = lax.broadcasted_iota(I32, (1, wb_rows), 1)
    dw = wb_rows - j
    s = _dot_nt(qzb, xwb) + _row_bias(dw, tab_ref)
    s_new = jnp.sum(qz * wn, axis=-1, keepdims=True) + tab0
    mask = jnp.broadcast_to(dw < WINDOW, s.shape)
    zm = jnp.where(mask, s, NEG)
    m = jnp.maximum(jnp.max(zm, axis=-1, keepdims=True), s_new)
    e = jnp.exp(zm - m) * mask.astype(F32)
    e_new = jnp.exp(s_new - m)
    ow = (_dot(e.astype(BF16), xwb) + e_new * wn) / jnp.maximum(jnp.sum(e, axis=-1, keepdims=True) + e_new, 1e-30)
    wout_ref[0, 0:wb_rows - 1, :] = win_ref[0, 1:wb_rows, :]
    wout_ref[0, wb_rows - 1:wb_rows, :] = wn

    xm = mem_ref[0].astype(BF16)
    half = MEM_HEADS * MEM_HD
    sm = _dot_nt(mq_ref[0].astype(BF16), xm[:, :half]) * (MEM_HD ** -0.5)
    em = jnp.exp(sm - jnp.max(sm, axis=-1, keepdims=True))
    pm = em / jnp.sum(em, axis=-1, keepdims=True)
    omem_ref[0] = _dot(pm.astype(BF16), xm[:, half:])

    lax.fori_loop(0, NSA_KV * n_top, wait, 0)
    sn = selnew_ref[0]
    s_new = jnp.sum(qz * sn, axis=-1, keepdims=True) + tab0
    ncols = n_top * SEL_BLK
    col = lax.broadcasted_iota(I32, (1, ncols), 1)
    osel = jnp.zeros((NSA_HEADS, KV_W), F32)
    for g in range(NSA_KV):
        xg = buf[g * n_top:(g + 1) * n_top].reshape(ncols, KV_W).astype(BF16)
        blk = idxe_ref[0, g:g + 1, :].astype(I32)
        spos = blk * SEL_BLK + col % SEL_BLK
        sg = _dot_nt(qzb, xg) + _row_bias(past - spos, tab_ref)
        mask = jnp.broadcast_to(blk < cur, sg.shape)
        new_sel = jnp.max(jnp.where(blk == cur, 1.0, 0.0), axis=-1, keepdims=True) > 0.5
        zm = jnp.where(mask, sg, NEG)
        z_new = jnp.where(new_sel, s_new, NEG)
        m = jnp.maximum(jnp.max(zm, axis=-1, keepdims=True), z_new)
        e = jnp.exp(zm - m) * mask.astype(F32)
        e_new = jnp.exp(z_new - m) * new_sel.astype(F32)
        og = (_dot(e.astype(BF16), xg) + e_new * sn) / jnp.maximum(jnp.sum(e, axis=-1, keepdims=True) + e_new, 1e-30)
        osel = jnp.where(hrow // NSA_GRP == g, og, osel)

    gates = gate_ref[0]
    oc = oc_ref[0]
    oc = jnp.concatenate([jnp.zeros_like(oc), oc], axis=1)
    o = gates[:, 0:1] * oc + gates[:, 1:2] * osel + gates[:, 2:3] * ow
    onsa_ref[0] = jnp.where(v_lanes, o, 0.0)


def _sample_attn(table, idx2d, pool, qz256, idxe, oc, gate8, selnew, winnew, win, mq8, mem, tab8, past, n_top):
    nb = qz256.shape[0]
    wb_rows = win.shape[1]
    blk3 = lambda a: pl.BlockSpec((1,) + a.shape[1:], lambda i, pt, ix: (i, 0, 0))
    ins = [qz256, idxe, oc, gate8, selnew, winnew, win, mq8, mem]
    gs = pltpu.PrefetchScalarGridSpec(
        num_scalar_prefetch=2,
        grid=(nb,),
        in_specs=[pl.BlockSpec(memory_space=pl.ANY)] + [blk3(a) for a in ins]
                 + [pl.BlockSpec(tab8.shape, lambda i, pt, ix: (0, 0))],
        out_specs=[pl.BlockSpec((1, NSA_HEADS, KV_W), lambda i, pt, ix: (i, 0, 0)),
                   pl.BlockSpec((1, NSA_HEADS, MEM_HEADS * MEM_HD), lambda i, pt, ix: (i, 0, 0)),
                   pl.BlockSpec((1, wb_rows, KV_W), lambda i, pt, ix: (i, 0, 0))],
        scratch_shapes=[pltpu.VMEM((NSA_KV * n_top, SEL_BLK, KV_W), F32), pltpu.SemaphoreType.DMA(())],
    )
    return pl.pallas_call(
        functools.partial(_sample_attn_kernel, past=past),
        grid_spec=gs,
        out_shape=[jax.ShapeDtypeStruct((nb, NSA_HEADS, KV_W), F32),
                   jax.ShapeDtypeStruct((nb, NSA_HEADS, MEM_HEADS * MEM_HD), F32),
                   jax.ShapeDtypeStruct((nb, wb_rows, KV_W), F32)],
        compiler_params=_cparams(("arbitrary",)),
    )(table, idx2d, pool, *ins, tab8)


def _merge_ffn_kernel(x_ref, ol_ref, on_ref, om_ref, mg_ref, wa_ref, wb_ref, wm_ref, wo_ref, gf_ref, wg_ref,
                      wu_ref, wd_ref, y_ref):
    d = x_ref.shape[1]
    mg = mg_ref[...]
    z = (mg[:, 0:d] * _dot(ol_ref[...], wa_ref[...]) + mg[:, d:2 * d] * _dot(on_ref[...], wb_ref[...])
         + mg[:, 2 * d:3 * d] * _dot(om_ref[...], wm_ref[...]))
    h = x_ref[...] + _dot(z.astype(BF16), wo_ref[...])
    f = (h * lax.rsqrt(jnp.mean(h * h, axis=-1, keepdims=True) + EPS) * gf_ref[...]).astype(BF16)
    a = jax.nn.silu(_dot(f, wg_ref[...])) * _dot(f, wu_ref[...])
    y_ref[...] = h + _dot(a.astype(BF16), wd_ref[...])


def _merge_ffn(x2d, ol, on, om, mg, prm, tm):
    n, d = x2d.shape
    row = lambda a: pl.BlockSpec((tm, a.shape[1]), lambda i: (i, 0))
    consts = [prm['w_up_a'], prm['w_up_b'], prm['w_up_m'], prm['w_o'], prm['norm_ffn'], prm['w_ffn_gate'],
              prm['w_ffn_up'], prm['w_ffn_down']]
    acts = [x2d, ol, on, om, mg]
    return pl.pallas_call(
        _merge_ffn_kernel,
        grid=(n // tm,),
        in_specs=[row(a) for a in acts] + [pl.BlockSpec(c.shape, lambda i: (0, 0), pipeline_mode=pl.Buffered(1))
                                           for c in consts],
        out_specs=pl.BlockSpec((tm, d), lambda i: (i, 0)),
        out_shape=jax.ShapeDtypeStruct((n, d), F32),
        compiler_params=_cparams(("parallel",)),
    )(*acts, *consts)


def _round_up(n, m):
    return -(-n // m) * m


def _overlap(ncp, n_cmp, nsp):
    cs = jnp.arange(ncp)[:, None] * CMP_STRIDE
    ss = jnp.arange(nsp)[None, :] * SEL_BLK
    hit = (cs < ss + SEL_BLK) & (cs + CMP_LEN > ss) & (jnp.arange(ncp)[:, None] < n_cmp)
    return hit.astype(BF16)


def _prep_params(norm_mix, w_in, conv_w, conv_b, w_lru_a, b_lru_a, w_lru_i, b_lru_i, lru_lambda, g_nsa_q, g_nsa_k,
                 pe_cmp_k, w_cmp_k1, w_cmp_k2, pe_cmp_v, w_cmp_v1, w_cmp_v2, norm_mem, w_mem_kv, g_mem_q, g_mem_k,
                 w_up_a, w_up_b, w_up_m, w_o, norm_ffn, w_ffn_gate, w_ffn_up, w_ffn_down):
    row = lambda v: v.reshape(1, -1).astype(F32)
    n_ng = N_BRANCH * NSA_HEADS
    w = w_in
    w_packed = jnp.concatenate(
        [w[:, :_OFF_NG], w[:, _OFF_NG:_OFF_NG + n_ng], jnp.zeros((w.shape[0], LANES - n_ng), w.dtype),
         w[:, _OFF_NG + n_ng:]], axis=1).astype(BF16)
    eye = jnp.eye(NSA_KV, dtype=F32)

    def w1_big(w1):
        wr = w1.reshape(2, CMP_STRIDE, NSA_HD, CMP_HID)
        big = jnp.einsum('hrdj,gk->rgdhkj', wr, eye)
        return big.reshape(CMP_STRIDE * NSA_KV * NSA_HD, 2 * NSA_KV * CMP_HID).astype(BF16)

    def w2_bd(w2):
        return jnp.einsum('jd,gk->gjkd', w2, eye).reshape(NSA_KV * CMP_HID, NSA_KV * NSA_HD).astype(BF16)

    pe8 = lambda pe: jnp.broadcast_to(pe.reshape(1, -1), (8, pe.size)).astype(BF16)
    return dict(
        norm_mix=row(norm_mix), w_in=w_packed,
        bd_q=_block_diag_ones(NSA_HEADS * NSA_HD, NSA_HD), bd_k=_block_diag_ones(NSA_KV * NSA_HD, NSA_HD),
        bd_m=_block_diag_ones(MEM_HEADS * MEM_HD, MEM_HD),
        gq=row(jnp.tile(g_nsa_q, NSA_HEADS)), gk_cmp=row(jnp.tile(g_nsa_k[0], NSA_KV)),
        gk_sel=row(jnp.tile(g_nsa_k[1], NSA_KV)), gk_win=row(jnp.tile(g_nsa_k[2], NSA_KV)),
        gmq=row(jnp.tile(g_mem_q, MEM_HEADS)), gmk=row(jnp.tile(g_mem_k, MEM_HEADS)),
        conv_w=conv_w.astype(F32), conv_b=row(conv_b), w_lru_a=w_lru_a.astype(BF16), w_lru_i=w_lru_i.astype(BF16),
        b_lru_a=row(b_lru_a), b_lru_i=row(b_lru_i), lru_lambda=row(lru_lambda),
        w1k_big=w1_big(w_cmp_k1), w1v_big=w1_big(w_cmp_v1), w1k=w_cmp_k1.astype(BF16), w1v=w_cmp_v1.astype(BF16),
        pe_k=pe8(pe_cmp_k), pe_v=pe8(pe_cmp_v), w2k_bd=w2_bd(w_cmp_k2), w2v_bd=w2_bd(w_cmp_v2),
        norm_mem=row(norm_mem), w_mem_kv=w_mem_kv.astype(BF16),
        w_up_a=w_up_a.astype(BF16), w_up_b=w_up_b.astype(BF16), w_up_m=w_up_m.astype(BF16), w_o=w_o.astype(BF16),
        norm_ffn=row(norm_ffn), w_ffn_gate=w_ffn_gate.astype(BF16), w_ffn_up=w_ffn_up.astype(BF16),
        w_ffn_down=w_ffn_down.astype(BF16))


def _head_rows(q2d, width):
    n = q2d.shape[0]
    qh = q2d.reshape(n, NSA_KV, NSA_GRP, NSA_HD)
    out = jnp.zeros((n, NSA_KV, NSA_GRP, width // NSA_HD, NSA_HD), q2d.dtype)
    for g in range(NSA_KV):
        out = out.at[:, g, :, g, :].set(qh[:, g])
    return out.reshape(n, NSA_HEADS, width)


def kernel(x_prompt, x_sample, mem_prompt, cache_cmp_kv, cache_sel_kv, page_table, cache_win_kv, cache_mem_kv,
           state_lru_h, state_conv, rel_bias, norm_mix, w_in, conv_w, conv_b, w_lru_a, b_lru_a, w_lru_i, b_lru_i,
           lru_lambda, g_nsa_q, g_nsa_k, pe_cmp_k, w_cmp_k1, w_cmp_k2, pe_cmp_v, w_cmp_v1, w_cmp_v2, norm_mem,
           w_mem_kv, g_mem_q, g_mem_k, w_up_a, w_up_b, w_up_m, w_o, norm_ffn, w_ffn_gate, w_ffn_up, w_ffn_down):
    assert norm_mix.shape[0] == 1 and x_sample.shape[1] == 1
    weights = (norm_mix, w_in, conv_w, conv_b, w_lru_a, b_lru_a, w_lru_i, b_lru_i, lru_lambda, g_nsa_q, g_nsa_k,
               pe_cmp_k, w_cmp_k1, w_cmp_k2, pe_cmp_v, w_cmp_v1, w_cmp_v2, norm_mem, w_mem_kv, g_mem_q, g_mem_k,
               w_up_a, w_up_b, w_up_m, w_o, norm_ffn, w_ffn_gate, w_ffn_up, w_ffn_down)
    prm = _prep_params(*[w[0] for w in weights])
    bsz, t, d = x_prompt.shape
    db = x_sample.shape[0]
    n_pages = page_table.shape[1]
    page_rows = cache_cmp_kv.shape[2]
    past = n_pages * page_rows
    assert t % (4 * Q_BLOCK) == 0 and page_rows % SEL_BLK == 0 and past % SEL_BLK == 0

    xp2 = x_prompt.reshape(bsz * t, d)
    (lx, gg, q, cmp_p, sel_p, win_p, selb, winb, ng, mq, mg) = _project(xp2, prm, 256)
    o_lru, h_p, cv_p = _lru_prompt(lx.reshape(bsz, t, d), gg.reshape(bsz, t, d), prm, 256)

    chunks_pp = page_rows // CMP_STRIDE
    pages_p = t // page_rows
    ncp_p = _round_up(t // CMP_STRIDE, LANES)
    pool_p = cmp_p.reshape(bsz * pages_p, chunks_pp, CHUNK_W)
    table_p = jnp.arange(bsz * pages_p, dtype=I32).reshape(bsz, pages_p)
    kc_p, vc_p = _compress(pool_p, table_p, prm, ncp_p)

    nsp_p = _round_up(t // SEL_BLK, LANES)
    wb, cb = _bias_tiles(rel_bias.astype(F32), ncp_p)
    ov_p = _overlap(ncp_p, t // CMP_STRIDE - 1, nsp_p)
    o_nsa = _nsa_prompt(q.reshape(bsz, t, -1), kc_p, vc_p, selb.reshape(bsz, t, KV_W), winb.reshape(bsz, t, KV_W),
                        ng.reshape(bsz, t, LANES), wb, cb, ov_p)

    m_rows = mem_prompt.shape[1]
    mkv = _mem_kv(mem_prompt.reshape(bsz * m_rows, d), prm)
    o_mem = _mem_attn(mq.reshape(bsz, t, -1), mkv.reshape(bsz, m_rows, -1), 256)

    y_p = _merge_ffn(xp2, o_lru.reshape(bsz * t, d), o_nsa.reshape(bsz * t, -1), o_mem.reshape(bsz * t, -1), mg,
                     prm, 256)

    xs2 = x_sample.reshape(db, d)
    (lx_s, gg_s, q_s, cmp_s, sel_s, win_s, _, _, ng_s, mq_s, mg_s) = _project(xs2, prm, db)
    o_lru_s, h_s, cv_s = _lru_sample(lx_s, gg_s, state_conv[0].reshape(db, -1), state_lru_h[0], prm)

    ncp_s = _round_up(past // CMP_STRIDE, LANES)
    pool_c = cache_cmp_kv[0].reshape(-1, chunks_pp, CHUNK_W)
    kc_s, vc_s = _compress(pool_c, page_table, prm, ncp_s)

    n_sel = -(-(past + 1) // SEL_BLK)
    nsp_s = _round_up(n_sel, LANES)
    n_top = min(TOP_N, n_sel)
    ov_s = _overlap(ncp_s, (past + 1) // CMP_STRIDE - 1, nsp_s)
    tab8 = jnp.zeros((NSA_HEADS, LANES), F32).at[:, :REL_BUCKETS].set(rel_bias.astype(F32).T)
    ex = (jnp.arange(LANES)[:, None] == (jnp.arange(n_top * SEL_BLK) // SEL_BLK)[None, :]).astype(BF16)
    oc_s, idx, idxe = _sample_score(_head_rows(q_s, LANES), kc_s, vc_s, tab8, ov_s, ex, past, n_sel)
    idx2d = idx[:, :NSA_KV, :n_top].reshape(db, NSA_KV * n_top)

    gate8 = jnp.zeros((db, NSA_HEADS, LANES), F32).at[:, :, :N_BRANCH].set(
        ng_s[:, :N_BRANCH * NSA_HEADS].reshape(db, NSA_HEADS, N_BRANCH))
    mq8 = jnp.zeros((db, NSA_HEADS, MEM_HEADS, MEM_HD), F32)
    mqh = mq_s.astype(F32).reshape(db, MEM_HEADS, MEM_HD)
    for h in range(MEM_HEADS):
        mq8 = mq8.at[:, h, h, :].set(mqh[:, h])
    mq8 = mq8.reshape(db, NSA_HEADS, MEM_HEADS * MEM_HD)
    pool_s = cache_sel_kv[0].reshape(-1, page_rows, KV_W)
    wb_rows = cache_win_kv.shape[2]
    o_nsa8, o_mem8, win_new = _sample_attn(
        page_table, idx2d, pool_s, _head_rows(q_s, KV_W), idxe, oc_s, gate8, sel_s.reshape(db, 1, KV_W),
        win_s.reshape(db, 1, KV_W), cache_win_kv[0].reshape(db, wb_rows, KV_W), mq8,
        cache_mem_kv[0].reshape(db, cache_mem_kv.shape[2], -1), tab8, past, n_top)
    o5 = o_nsa8.reshape(db, NSA_KV, NSA_GRP, 2 * NSA_KV, NSA_HD)
    o_nsa_s = jnp.stack([o5[:, g, :, NSA_KV + g, :] for g in range(NSA_KV)], axis=1).reshape(db, -1)
    m5 = o_mem8.reshape(db, NSA_HEADS, MEM_HEADS, MEM_HD)
    o_mem_s = jnp.stack([m5[:, h, h, :] for h in range(MEM_HEADS)], axis=1).reshape(db, -1)
    y_s = _merge_ffn(xs2, o_lru_s, o_nsa_s.astype(BF16), o_mem_s.astype(BF16), mg_s, prm, db)

    kv6 = lambda a, lead: a.reshape((1,) + lead + (2, NSA_KV, NSA_HD))
    w_keep = min(WINDOW, t)
    return (y_p.reshape(bsz, t, d), y_s.reshape(db, 1, d),
            kv6(cmp_p, (bsz, t)), kv6(cmp_s, (db, 1)), kv6(sel_p, (bsz, t)), kv6(sel_s, (db, 1)),
            kv6(win_p.reshape(bsz, t, KV_W)[:, t - w_keep:], (bsz, w_keep)), kv6(win_new, (db, wb_rows)),
            mkv.reshape(1, bsz, m_rows, 2, MEM_HEADS, MEM_HD),
            h_p.reshape(1, bsz, d), h_s.reshape(1, db, d),
            cv_p.reshape(1, bsz, CONV_W - 1, d), cv_s.reshape(1, db, CONV_W - 1, d))
```

```python
import functools
import math

import jax
import jax.numpy as jnp
from jax import lax
from jax.experimental import pallas as pl
from jax.experimental.pallas import tpu as pltpu

F32 = jnp.float32
BF16 = jnp.bfloat16
I32 = jnp.int32

EPS = 1e-6
NEG = -1e30
FORCE_SCORE = 1e6
LOG2E = math.log2(math.e)
LRU_C = 8.0
LRU_BLOCKS = 4
CONV_W = 4
NSA_HEADS = 8
NSA_KV = 2
NSA_GRP = NSA_HEADS // NSA_KV
NSA_HD = 64
CMP_STRIDE = 16
CMP_LEN = 2 * CMP_STRIDE
CMP_HID = 2 * NSA_HD
SEL_BLK = 64
TOP_N = 16
WINDOW = 512
Q_BLOCK = 128
MEM_HEADS = 4
MEM_HD = 128
REL_BUCKETS = 32
REL_MAX_DIST = 128
N_BRANCH = 3

LANES = 128
KV_W = 2 * NSA_KV * NSA_HD
CHUNK_W = CMP_STRIDE * KV_W
VMEM_LIMIT = 56 * 1024 * 1024


def _cparams(sem):
    return pltpu.CompilerParams(dimension_semantics=sem, vmem_limit_bytes=VMEM_LIMIT)


def _full(shape):
    n = len(shape)
    return pl.BlockSpec(shape, lambda *_: (0,) * n)


def _dot(a, b):
    return jnp.dot(a, b, preferred_element_type=F32)


def _dot_nt(a, b):
    return lax.dot_general(a, b, (((1,), (1,)), ((), ())), preferred_element_type=F32)


def _group_rms(z, ones_bd, gain, width):
    ss = _dot((z * z).astype(BF16), ones_bd)
    return z * lax.rsqrt(ss * (1.0 / width) + EPS) * gain


def _masked_softmax(z, mask):
    zm = jnp.where(mask, z, NEG)
    e = jnp.exp(zm - jnp.max(zm, axis=-1, keepdims=True)) * mask.astype(F32)
    return e / jnp.maximum(jnp.sum(e, axis=-1, keepdims=True), 1e-30)


def _t5_bucket(dist):
    n = jnp.maximum(dist, 0)
    exact = REL_BUCKETS // 2
    nf = jnp.maximum(n, 1).astype(F32)
    large = exact + (jnp.log(nf / exact) / math.log(REL_MAX_DIST / exact) * (REL_BUCKETS - exact)).astype(I32)
    return jnp.where(n < exact, n, jnp.minimum(large, REL_BUCKETS - 1))


def _block_diag_ones(n, width):
    i = jnp.arange(n) // width
    return (i[:, None] == i[None, :]).astype(BF16)


def _split_bf16(x):
    hi = x.astype(BF16)
    return hi, (x - hi.astype(F32)).astype(BF16)


_D = 1024
_NG_W = NSA_KV * LANES
_OFF_LX, _OFF_LG, _OFF_Q, _OFF_KV, _OFF_NG, _OFF_MQ, _OFF_MG, _OFF_END = (
    0, 1024, 2048, 2560, 3328, 3584, 4096, 7168)


def _proj_kernel(x_ref, g_ref, w_ref, bdq_ref, bdk_ref, bdm_ref, gq_ref, gks_ref, gkw_ref, gmq_ref,
                 lx_ref, lg_ref, q_ref, cmp_ref, sel_ref, win_ref, selb_ref, winb_ref, ng_ref, mq_ref, mg_ref):
    x = x_ref[...]
    xn = (x * lax.rsqrt(jnp.mean(x * x, axis=-1, keepdims=True) + EPS) * g_ref[...]).astype(BF16)

    def seg(a, b):
        return _dot(xn, w_ref[:, a:b])

    lx_ref[...] = seg(_OFF_LX, _OFF_LG)
    lg_ref[...] = jax.nn.gelu(seg(_OFF_LG, _OFF_Q))
    zq = seg(_OFF_Q, _OFF_KV)
    q_ref[...] = _group_rms(zq, bdq_ref[...], gq_ref[...], NSA_HD) * (NSA_HD ** -0.5)
    cmp_ref[...] = seg(_OFF_KV, _OFF_KV + KV_W)
    for off, gain_ref, o_ref, ob_ref in ((_OFF_KV + KV_W, gks_ref, sel_ref, selb_ref),
                                         (_OFF_KV + 2 * KV_W, gkw_ref, win_ref, winb_ref)):
        z = seg(off, off + KV_W)
        kn = _group_rms(z[:, :LANES], bdk_ref[...], gain_ref[...], NSA_HD)
        kv = jnp.concatenate([kn, z[:, LANES:]], axis=1)
        o_ref[...] = kv
        ob_ref[...] = kv.astype(BF16)
    ng_ref[...] = jax.nn.sigmoid(seg(_OFF_NG, _OFF_MQ))
    zm = seg(_OFF_MQ, _OFF_MG)
    mq_ref[...] = _group_rms(zm, bdm_ref[...], gmq_ref[...], MEM_HD).astype(BF16)
    mg_ref[...] = jax.nn.sigmoid(seg(_OFF_MG, _OFF_END))


def _project(x2d, prm, tm):
    n = x2d.shape[0]
    row = lambda w: pl.BlockSpec((tm, w), lambda i: (i, 0))
    consts = [prm['norm_mix'], prm['w_in'], prm['bd_q'], prm['bd_k'], prm['bd_m'],
              prm['gq'], prm['gk_sel'], prm['gk_win'], prm['gmq']]
    widths = [(_D, F32), (_D, F32), (512, F32), (KV_W, F32), (KV_W, F32), (KV_W, F32), (KV_W, BF16), (KV_W, BF16),
              (_NG_W, F32), (512, BF16), (3 * _D, F32)]
    return pl.pallas_call(
        _proj_kernel,
        grid=(n // tm,),
        in_specs=[row(_D)] + [_full(c.shape) for c in consts],
        out_specs=[row(w) for w, _ in widths],
        out_shape=[jax.ShapeDtypeStruct((n, w), dt) for w, dt in widths],
        compiler_params=_cparams(("parallel",)),
    )(x2d, *consts)


def _lru_gates(xc, wa_ref, wi_ref, ba_ref, bi_ref, lam_ref):
    bw = xc.shape[1] // LRU_BLOCKS
    ra, ri = [], []
    for n in range(LRU_BLOCKS):
        xr = xc[:, n * bw:(n + 1) * bw].astype(BF16)
        ra.append(_dot(xr, wa_ref[n]))
        ri.append(_dot(xr, wi_ref[n]))
    r = jax.nn.sigmoid(jnp.concatenate(ra, axis=1) + ba_ref[...])
    i = jax.nn.sigmoid(jnp.concatenate(ri, axis=1) + bi_ref[...])
    lam = -lam_ref[...]
    softplus = jnp.maximum(lam, 0.0) + jnp.log1p(jnp.exp(-jnp.abs(lam)))
    log_a = -LRU_C * r * softplus
    a = jnp.exp(log_a)
    b = jnp.sqrt(jnp.tanh(-log_a) * (a * a + 1.0)) * (i * xc)
    return a, b


def _lru_scan(a, b):
    tt = a.shape[0]
    row = lax.broadcasted_iota(I32, a.shape, 0)
    k = 1
    while k < tt:
        keep = row >= k
        b = b + a * jnp.where(keep, pltpu.roll(b, k, axis=0), 0.0)
        a = a * jnp.where(keep, pltpu.roll(a, k, axis=0), 1.0)
        k *= 2
    return a, b


def _lru_prompt_kernel(x_ref, gg_ref, cw_ref, cb_ref, wa_ref, wi_ref, ba_ref, bi_ref, lam_ref,
                       o_ref, h_ref, cv_ref, xbuf, hcar):
    t = pl.program_id(1)
    tt = x_ref.shape[1]

    @pl.when(t == 0)
    def _():
        xbuf[0:8, :] = jnp.zeros((8, xbuf.shape[1]), F32)
        hcar[...] = jnp.zeros(hcar.shape, F32)

    x = x_ref[0]
    xbuf[8:8 + tt, :] = x
    xc = cb_ref[...] + xbuf[5:5 + tt, :] * cw_ref[0:1, :]
    for k in range(1, CONV_W):
        xc = xc + xbuf[5 + k:5 + k + tt, :] * cw_ref[k:k + 1, :]
    a, b = _lru_gates(xc, wa_ref, wi_ref, ba_ref, bi_ref, lam_ref)
    ap, hs = _lru_scan(a, b)
    h = hs + ap * hcar[...]
    o_ref[0] = (h * gg_ref[0]).astype(BF16)
    hcar[...] = h[tt - 1:tt, :]
    xbuf[0:8, :] = x[tt - 8:tt, :]

    @pl.when(t == pl.num_programs(1) - 1)
    def _():
        h_ref[0] = h[tt - 1:tt, :]
        cv_ref[0] = x[tt - (CONV_W - 1):tt, :]


def _lru_prompt(lx, gg, prm, tt):
    b, t, w = lx.shape
    blk = pl.BlockSpec((1, tt, w), lambda i, j: (i, j, 0))
    consts = [prm['conv_w'], prm['conv_b'], prm['w_lru_a'], prm['w_lru_i'], prm['b_lru_a'], prm['b_lru_i'],
              prm['lru_lambda']]
    return pl.pallas_call(
        _lru_prompt_kernel,
        grid=(b, t // tt),
        in_specs=[blk, blk] + [_full(c.shape) for c in consts],
        out_specs=[blk, pl.BlockSpec((1, 1, w), lambda i, j: (i, 0, 0)),
                   pl.BlockSpec((1, CONV_W - 1, w), lambda i, j: (i, 0, 0))],
        out_shape=[jax.ShapeDtypeStruct((b, t, w), BF16), jax.ShapeDtypeStruct((b, 1, w), F32),
                   jax.ShapeDtypeStruct((b, CONV_W - 1, w), F32)],
        scratch_shapes=[pltpu.VMEM((tt + 8, w), F32), pltpu.VMEM((1, w), F32)],
        compiler_params=_cparams(("parallel", "arbitrary")),
    )(lx, gg, *consts)


def _lru_sample_kernel(x_ref, gg_ref, cv0_ref, h0_ref, cw_ref, cb_ref, wa_ref, wi_ref, ba_ref, bi_ref, lam_ref,
                       o_ref, h_ref, cv_ref):
    x = x_ref[...]
    xc = cb_ref[...] + x * cw_ref[CONV_W - 1:CONV_W, :]
    for k in range(CONV_W - 1):
        xc = xc + cv0_ref[k] * cw_ref[k:k + 1, :]
    a, b = _lru_gates(xc, wa_ref, wi_ref, ba_ref, bi_ref, lam_ref)
    h = a * h0_ref[...] + b
    o_ref[...] = (h * gg_ref[...]).astype(BF16)
    h_ref[...] = h
    for k in range(CONV_W - 2):
        cv_ref[k] = cv0_ref[k + 1]
    cv_ref[CONV_W - 2] = x


def _lru_sample(lx, gg, cv0, h0, prm):
    n, w = lx.shape
    consts = [prm['conv_w'], prm['conv_b'], prm['w_lru_a'], prm['w_lru_i'], prm['b_lru_a'], prm['b_lru_i'],
              prm['lru_lambda']]
    args = [lx, gg, cv0, h0] + consts
    return pl.pallas_call(
        _lru_sample_kernel,
        grid=(1,),
        in_specs=[_full(a.shape) for a in args],
        out_specs=[_full((n, w)), _full((n, w)), _full((CONV_W - 1, n, w))],
        out_shape=[jax.ShapeDtypeStruct((n, w), BF16), jax.ShapeDtypeStruct((n, w), F32),
                   jax.ShapeDtypeStruct((CONV_W - 1, n, w), F32)],
        compiler_params=_cparams(("arbitrary",)),
    )(*args)


def _compress_kernel(pt_ref, pool_ref, w1k_ref, w1v_ref, w1kp_ref, w1vp_ref, pek_ref, pev_ref, w2k_ref, w2v_ref,
                     bdk_ref, gk_ref, kc_ref, vc_ref, buf, sem, *scratch, n_valid, feature_major):
    b = pl.program_id(0)
    nb = pl.num_programs(0)
    n_pages = pt_ref.shape[1]
    ncp = kc_ref.shape[1]

    def page_copy(step, slot, p):
        return pltpu.make_async_copy(pool_ref.at[pt_ref[step, p]], buf.at[slot, p], sem.at[slot])

    def fetch(step, slot):
        def body(p, c):
            page_copy(step, slot, p).start()
            return c
        lax.fori_loop(0, n_pages, body, 0)

    slot = b % 2

    @pl.when(b == 0)
    def _():
        fetch(0, 0)

    @pl.when(b + 1 < nb)
    def _():
        fetch(b + 1, 1 - slot)

    def wait_body(p, c):
        page_copy(b, slot, p).wait()
        return c

    lax.fori_loop(0, n_pages, wait_body, 0)

    if feature_major:
        zbuf, = scratch
        page_rows = buf.shape[3]
        n_ch = n_pages * page_rows // CMP_STRIDE

        def transpose_page(p, c):
            r0 = pl.multiple_of(p * page_rows, page_rows)
            for kv in range(2):
                zbuf[kv, pl.ds(r0, page_rows), :] = buf[slot, p, kv * LANES:(kv + 1) * LANES, :].T
            return c

        lax.fori_loop(0, n_pages, transpose_page, 0)

        def chunk_rows(kv):
            return jnp.concatenate(
                [zbuf[kv, pl.ds(r, n_ch, stride=CMP_STRIDE), :].astype(BF16) for r in range(CMP_STRIDE)], axis=1)
    else:
        rows_pp = buf.shape[2]
        n_ch = n_pages * rows_pp

        def chunk_rows(kv):
            return jnp.concatenate(
                [buf[slot, :, :, r * KV_W + kv * LANES:r * KV_W + (kv + 1) * LANES].reshape(n_ch, LANES).astype(BF16)
                 for r in range(CMP_STRIDE)], axis=1)

    row = lax.broadcasted_iota(I32, (n_ch, 1), 0)
    keep = row < n_valid
    outs = []
    for kv, w1_ref, w1p_ref, pe_ref, w2_ref in ((0, w1k_ref, w1kp_ref, pek_ref, w2k_ref),
                                                 (1, w1v_ref, w1vp_ref, pev_ref, w2v_ref)):
        hh = _dot(chunk_rows(kv), w1_ref[...])
        pos = _dot(pe_ref[...], w1p_ref[...])[0:1, :]
        pos = jnp.concatenate([pos, pos], axis=1)
        nh = NSA_KV * CMP_HID
        h = hh[:, :nh] + pltpu.roll(hh[:, nh:], n_ch - 1, axis=0) + pos
        outs.append(_dot(jax.nn.gelu(h).astype(BF16), w2_ref[...]))
    kc = _group_rms(outs[0], bdk_ref[...], gk_ref[...], NSA_HD)
    kc = jnp.where(keep, kc, 0.0).astype(BF16)
    vc = jnp.where(keep, outs[1], 0.0).astype(BF16)
    if ncp > n_ch:
        pad = jnp.zeros((ncp - n_ch, kc.shape[1]), BF16)
        kc = jnp.concatenate([kc, pad], axis=0)
        vc = jnp.concatenate([vc, pad], axis=0)
    kc_ref[0] = kc
    vc_ref[0] = vc


def _compress(pool, table, prm, ncp, feature_major):
    nb, n_pages = table.shape
    if feature_major:
        page_rows = pool.shape[2]
        n_ch = n_pages * page_rows // CMP_STRIDE
        scratch = [pltpu.VMEM((2, n_pages * page_rows, LANES), F32)]
    else:
        n_ch = n_pages * pool.shape[1]
        scratch = []
    consts = [prm['w1k_big'], prm['w1v_big'], prm['w1k'], prm['w1v'], prm['pe_k'], prm['pe_v'],
              prm['w2k_bd'], prm['w2v_bd'], prm['bd_k'], prm['gk_cmp']]
    out_blk = pl.BlockSpec((1, ncp, LANES), lambda i, pt: (i, 0, 0))
    gs = pltpu.PrefetchScalarGridSpec(
        num_scalar_prefetch=1,
        grid=(nb,),
        in_specs=[pl.BlockSpec(memory_space=pl.ANY)] + [pl.BlockSpec(c.shape, lambda i, pt, _n=len(c.shape): (0,) * _n)
                                                         for c in consts],
        out_specs=[out_blk, out_blk],
        scratch_shapes=[pltpu.VMEM((2, n_pages) + pool.shape[1:], F32), pltpu.SemaphoreType.DMA((2,))] + scratch,
    )
    return pl.pallas_call(
        functools.partial(_compress_kernel, n_valid=n_ch - 1, feature_major=feature_major),
        grid_spec=gs,
        out_shape=[jax.ShapeDtypeStruct((nb, ncp, LANES), BF16)] * 2,
        compiler_params=_cparams(("arbitrary",)),
    )(table, pool, *consts)


_WIN_TILES = WINDOW // Q_BLOCK + 1
_CB_SHIFT = 16
_CB_BAND_END = _CB_SHIFT + Q_BLOCK // CMP_STRIDE


def _bias_tiles_kernel(tab_ref, wb_ref, cb_ref):
    g = pl.program_id(0)
    wcols = wb_ref.shape[2]
    ncp = cb_ref.shape[2]
    i_w = lax.broadcasted_iota(I32, (Q_BLOCK, wcols), 0)
    c_w = lax.broadcasted_iota(I32, (Q_BLOCK, wcols), 1)
    d_w = (_WIN_TILES - 1) * Q_BLOCK + i_w - c_w
    off_w = jnp.where((d_w >= 0) & (d_w < WINDOW), 0.0, NEG)
    i_c = lax.broadcasted_iota(I32, (Q_BLOCK, ncp), 0)
    u_c = lax.broadcasted_iota(I32, (Q_BLOCK, ncp), 1)
    d_c = i_c - (CMP_LEN - 1) - CMP_STRIDE * (u_c - _CB_SHIFT)
    off_c = jnp.where(u_c < _CB_BAND_END, NEG, 0.0)
    for d, off, ref in ((d_w, off_w, wb_ref), (d_c, off_c, cb_ref)):
        bucket = _t5_bucket(d)
        for r in range(NSA_GRP):
            h = g * NSA_GRP + r
            acc = jnp.zeros(d.shape, F32)
            for bk in range(REL_BUCKETS):
                acc = jnp.where(bucket == bk, tab_ref[bk, h], acc)
            far = tab_ref[REL_BUCKETS - 1, h]
            val = (acc - far) * LOG2E
            if ref is wb_ref:
                val = val + off
            else:
                val = jnp.where(d >= 0, val, off)
            ref[0, r * Q_BLOCK:(r + 1) * Q_BLOCK, :] = val


def _bias_tiles(rel_bias, ncp):
    wcols = _WIN_TILES * Q_BLOCK
    rows = NSA_GRP * Q_BLOCK
    return pl.pallas_call(
        _bias_tiles_kernel,
        grid=(NSA_KV,),
        in_specs=[pl.BlockSpec(memory_space=pltpu.SMEM)],
        out_specs=[pl.BlockSpec((1, rows, wcols), lambda g: (g, 0, 0)),
                   pl.BlockSpec((1, rows, ncp), lambda g: (g, 0, 0))],
        out_shape=[jax.ShapeDtypeStruct((NSA_KV, rows, wcols), F32),
                   jax.ShapeDtypeStruct((NSA_KV, rows, ncp), F32)],
        compiler_params=_cparams(("arbitrary",)),
    )(rel_bias)


def _topk_select(score, n_top):
    lane = lax.broadcasted_iota(I32, score.shape, 1).astype(F32)
    big = float(score.shape[1])
    slot = lax.broadcasted_iota(I32, (score.shape[0], LANES), 1)
    idx = jnp.zeros((score.shape[0], LANES), F32)
    for it in range(n_top):
        m = jnp.max(score, axis=-1, keepdims=True)
        first = jnp.min(jnp.where(score == m, lane, big), axis=-1, keepdims=True)
        score = jnp.where(lane == first, -jnp.inf, score)
        idx = jnp.where(slot == it, first, idx)
    return idx


def _topk_mask_cols(score, n_top):
    sid = lax.broadcasted_iota(I32, score.shape, 0).astype(F32)
    big = float(score.shape[0])
    sel = jnp.zeros(score.shape, F32)
    for _ in range(n_top):
        m = jnp.max(score, axis=0, keepdims=True)
        first = jnp.min(jnp.where(score == m, sid, big), axis=0, keepdims=True)
        hit = sid == first
        sel = jnp.where(hit, 1.0, sel)
        score = jnp.where(hit, -jnp.inf, score)
    return sel


_FAR_TK = 256


def _nsa_prompt_kernel(q_ref, kc_ref, vc_ref, sel_ref, win_ref, gate_ref, wb_ref, cb_ref, ovt_ref, ex_ref, o_ref,
                       *, n_sel):
    bi = pl.program_id(1)
    g = pl.program_id(2)
    q0 = bi * Q_BLOCK
    ncp = kc_ref.shape[1]
    nsp = ovt_ref.shape[0]
    heads = range(NSA_GRP)

    lane = lax.broadcasted_iota(I32, (Q_BLOCK, LANES), 1)
    in_g = (lane // NSA_HD) == g
    qf = q_ref[0] * LOG2E
    qz = []
    for r in heads:
        blk = qf[:, (r // 2) * LANES:(r // 2 + 1) * LANES]
        blk = jnp.where((r % 2) == g, blk, pltpu.roll(blk, NSA_HD, axis=1))
        qz.append(jnp.where(in_g, blk, 0.0).astype(BF16))

    def rows(r):
        return slice(r * Q_BLOCK, (r + 1) * Q_BLOCK)

    kc = kc_ref[0]
    vc = vc_ref[0]
    shift = (bi * (Q_BLOCK // CMP_STRIDE) + ncp - _CB_SHIFT) % ncp
    c_id = lax.broadcasted_iota(I32, (1, ncp), 1)
    future = jnp.where(c_id >= (bi + 1) * (Q_BLOCK // CMP_STRIDE), NEG, 0.0)
    pcs = jnp.zeros((Q_BLOCK, ncp), F32)
    oc = []
    for r in heads:
        z = _dot_nt(qz[r], kc) + pltpu.roll(cb_ref[0, rows(r), :], shift, axis=1) + future
        m = jnp.max(z, axis=-1, keepdims=True)
        e = jnp.exp2(z - m)
        inv = jnp.where(m > 0.5 * NEG, 1.0 / jnp.sum(e, axis=-1, keepdims=True), 0.0)
        p = e * inv
        pcs = pcs + p
        oc.append(_dot(p.astype(BF16), vc))

    hi, lo = _split_bf16(pcs)
    ssum = _dot_nt(ovt_ref[...], hi) + _dot_nt(ovt_ref[...], lo)
    s_id = lax.broadcasted_iota(I32, (nsp, Q_BLOCK), 0)
    cur = (q0 + lax.broadcasted_iota(I32, (1, Q_BLOCK), 1)) // SEL_BLK
    forced = (s_id == 0) | (s_id == cur) | (s_id == cur - 1)
    score = jnp.where(forced, FORCE_SCORE, jnp.where(s_id <= cur, ssum, -1.0))
    selm = _topk_mask_cols(score, min(TOP_N, n_sel)).T
    blk_id = lax.broadcasted_iota(I32, (Q_BLOCK, nsp), 1)
    selm_near = selm.astype(BF16)
    selm_far = jnp.where(blk_id < 2 * bi - 2, selm, 0.0).astype(BF16)

    p0 = pl.multiple_of(jnp.maximum(q0 - Q_BLOCK, 0), Q_BLOCK)
    d0 = pl.multiple_of(q0, Q_BLOCK)
    kn = jnp.concatenate([sel_ref[0, pl.ds(p0, Q_BLOCK), 0:LANES], sel_ref[0, pl.ds(d0, Q_BLOCK), 0:LANES]], axis=0)
    vn = jnp.concatenate([sel_ref[0, pl.ds(p0, Q_BLOCK), LANES:2 * LANES],
                          sel_ref[0, pl.ds(d0, Q_BLOCK), LANES:2 * LANES]], axis=0)
    ex_near = jnp.concatenate([ex_ref[:, pl.ds(p0, Q_BLOCK)], ex_ref[:, pl.ds(d0, Q_BLOCK)]], axis=1)
    ncol = lax.broadcasted_iota(I32, (1, 2 * Q_BLOCK), 1)
    madd = (_dot(selm_near, ex_near) - 1.0) * (-NEG) + jnp.where((ncol < Q_BLOCK) & (bi == 0), NEG, 0.0)
    wcols = wb_ref.shape[2]
    ms, ls, accs = [], [], []
    for r in heads:
        z = _dot_nt(qz[r], kn) + wb_ref[0, rows(r), wcols - 2 * Q_BLOCK:wcols] + madd
        m = jnp.max(z, axis=-1, keepdims=True)
        e = jnp.exp2(z - m)
        ms.append(m)
        ls.append(jnp.sum(e, axis=-1, keepdims=True))
        accs.append(_dot(e.astype(BF16), vn))

    def far_body(t, carry):
        ms, ls, accs = carry
        k0 = pl.multiple_of(t * _FAR_TK, _FAR_TK)
        kt = sel_ref[0, pl.ds(k0, _FAR_TK), 0:LANES]
        vt = sel_ref[0, pl.ds(k0, _FAR_TK), LANES:2 * LANES]
        madd = (_dot(selm_far, ex_ref[:, pl.ds(k0, _FAR_TK)]) - 1.0) * (-NEG)
        out_m, out_l, out_a = [], [], []
        for r in heads:
            z = _dot_nt(qz[r], kt) + madd
            m_new = jnp.maximum(ms[r], jnp.max(z, axis=-1, keepdims=True))
            alpha = jnp.exp2(ms[r] - m_new)
            e = jnp.exp2(z - m_new)
            out_m.append(m_new)
            out_l.append(alpha * ls[r] + jnp.sum(e, axis=-1, keepdims=True))
            out_a.append(alpha * accs[r] + _dot(e.astype(BF16), vt))
        return tuple(out_m), tuple(out_l), tuple(out_a)

    _, ls, accs = lax.fori_loop(0, bi // 2, far_body, (tuple(ms), tuple(ls), tuple(accs)))
    osel = [accs[r] / ls[r] for r in heads]

    ks, vs = [], []
    for t in range(_WIN_TILES):
        st = pl.multiple_of(jnp.maximum(q0 - (_WIN_TILES - 1 - t) * Q_BLOCK, 0), Q_BLOCK)
        ks.append(win_ref[0, pl.ds(st, Q_BLOCK), 0:LANES])
        vs.append(win_ref[0, pl.ds(st, Q_BLOCK), LANES:2 * LANES])
    n_far = _WIN_TILES - 2
    kw_far, kw_near = jnp.concatenate(ks[:n_far], axis=0), jnp.concatenate(ks[n_far:], axis=0)
    vw = jnp.concatenate(vs, axis=0)
    wcol = lax.broadcasted_iota(I32, (1, wcols), 1)
    before_start = jnp.where(wcol < (_WIN_TILES - 1 - bi) * Q_BLOCK, NEG, 0.0)
    ow = []
    for r in heads:
        z = jnp.concatenate([_dot_nt(qz[r], kw_far), _dot_nt(qz[r], kw_near)], axis=1) + wb_ref[0, rows(r), :]
        z = z + before_start
        e = jnp.exp2(z - jnp.max(z, axis=-1, keepdims=True))
        ow.append(_dot(e.astype(BF16), vw) / jnp.sum(e, axis=-1, keepdims=True))

    gates = gate_ref[0]
    mixed = []
    for r in heads:
        c = r * N_BRANCH
        mixed.append(gates[:, c:c + 1] * oc[r] + gates[:, c + 1:c + 2] * osel[r] + gates[:, c + 2:c + 3] * ow[r])
    out = []
    for j in range(NSA_GRP // 2):
        a, b = mixed[2 * j], mixed[2 * j + 1]
        a = jnp.where(g == 0, a, pltpu.roll(a, NSA_HD, axis=1))
        b = jnp.where(g == 1, b, pltpu.roll(b, NSA_HD, axis=1))
        out.append(jnp.where(lane < NSA_HD, a, b))
    o_ref[0] = jnp.concatenate(out, axis=1).astype(BF16)


def _nsa_prompt(q, kc, vc, selb, winb, gates, wb, cb, ovt, ex):
    b, t, _ = q.shape
    ncp = kc.shape[1]
    gw = NSA_GRP * NSA_HD
    return pl.pallas_call(
        functools.partial(_nsa_prompt_kernel, n_sel=t // SEL_BLK),
        grid=(b, t // Q_BLOCK, NSA_KV),
        in_specs=[pl.BlockSpec((1, Q_BLOCK, gw), lambda i, j, g: (i, j, g)),
                  pl.BlockSpec((1, ncp, LANES), lambda i, j, g: (i, 0, 0)),
                  pl.BlockSpec((1, ncp, LANES), lambda i, j, g: (i, 0, 0)),
                  pl.BlockSpec((1, t, KV_W), lambda i, j, g: (i, 0, 0)),
                  pl.BlockSpec((1, t, KV_W), lambda i, j, g: (i, 0, 0)),
                  pl.BlockSpec((1, Q_BLOCK, LANES), lambda i, j, g: (i, j, g)),
                  pl.BlockSpec((1,) + wb.shape[1:], lambda i, j, g: (g, 0, 0)),
                  pl.BlockSpec((1,) + cb.shape[1:], lambda i, j, g: (g, 0, 0)),
                  pl.BlockSpec(ovt.shape, lambda i, j, g: (0, 0)),
                  pl.BlockSpec(ex.shape, lambda i, j, g: (0, 0))],
        out_specs=pl.BlockSpec((1, Q_BLOCK, gw), lambda i, j, g: (i, j, g)),
        out_shape=jax.ShapeDtypeStruct((b, t, NSA_KV * gw), BF16),
        compiler_params=_cparams(("parallel", "parallel", "arbitrary")),
    )(q, kc, vc, selb, winb, gates, wb, cb, ovt, ex)


def _mem_kv_kernel(m_ref, g_ref, w_ref, bd_ref, gk_ref, o_ref):
    x = m_ref[...]
    xn = (x * lax.rsqrt(jnp.mean(x * x, axis=-1, keepdims=True) + EPS) * g_ref[...]).astype(BF16)
    z = _dot(xn, w_ref[...])
    half = MEM_HEADS * MEM_HD
    kn = _group_rms(z[:, :half], bd_ref[...], gk_ref[...], MEM_HD)
    o_ref[...] = jnp.concatenate([kn, z[:, half:]], axis=1)


def _mem_kv(mem2d, prm):
    n = mem2d.shape[0]
    args = [mem2d, prm['norm_mem'], prm['w_mem_kv'], prm['bd_m'], prm['gmk']]
    w = 2 * MEM_HEADS * MEM_HD
    return pl.pallas_call(
        _mem_kv_kernel,
        grid=(1,),
        in_specs=[_full(a.shape) for a in args],
        out_specs=_full((n, w)),
        out_shape=jax.ShapeDtypeStruct((n, w), F32),
        compiler_params=_cparams(("arbitrary",)),
    )(*args)


def _mem_attn_kernel(q_ref, kv_ref, o_ref):
    half = MEM_HEADS * MEM_HD
    out = []
    for h in range(MEM_HEADS):
        sl = slice(h * MEM_HD, (h + 1) * MEM_HD)
        k = kv_ref[0, :, sl].astype(BF16)
        v = kv_ref[0, :, half + h * MEM_HD:half + (h + 1) * MEM_HD].astype(BF16)
        s = _dot_nt(q_ref[0, :, sl], k) * (MEM_HD ** -0.5)
        e = jnp.exp(s - jnp.max(s, axis=-1, keepdims=True))
        p = e / jnp.sum(e, axis=-1, keepdims=True)
        out.append(_dot(p.astype(BF16), v))
    o_ref[0] = jnp.concatenate(out, axis=1).astype(BF16)


def _mem_attn(mq, mkv, tq):
    b, t, w = mq.shape
    m = mkv.shape[1]
    return pl.pallas_call(
        _mem_attn_kernel,
        grid=(b, t // tq),
        in_specs=[pl.BlockSpec((1, tq, w), lambda i, j: (i, j, 0)),
                  pl.BlockSpec((1, m, 2 * w), lambda i, j: (i, 0, 0))],
        out_specs=pl.BlockSpec((1, tq, w), lambda i, j: (i, j, 0)),
        out_shape=jax.ShapeDtypeStruct((b, t, w), BF16),
        compiler_params=_cparams(("parallel", "parallel")),
    )(mq, mkv)


_SCORE_BATCH = 16


def _row_bias(dist, tab_ref):
    bucket = _t5_bucket(dist)
    acc = jnp.zeros((NSA_HEADS, dist.shape[1]), F32)
    for bk in range(REL_BUCKETS):
        acc = jnp.where(bucket == bk, tab_ref[:, bk:bk + 1], acc)
    return acc


def _sample_score_kernel(qz_ref, kc_ref, vc_ref, tab_ref, ov_ref, oc_ref, idx_ref, *, past, n_sel):
    sb = qz_ref.shape[0]
    ncp = kc_ref.shape[1]
    nsp = ov_ref.shape[1]
    cpos = lax.broadcasted_iota(I32, (1, ncp), 1) * CMP_STRIDE + (CMP_LEN - 1)
    bias = _row_bias(past - cpos, tab_ref)
    valid = jnp.broadcast_to(cpos <= past, (NSA_HEADS, ncp))
    hrow = lax.broadcasted_iota(I32, (NSA_HEADS, 1), 0)
    pcs = []
    for j in range(sb):
        pc = _masked_softmax(_dot_nt(qz_ref[j].astype(BF16), kc_ref[j]) + bias, valid)
        oc_ref[j] = _dot(pc.astype(BF16), vc_ref[j])
        grp = [jnp.sum(jnp.where(hrow // NSA_GRP == g, pc, 0.0), axis=0, keepdims=True) for g in range(NSA_KV)]
        pcs.append(jnp.concatenate(grp + [jnp.zeros((NSA_HEADS - NSA_KV, ncp), F32)], axis=0))
    hi, lo = _split_bf16(jnp.concatenate(pcs, axis=0))
    ssum = _dot(hi, ov_ref[...]) + _dot(lo, ov_ref[...])
    blk_id = lax.broadcasted_iota(I32, ssum.shape, 1)
    cur = past // SEL_BLK
    forced = (blk_id == 0) | (blk_id == cur) | (blk_id == cur - 1)
    score = jnp.where(forced, FORCE_SCORE, jnp.where(blk_id <= cur, ssum, -1.0))
    score = jnp.where(blk_id < n_sel, score, -jnp.inf)
    idx = _topk_select(score, min(TOP_N, n_sel))
    idx_ref[...] = idx.reshape(sb, NSA_HEADS, LANES)


def _sample_score(qz, kc, vc, tab8, ov, past, n_sel):
    nb = qz.shape[0]
    ncp = kc.shape[1]
    sb = math.gcd(nb, _SCORE_BATCH)
    blk8 = pl.BlockSpec((sb, NSA_HEADS, LANES), lambda i: (i, 0, 0))
    blkc = pl.BlockSpec((sb, ncp, LANES), lambda i: (i, 0, 0))
    return pl.pallas_call(
        functools.partial(_sample_score_kernel, past=past, n_sel=n_sel),
        grid=(nb // sb,),
        in_specs=[blk8, blkc, blkc, _full(tab8.shape), _full(ov.shape)],
        out_specs=[blk8, blk8],
        out_shape=[jax.ShapeDtypeStruct((nb, NSA_HEADS, LANES), F32)] * 2,
        compiler_params=_cparams(("parallel",)),
    )(qz, kc, vc, tab8, ov)


def _sample_attn_kernel(pt_ref, idx_ref, pool_ref, q8_ref, qz_ref, idxv_ref, ex_ref, oc_ref, gate_ref, selnew_ref,
                        winnew_ref, winnewt_ref, win_ref, mq_ref, mem_ref, tab_ref,
                        onsa_ref, omem_ref, wout_ref, buf, sem, *, past):
    b = pl.program_id(0)
    nb = pl.num_programs(0)
    page_rows = pool_ref.shape[4]
    n_top = buf.shape[4] // page_rows
    cur = past // SEL_BLK
    blocks_pp = page_rows // SEL_BLK

    def blk_copy(step, slot, j):
        g, n = j // n_top, j % n_top
        blk = jnp.minimum(idx_ref[step, j], cur - 1)
        page = pt_ref[step, blk // blocks_pp]
        c0 = pl.multiple_of(n * page_rows, page_rows)
        return pltpu.make_async_copy(pool_ref.at[page, :, g], buf.at[slot, g, :, :, pl.ds(c0, page_rows)],
                                     sem.at[slot])

    def fetch(step, slot):
        def body(j, c):
            blk_copy(step, slot, j).start()
            return c
        lax.fori_loop(0, NSA_KV * n_top, body, 0)

    slot = b % 2

    @pl.when(b == 0)
    def _():
        fetch(0, 0)

    @pl.when(b + 1 < nb)
    def _():
        fetch(b + 1, 1 - slot)

    hrow = lax.broadcasted_iota(I32, (NSA_HEADS, 1), 0)
    hgrp = hrow // NSA_GRP
    tab0 = tab_ref[:, 0:1]

    def half_of_group(x):
        return jnp.where(hgrp == 0, x[:, :NSA_HD], x[:, NSA_HD:])

    qz = qz_ref[0]
    xw = win_ref[0]
    wb_rows = xw.shape[1]
    wn = winnew_ref[0]
    j = lax.broadcasted_iota(I32, (1, wb_rows), 1)
    dw = wb_rows - j
    s = _dot(qz.astype(BF16), xw[:LANES].astype(BF16)) + _row_bias(dw, tab_ref)
    s_new = jnp.sum(qz * wn[:, :LANES], axis=-1, keepdims=True) + tab0
    mask = jnp.broadcast_to(dw < WINDOW, s.shape)
    zm = jnp.where(mask, s, NEG)
    m = jnp.maximum(jnp.max(zm, axis=-1, keepdims=True), s_new)
    e = jnp.exp(zm - m) * mask.astype(F32)
    e_new = jnp.exp(s_new - m)
    ow = (_dot_nt(e.astype(BF16), xw[LANES:].astype(BF16)) + e_new * wn[:, LANES:]) / jnp.maximum(
        jnp.sum(e, axis=-1, keepdims=True) + e_new, 1e-30)
    ow = half_of_group(ow)
    seq_lane = lax.broadcasted_iota(I32, winnewt_ref.shape, 1)
    new_col = jnp.sum(jnp.where(seq_lane == b, winnewt_ref[...], 0.0), axis=-1, keepdims=True)
    out_lane = lax.broadcasted_iota(I32, xw.shape, 1)
    wout_ref[0] = jnp.where(out_lane == wb_rows - 1, new_col, pltpu.roll(xw, wb_rows - 1, axis=1))

    mq = mq_ref[0].astype(BF16)
    n_mem = mem_ref.shape[1] // (2 * MEM_HEADS)
    omem = jnp.zeros((NSA_HEADS, MEM_HD), F32)
    for h in range(MEM_HEADS):
        kh = mem_ref[0, pl.ds(h, n_mem, stride=2 * MEM_HEADS), :].astype(BF16)
        vh = mem_ref[0, pl.ds(MEM_HEADS + h, n_mem, stride=2 * MEM_HEADS), :].astype(BF16)
        sm = _dot_nt(mq, kh) * (MEM_HD ** -0.5)
        em = jnp.exp(sm - jnp.max(sm, axis=-1, keepdims=True))
        pm = em / jnp.sum(em, axis=-1, keepdims=True)
        omem = jnp.where(hrow == h, _dot(pm.astype(BF16), vh), omem)
    omem_ref[0] = omem

    def wait_body(jj, c):
        blk_copy(b, slot, jj).wait()
        return c

    lax.fori_loop(0, NSA_KV * n_top, wait_body, 0)
    q8 = q8_ref[0]
    q8b = q8.astype(BF16)
    sn = selnew_ref[0]
    ncols = n_top * page_rows
    col = lax.broadcasted_iota(I32, (1, ncols), 1)
    idxe = _dot(idxv_ref[0].astype(BF16), ex_ref[...])
    osel = jnp.zeros((NSA_HEADS, NSA_HD), F32)
    for g in range(NSA_KV):
        kt = buf[slot, g, 0].astype(BF16)
        vt = buf[slot, g, 1].astype(BF16)
        blk = idxe[g:g + 1, :].astype(I32)
        in_page = col % page_rows
        spos = blk * SEL_BLK + in_page % SEL_BLK
        mask = (blk < cur) & (in_page // SEL_BLK == blk % blocks_pp)
        sg = _dot(q8b, kt) + _row_bias(past - spos, tab_ref)
        s_new = jnp.sum(q8 * sn[:, g * NSA_HD:(g + 1) * NSA_HD], axis=-1, keepdims=True) + tab0
        new_sel = jnp.max(jnp.where(blk == cur, 1.0, 0.0), axis=-1, keepdims=True) > 0.5
        maskb = jnp.broadcast_to(mask, sg.shape)
        zm = jnp.where(maskb, sg, NEG)
        z_new = jnp.where(new_sel, s_new, NEG)
        m = jnp.maximum(jnp.max(zm, axis=-1, keepdims=True), z_new)
        e = jnp.exp(zm - m) * maskb.astype(F32)
        e_new = jnp.exp(z_new - m) * new_sel.astype(F32)
        v_new = sn[:, (NSA_KV + g) * NSA_HD:(NSA_KV + g + 1) * NSA_HD]
        og = (_dot_nt(e.astype(BF16), vt) + e_new * v_new) / jnp.maximum(
            jnp.sum(e, axis=-1, keepdims=True) + e_new, 1e-30)
        osel = jnp.where(hgrp == g, og, osel)

    gates = gate_ref[0]
    onsa_ref[0] = gates[:, 0:1] * half_of_group(oc_ref[0]) + gates[:, 1:2] * osel + gates[:, 2:3] * ow


def _sample_attn(table, idx2d, pool, q8, qz, idxv, ex, oc, gate8, selnew, winnew, winnewt, win, mq8, mem, tab8, past,
                 n_top):
    nb = q8.shape[0]
    page_rows = pool.shape[4]
    blk3 = lambda a: pl.BlockSpec((1,) + a.shape[1:], lambda i, pt, ix: (i, 0, 0))
    whole = lambda a: pl.BlockSpec(a.shape, lambda i, pt, ix: (0, 0))
    gs = pltpu.PrefetchScalarGridSpec(
        num_scalar_prefetch=2,
        grid=(nb,),
        in_specs=[pl.BlockSpec(memory_space=pl.ANY), blk3(q8), blk3(qz), blk3(idxv), whole(ex), blk3(oc), blk3(gate8),
                  blk3(selnew), blk3(winnew), whole(winnewt), blk3(win), blk3(mq8), blk3(mem), whole(tab8)],
        out_specs=[pl.BlockSpec((1, NSA_HEADS, NSA_HD), lambda i, pt, ix: (i, 0, 0)),
                   pl.BlockSpec((1, NSA_HEADS, MEM_HD), lambda i, pt, ix: (i, 0, 0)),
                   pl.BlockSpec((1,) + win.shape[1:], lambda i, pt, ix: (i, 0, 0))],
        scratch_shapes=[pltpu.VMEM((2, NSA_KV, 2, NSA_HD, n_top * page_rows), F32), pltpu.SemaphoreType.DMA((2,))],
    )
    return pl.pallas_call(
        functools.partial(_sample_attn_kernel, past=past),
        grid_spec=gs,
        out_shape=[jax.ShapeDtypeStruct((nb, NSA_HEADS, NSA_HD), F32),
                   jax.ShapeDtypeStruct((nb, NSA_HEADS, MEM_HD), F32),
                   jax.ShapeDtypeStruct(win.shape, F32)],
        compiler_params=_cparams(("arbitrary",)),
    )(table, idx2d, pool, q8, qz, idxv, ex, oc, gate8, selnew, winnew, winnewt, win, mq8, mem, tab8)


def _merge_ffn_kernel(x_ref, ol_ref, on_ref, om_ref, mg_ref, wa_ref, wb_ref, wm_ref, wo_ref, gf_ref, wg_ref,
                      wu_ref, wd_ref, y_ref):
    d = x_ref.shape[1]
    mg = mg_ref[...]
    z = (mg[:, 0:d] * _dot(ol_ref[...], wa_ref[...]) + mg[:, d:2 * d] * _dot(on_ref[...], wb_ref[...])
         + mg[:, 2 * d:3 * d] * _dot(om_ref[...], wm_ref[...]))
    h = x_ref[...] + _dot(z.astype(BF16), wo_ref[...])
    f = (h * lax.rsqrt(jnp.mean(h * h, axis=-1, keepdims=True) + EPS) * gf_ref[...]).astype(BF16)
    a = jax.nn.silu(_dot(f, wg_ref[...])) * _dot(f, wu_ref[...])
    y_ref[...] = h + _dot(a.astype(BF16), wd_ref[...])


def _merge_ffn(x2d, ol, on, om, mg, prm, tm):
    n, d = x2d.shape
    row = lambda a: pl.BlockSpec((tm, a.shape[1]), lambda i: (i, 0))
    consts = [prm['w_up_a'], prm['w_up_b'], prm['w_up_m'], prm['w_o'], prm['norm_ffn'], prm['w_ffn_gate'],
              prm['w_ffn_up'], prm['w_ffn_down']]
    acts = [x2d, ol, on, om, mg]
    return pl.pallas_call(
        _merge_ffn_kernel,
        grid=(n // tm,),
        in_specs=[row(a) for a in acts] + [pl.BlockSpec(c.shape, lambda i: (0, 0), pipeline_mode=pl.Buffered(1))
                                           for c in consts],
        out_specs=pl.BlockSpec((tm, d), lambda i: (i, 0)),
        out_shape=jax.ShapeDtypeStruct((n, d), F32),
        compiler_params=_cparams(("parallel",)),
    )(*acts, *consts)


def _round_up(n, m):
    return -(-n // m) * m


def _overlap(ncp, n_cmp, nsp):
    cs = jnp.arange(ncp)[:, None] * CMP_STRIDE
    ss = jnp.arange(nsp)[None, :] * SEL_BLK
    hit = (cs < ss + SEL_BLK) & (cs + CMP_LEN > ss) & (jnp.arange(ncp)[:, None] < n_cmp)
    return hit.astype(BF16)


def _prep_params(norm_mix, w_in, conv_w, conv_b, w_lru_a, b_lru_a, w_lru_i, b_lru_i, lru_lambda, g_nsa_q, g_nsa_k,
                 pe_cmp_k, w_cmp_k1, w_cmp_k2, pe_cmp_v, w_cmp_v1, w_cmp_v2, norm_mem, w_mem_kv, g_mem_q, g_mem_k,
                 w_up_a, w_up_b, w_up_m, w_o, norm_ffn, w_ffn_gate, w_ffn_up, w_ffn_down):
    row = lambda v: v.reshape(1, -1).astype(F32)
    n_gg = N_BRANCH * NSA_GRP
    w = w_in
    zpad = jnp.zeros((w.shape[0], LANES - n_gg), w.dtype)
    ng0 = _OFF_NG
    w_packed = jnp.concatenate(
        [w[:, :ng0], w[:, ng0:ng0 + n_gg], zpad, w[:, ng0 + n_gg:ng0 + 2 * n_gg], zpad, w[:, ng0 + 2 * n_gg:]],
        axis=1).astype(BF16)
    eye = jnp.eye(NSA_KV, dtype=F32)

    def w1_big(w1):
        wr = w1.reshape(2, CMP_STRIDE, NSA_HD, CMP_HID)
        big = jnp.einsum('hrdj,gk->rgdhkj', wr, eye)
        return big.reshape(CMP_STRIDE * NSA_KV * NSA_HD, 2 * NSA_KV * CMP_HID).astype(BF16)

    def w2_bd(w2):
        return jnp.einsum('jd,gk->gjkd', w2, eye).reshape(NSA_KV * CMP_HID, NSA_KV * NSA_HD).astype(BF16)

    pe8 = lambda pe: jnp.broadcast_to(pe.reshape(1, -1), (8, pe.size)).astype(BF16)
    return dict(
        norm_mix=row(norm_mix), w_in=w_packed,
        bd_q=_block_diag_ones(NSA_HEADS * NSA_HD, NSA_HD), bd_k=_block_diag_ones(NSA_KV * NSA_HD, NSA_HD),
        bd_m=_block_diag_ones(MEM_HEADS * MEM_HD, MEM_HD),
        gq=row(jnp.tile(g_nsa_q, NSA_HEADS)), gk_cmp=row(jnp.tile(g_nsa_k[0], NSA_KV)),
        gk_sel=row(jnp.tile(g_nsa_k[1], NSA_KV)), gk_win=row(jnp.tile(g_nsa_k[2], NSA_KV)),
        gmq=row(jnp.tile(g_mem_q, MEM_HEADS)), gmk=row(jnp.tile(g_mem_k, MEM_HEADS)),
        conv_w=conv_w.astype(F32), conv_b=row(conv_b), w_lru_a=w_lru_a.astype(BF16), w_lru_i=w_lru_i.astype(BF16),
        b_lru_a=row(b_lru_a), b_lru_i=row(b_lru_i), lru_lambda=row(lru_lambda),
        w1k_big=w1_big(w_cmp_k1), w1v_big=w1_big(w_cmp_v1), w1k=w_cmp_k1.astype(BF16), w1v=w_cmp_v1.astype(BF16),
        pe_k=pe8(pe_cmp_k), pe_v=pe8(pe_cmp_v), w2k_bd=w2_bd(w_cmp_k2), w2v_bd=w2_bd(w_cmp_v2),
        norm_mem=row(norm_mem), w_mem_kv=w_mem_kv.astype(BF16),
        w_up_a=w_up_a.astype(BF16), w_up_b=w_up_b.astype(BF16), w_up_m=w_up_m.astype(BF16), w_o=w_o.astype(BF16),
        norm_ffn=row(norm_ffn), w_ffn_gate=w_ffn_gate.astype(BF16), w_ffn_up=w_ffn_up.astype(BF16),
        w_ffn_down=w_ffn_down.astype(BF16))


def _feature_major(cache):
    return jnp.transpose(cache, (0, 2, 3, 4, 1))


def kernel(x_prompt, x_sample, mem_prompt, cache_cmp_kv, cache_sel_kv, page_table, cache_win_kv, cache_mem_kv,
           state_lru_h, state_conv, rel_bias, norm_mix, w_in, conv_w, conv_b, w_lru_a, b_lru_a, w_lru_i, b_lru_i,
           lru_lambda, g_nsa_q, g_nsa_k, pe_cmp_k, w_cmp_k1, w_cmp_k2, pe_cmp_v, w_cmp_v1, w_cmp_v2, norm_mem,
           w_mem_kv, g_mem_q, g_mem_k, w_up_a, w_up_b, w_up_m, w_o, norm_ffn, w_ffn_gate, w_ffn_up, w_ffn_down):
    assert norm_mix.shape[0] == 1 and x_sample.shape[1] == 1
    weights = (norm_mix, w_in, conv_w, conv_b, w_lru_a, b_lru_a, w_lru_i, b_lru_i, lru_lambda, g_nsa_q, g_nsa_k,
               pe_cmp_k, w_cmp_k1, w_cmp_k2, pe_cmp_v, w_cmp_v1, w_cmp_v2, norm_mem, w_mem_kv, g_mem_q, g_mem_k,
               w_up_a, w_up_b, w_up_m, w_o, norm_ffn, w_ffn_gate, w_ffn_up, w_ffn_down)
    prm = _prep_params(*[w[0] for w in weights])
    bsz, t, d = x_prompt.shape
    db = x_sample.shape[0]
    n_pages = page_table.shape[1]
    page_rows = cache_cmp_kv.shape[2]
    past = n_pages * page_rows
    n_sel = -(-(past + 1) // SEL_BLK)
    assert t % (4 * Q_BLOCK) == 0 and page_rows % LANES == 0 and n_sel <= 256

    xp2 = x_prompt.reshape(bsz * t, d)
    (lx, gg, q, cmp_p, sel_p, win_p, selb, winb, ng, mq, mg) = _project(xp2, prm, 256)
    o_lru, h_p, cv_p = _lru_prompt(lx.reshape(bsz, t, d), gg.reshape(bsz, t, d), prm, 256)

    chunks_pp = page_rows // CMP_STRIDE
    pages_p = t // page_rows
    ncp_p = _round_up(t // CMP_STRIDE, LANES)
    pool_p = cmp_p.reshape(bsz * pages_p, chunks_pp, CHUNK_W)
    table_p = jnp.arange(bsz * pages_p, dtype=I32).reshape(bsz, pages_p)
    kc_p, vc_p = _compress(pool_p, table_p, prm, ncp_p, feature_major=False)

    nsp_p = _round_up(t // SEL_BLK, LANES)
    wb, cb = _bias_tiles(rel_bias.astype(F32), ncp_p)
    ovt_p = _overlap(ncp_p, t // CMP_STRIDE - 1, nsp_p).T
    ex_p = (jnp.arange(nsp_p)[:, None] == (jnp.arange(t) // SEL_BLK)[None, :]).astype(BF16)
    o_nsa = _nsa_prompt(q.reshape(bsz, t, -1), kc_p, vc_p, selb.reshape(bsz, t, KV_W), winb.reshape(bsz, t, KV_W),
                        ng.reshape(bsz, t, _NG_W), wb, cb, ovt_p, ex_p)

    m_rows = mem_prompt.shape[1]
    mkv = _mem_kv(mem_prompt.reshape(bsz * m_rows, d), prm)
    o_mem = _mem_attn(mq.reshape(bsz, t, -1), mkv.reshape(bsz, m_rows, -1), 256)

    y_p = _merge_ffn(xp2, o_lru.reshape(bsz * t, d), o_nsa.reshape(bsz * t, -1), o_mem.reshape(bsz * t, -1), mg,
                     prm, 256)

    xs2 = x_sample.reshape(db, d)
    (lx_s, gg_s, q_s, cmp_s, sel_s, win_s, _, _, ng_s, mq_s, mg_s) = _project(xs2, prm, db)
    cv0 = jnp.transpose(state_conv[0], (1, 0, 2))
    o_lru_s, h_s, cv_s = _lru_sample(lx_s, gg_s, cv0, state_lru_h[0], prm)

    ncp_s = _round_up(past // CMP_STRIDE, LANES)
    pool_c = _feature_major(cache_cmp_kv[0]).reshape(-1, KV_W, page_rows)
    kc_s, vc_s = _compress(pool_c, page_table, prm, ncp_s, feature_major=True)

    nsp_s = _round_up(n_sel, LANES)
    n_top = min(TOP_N, n_sel)
    ov_s = _overlap(ncp_s, (past + 1) // CMP_STRIDE - 1, nsp_s)
    tab8 = jnp.zeros((NSA_HEADS, LANES), F32).at[:, :REL_BUCKETS].set(rel_bias.astype(F32).T)
    q8 = q_s.reshape(db, NSA_HEADS, NSA_HD)
    qz = jnp.concatenate([jnp.where(jnp.arange(NSA_HEADS)[None, :, None] // NSA_GRP == gi, q8, 0.0)
                          for gi in range(NSA_KV)], axis=2)
    oc_s, idx = _sample_score(qz, kc_s, vc_s, tab8, ov_s, past, n_sel)
    idx2d = idx[:, :NSA_KV, :n_top].astype(I32).reshape(db, NSA_KV * n_top)

    n_gg = N_BRANCH * NSA_GRP
    gate8 = jnp.concatenate([ng_s[:, gi * LANES:gi * LANES + n_gg] for gi in range(NSA_KV)], axis=1)
    gate8 = jnp.pad(gate8.reshape(db, NSA_HEADS, N_BRANCH), ((0, 0), (0, 0), (0, LANES - N_BRANCH)))
    mq8 = jnp.pad(mq_s.astype(F32).reshape(db, MEM_HEADS, MEM_HD), ((0, 0), (0, NSA_HEADS - MEM_HEADS), (0, 0)))
    ex_s = (jnp.arange(LANES)[:, None] == (jnp.arange(n_top * page_rows) // page_rows)[None, :]).astype(BF16)
    pool_s = _feature_major(cache_sel_kv[0])
    win_t = _feature_major(cache_win_kv[0])
    wb_rows = win_t.shape[-1]
    mem_s = cache_mem_kv[0]
    n_mem = mem_s.shape[1]
    o_nsa8, o_mem8, win_new = _sample_attn(
        page_table, idx2d, pool_s, q8, qz, idx, ex_s, oc_s, gate8, sel_s.reshape(db, 1, KV_W),
        win_s.reshape(db, 1, KV_W), win_s.T, win_t.reshape(db, KV_W, wb_rows), mq8,
        mem_s.reshape(db, n_mem * 2 * MEM_HEADS, MEM_HD), tab8, past, n_top)
    o_nsa_s = o_nsa8.reshape(db, NSA_HEADS * NSA_HD)
    o_mem_s = o_mem8[:, :MEM_HEADS].reshape(db, MEM_HEADS * MEM_HD)
    y_s = _merge_ffn(xs2, o_lru_s, o_nsa_s.astype(BF16), o_mem_s.astype(BF16), mg_s, prm, db)

    kv6 = lambda a, lead: a.reshape((1,) + lead + (2, NSA_KV, NSA_HD))
    w_keep = min(WINDOW, t)
    win_out = jnp.transpose(win_new.reshape(db, 2, NSA_KV, NSA_HD, wb_rows), (0, 4, 1, 2, 3))
    return (y_p.reshape(bsz, t, d), y_s.reshape(db, 1, d),
            kv6(cmp_p, (bsz, t)), kv6(cmp_s, (db, 1)), kv6(sel_p, (bsz, t)), kv6(sel_s, (db, 1)),
            kv6(win_p.reshape(bsz, t, KV_W)[:, t - w_keep:], (bsz, w_keep)), win_out[None],
            mkv.reshape(1, bsz, m_rows, 2, MEM_HEADS, MEM_HD),
            h_p.reshape(1, bsz, d), h_s.reshape(1, db, d),
            cv_p.reshape(1, bsz, CONV_W - 1, d), jnp.transpose(cv_s, (1, 0, 2))[None])
```

```python
import functools
import math

import jax
import jax.numpy as jnp
from jax import lax
from jax.experimental import pallas as pl
from jax.experimental.pallas import tpu as pltpu

F32 = jnp.float32
BF16 = jnp.bfloat16
I32 = jnp.int32

EPS = 1e-6
NEG = -1e30
FORCE_SCORE = 1e6
LOG2E = math.log2(math.e)
LRU_C = 8.0
LRU_BLOCKS = 4
CONV_W = 4
NSA_HEADS = 8
NSA_KV = 2
NSA_GRP = NSA_HEADS // NSA_KV
NSA_HD = 64
CMP_STRIDE = 16
CMP_LEN = 2 * CMP_STRIDE
CMP_HID = 2 * NSA_HD
SEL_BLK = 64
TOP_N = 16
WINDOW = 512
Q_BLOCK = 128
MEM_HEADS = 4
MEM_HD = 128
REL_BUCKETS = 32
REL_MAX_DIST = 128
N_BRANCH = 3

LANES = 128
KV_W = 2 * NSA_KV * NSA_HD
CHUNK_W = CMP_STRIDE * KV_W
VMEM_LIMIT = 56 * 1024 * 1024


def _cparams(sem):
    return pltpu.CompilerParams(dimension_semantics=sem, vmem_limit_bytes=VMEM_LIMIT)


def _full(shape):
    n = len(shape)
    return pl.BlockSpec(shape, lambda *_: (0,) * n)


def _dot(a, b):
    return jnp.dot(a, b, preferred_element_type=F32)


def _dot_nt(a, b):
    return lax.dot_general(a, b, (((1,), (1,)), ((), ())), preferred_element_type=F32)


def _group_rms(z, ones_bd, gain, width):
    ss = _dot((z * z).astype(BF16), ones_bd)
    return z * lax.rsqrt(ss * (1.0 / width) + EPS) * gain


def _masked_softmax(z, mask):
    zm = jnp.where(mask, z, NEG)
    e = jnp.exp(zm - jnp.max(zm, axis=-1, keepdims=True)) * mask.astype(F32)
    return e / jnp.maximum(jnp.sum(e, axis=-1, keepdims=True), 1e-30)


def _t5_bucket(dist):
    n = jnp.maximum(dist, 0)
    exact = REL_BUCKETS // 2
    nf = jnp.maximum(n, 1).astype(F32)
    large = exact + (jnp.log(nf / exact) / math.log(REL_MAX_DIST / exact) * (REL_BUCKETS - exact)).astype(I32)
    return jnp.where(n < exact, n, jnp.minimum(large, REL_BUCKETS - 1))


def _block_diag_ones(n, width):
    i = jnp.arange(n) // width
    return (i[:, None] == i[None, :]).astype(BF16)


def _split_bf16(x):
    hi = x.astype(BF16)
    return hi, (x - hi.astype(F32)).astype(BF16)


_D = 1024
_NG_W = NSA_KV * LANES
_OFF_LX, _OFF_LG, _OFF_Q, _OFF_KV, _OFF_NG, _OFF_MQ, _OFF_MG, _OFF_END = (
    0, 1024, 2048, 2560, 3328, 3584, 4096, 7168)


def _proj_kernel(x_ref, g_ref, w_ref, bdq_ref, bdk_ref, bdm_ref, gq_ref, gks_ref, gkw_ref, gmq_ref,
                 lx_ref, lg_ref, q_ref, cmp_ref, sel_ref, win_ref, selk_ref, wink_ref, mq_ref, mg_ref,
                 selvt_ref, winvt_ref, ngt_ref):
    x = x_ref[...]
    xn = (x * lax.rsqrt(jnp.mean(x * x, axis=-1, keepdims=True) + EPS) * g_ref[...]).astype(BF16)

    def seg(a, b):
        return _dot(xn, w_ref[:, a:b])

    lx_ref[...] = seg(_OFF_LX, _OFF_LG)
    lg_ref[...] = jax.nn.gelu(seg(_OFF_LG, _OFF_Q))
    zq = seg(_OFF_Q, _OFF_KV)
    q_ref[...] = _group_rms(zq, bdq_ref[...], gq_ref[...], NSA_HD) * (NSA_HD ** -0.5)
    cmp_ref[...] = seg(_OFF_KV, _OFF_KV + KV_W)
    for off, gain_ref, o_ref, k_ref, vt_ref in ((_OFF_KV + KV_W, gks_ref, sel_ref, selk_ref, selvt_ref),
                                                (_OFF_KV + 2 * KV_W, gkw_ref, win_ref, wink_ref, winvt_ref)):
        z = seg(off, off + KV_W)
        kn = _group_rms(z[:, :LANES], bdk_ref[...], gain_ref[...], NSA_HD)
        o_ref[...] = jnp.concatenate([kn, z[:, LANES:]], axis=1)
        k_ref[...] = kn.astype(BF16)
        vt_ref[0] = z[:, LANES:].T.astype(BF16)
    ngt_ref[0] = jax.nn.sigmoid(seg(_OFF_NG, _OFF_MQ)).T
    zm = seg(_OFF_MQ, _OFF_MG)
    mq_ref[...] = _group_rms(zm, bdm_ref[...], gmq_ref[...], MEM_HD).astype(BF16)
    mg_ref[...] = jax.nn.sigmoid(seg(_OFF_MG, _OFF_END))


def _project(x2d, prm, tm, bsz):
    n = x2d.shape[0]
    t = n // bsz
    per_b = t // tm
    row = lambda w: pl.BlockSpec((tm, w), lambda i: (i, 0))
    fmaj = lambda w: pl.BlockSpec((1, w, tm), lambda i: (i // per_b, 0, i % per_b))
    consts = [prm['norm_mix'], prm['w_in'], prm['bd_q'], prm['bd_k'], prm['bd_m'],
              prm['gq'], prm['gk_sel'], prm['gk_win'], prm['gmq']]
    widths = [(_D, F32), (_D, F32), (512, F32), (KV_W, F32), (KV_W, F32), (KV_W, F32), (LANES, BF16), (LANES, BF16),
              (512, BF16), (3 * _D, F32)]
    fwidths = [(LANES, BF16), (LANES, BF16), (_NG_W, F32)]
    return pl.pallas_call(
        _proj_kernel,
        grid=(n // tm,),
        in_specs=[row(_D)] + [_full(c.shape) for c in consts],
        out_specs=[row(w) for w, _ in widths] + [fmaj(w) for w, _ in fwidths],
        out_shape=[jax.ShapeDtypeStruct((n, w), dt) for w, dt in widths]
                  + [jax.ShapeDtypeStruct((bsz, w, t), dt) for w, dt in fwidths],
        compiler_params=_cparams(("parallel",)),
    )(x2d, *consts)


def _lru_gates(xc, wa_ref, wi_ref, ba_ref, bi_ref, lam_ref):
    bw = xc.shape[1] // LRU_BLOCKS
    ra, ri = [], []
    for n in range(LRU_BLOCKS):
        xr = xc[:, n * bw:(n + 1) * bw].astype(BF16)
        ra.append(_dot(xr, wa_ref[n]))
        ri.append(_dot(xr, wi_ref[n]))
    r = jax.nn.sigmoid(jnp.concatenate(ra, axis=1) + ba_ref[...])
    i = jax.nn.sigmoid(jnp.concatenate(ri, axis=1) + bi_ref[...])
    lam = -lam_ref[...]
    softplus = jnp.maximum(lam, 0.0) + jnp.log1p(jnp.exp(-jnp.abs(lam)))
    log_a = -LRU_C * r * softplus
    a = jnp.exp(log_a)
    b = jnp.sqrt(jnp.tanh(-log_a) * (a * a + 1.0)) * (i * xc)
    return a, b


def _lru_scan(a, b):
    tt = a.shape[0]
    row = lax.broadcasted_iota(I32, a.shape, 0)
    k = 1
    while k < tt:
        keep = row >= k
        b = b + a * jnp.where(keep, pltpu.roll(b, k, axis=0), 0.0)
        a = a * jnp.where(keep, pltpu.roll(a, k, axis=0), 1.0)
        k *= 2
    return a, b


def _lru_prompt_kernel(x_ref, gg_ref, cw_ref, cb_ref, wa_ref, wi_ref, ba_ref, bi_ref, lam_ref,
                       o_ref, h_ref, cv_ref, xbuf, hcar):
    t = pl.program_id(1)
    tt = x_ref.shape[1]

    @pl.when(t == 0)
    def _():
        xbuf[0:8, :] = jnp.zeros((8, xbuf.shape[1]), F32)
        hcar[...] = jnp.zeros(hcar.shape, F32)

    x = x_ref[0]
    xbuf[8:8 + tt, :] = x
    xc = cb_ref[...] + xbuf[5:5 + tt, :] * cw_ref[0:1, :]
    for k in range(1, CONV_W):
        xc = xc + xbuf[5 + k:5 + k + tt, :] * cw_ref[k:k + 1, :]
    a, b = _lru_gates(xc, wa_ref, wi_ref, ba_ref, bi_ref, lam_ref)
    ap, hs = _lru_scan(a, b)
    h = hs + ap * hcar[...]
    o_ref[0] = (h * gg_ref[0]).astype(BF16)
    hcar[...] = h[tt - 1:tt, :]
    xbuf[0:8, :] = x[tt - 8:tt, :]

    @pl.when(t == pl.num_programs(1) - 1)
    def _():
        h_ref[0] = h[tt - 1:tt, :]
        cv_ref[0] = x[tt - (CONV_W - 1):tt, :]


def _lru_prompt(lx, gg, prm, tt):
    b, t, w = lx.shape
    blk = pl.BlockSpec((1, tt, w), lambda i, j: (i, j, 0))
    consts = [prm['conv_w'], prm['conv_b'], prm['w_lru_a'], prm['w_lru_i'], prm['b_lru_a'], prm['b_lru_i'],
              prm['lru_lambda']]
    return pl.pallas_call(
        _lru_prompt_kernel,
        grid=(b, t // tt),
        in_specs=[blk, blk] + [_full(c.shape) for c in consts],
        out_specs=[blk, pl.BlockSpec((1, 1, w), lambda i, j: (i, 0, 0)),
                   pl.BlockSpec((1, CONV_W - 1, w), lambda i, j: (i, 0, 0))],
        out_shape=[jax.ShapeDtypeStruct((b, t, w), BF16), jax.ShapeDtypeStruct((b, 1, w), F32),
                   jax.ShapeDtypeStruct((b, CONV_W - 1, w), F32)],
        scratch_shapes=[pltpu.VMEM((tt + 8, w), F32), pltpu.VMEM((1, w), F32)],
        compiler_params=_cparams(("parallel", "arbitrary")),
    )(lx, gg, *consts)


def _lru_sample_kernel(x_ref, gg_ref, cv0_ref, h0_ref, cw_ref, cb_ref, wa_ref, wi_ref, ba_ref, bi_ref, lam_ref,
                       o_ref, h_ref, cv_ref):
    x = x_ref[...]
    xc = cb_ref[...] + x * cw_ref[CONV_W - 1:CONV_W, :]
    for k in range(CONV_W - 1):
        xc = xc + cv0_ref[k] * cw_ref[k:k + 1, :]
    a, b = _lru_gates(xc, wa_ref, wi_ref, ba_ref, bi_ref, lam_ref)
    h = a * h0_ref[...] + b
    o_ref[...] = (h * gg_ref[...]).astype(BF16)
    h_ref[...] = h
    for k in range(CONV_W - 2):
        cv_ref[k] = cv0_ref[k + 1]
    cv_ref[CONV_W - 2] = x


def _lru_sample(lx, gg, cv0, h0, prm):
    n, w = lx.shape
    consts = [prm['conv_w'], prm['conv_b'], prm['w_lru_a'], prm['w_lru_i'], prm['b_lru_a'], prm['b_lru_i'],
              prm['lru_lambda']]
    args = [lx, gg, cv0, h0] + consts
    return pl.pallas_call(
        _lru_sample_kernel,
        grid=(1,),
        in_specs=[_full(a.shape) for a in args],
        out_specs=[_full((n, w)), _full((n, w)), _full((CONV_W - 1, n, w))],
        out_shape=[jax.ShapeDtypeStruct((n, w), BF16), jax.ShapeDtypeStruct((n, w), F32),
                   jax.ShapeDtypeStruct((CONV_W - 1, n, w), F32)],
        compiler_params=_cparams(("arbitrary",)),
    )(*args)


_PAGE_UNROLL = 8


def _compress_kernel(pt_ref, pool_ref, w1k_ref, w1v_ref, w1kp_ref, w1vp_ref, pek_ref, pev_ref, w2k_ref, w2v_ref,
                     bdk_ref, gk_ref, kc_ref, vct_ref, buf, sem, *scratch, n_valid, feature_major):
    b = pl.program_id(0)
    nb = pl.num_programs(0)
    n_pages = pt_ref.shape[1]
    ncp = kc_ref.shape[1]

    def page_copy(step, slot, p):
        return pltpu.make_async_copy(pool_ref.at[pt_ref[step, p]], buf.at[slot, p], sem.at[slot])

    def fetch(step, slot):
        def body(p, c):
            page_copy(step, slot, p).start()
            return c
        lax.fori_loop(0, n_pages, body, 0)

    slot = b % 2

    @pl.when(b == 0)
    def _():
        fetch(0, 0)

    @pl.when(b + 1 < nb)
    def _():
        fetch(b + 1, 1 - slot)

    def wait_body(p, c):
        page_copy(b, slot, p).wait()
        return c

    lax.fori_loop(0, n_pages, wait_body, 0)

    if feature_major:
        zbuf, = scratch
        page_rows = buf.shape[3]
        n_ch = n_pages * page_rows // CMP_STRIDE

        def transpose_page(p, c):
            r0 = pl.multiple_of(p * page_rows, page_rows)
            for kv in range(2):
                zbuf[kv, pl.ds(r0, page_rows), :] = buf[slot, p, kv * LANES:(kv + 1) * LANES, :].T
            return c

        lax.fori_loop(0, n_pages, transpose_page, 0, unroll=math.gcd(n_pages, _PAGE_UNROLL))

        def chunk_rows(kv):
            return jnp.concatenate(
                [zbuf[kv, pl.ds(r, n_ch, stride=CMP_STRIDE), :].astype(BF16) for r in range(CMP_STRIDE)], axis=1)
    else:
        rows_pp = buf.shape[2]
        n_ch = n_pages * rows_pp

        def chunk_rows(kv):
            return jnp.concatenate(
                [buf[slot, :, :, r * KV_W + kv * LANES:r * KV_W + (kv + 1) * LANES].reshape(n_ch, LANES).astype(BF16)
                 for r in range(CMP_STRIDE)], axis=1)

    row = lax.broadcasted_iota(I32, (n_ch, 1), 0)
    keep = row < n_valid
    outs = []
    for kv, w1_ref, w1p_ref, pe_ref, w2_ref in ((0, w1k_ref, w1kp_ref, pek_ref, w2k_ref),
                                                 (1, w1v_ref, w1vp_ref, pev_ref, w2v_ref)):
        hh = _dot(chunk_rows(kv), w1_ref[...])
        pos = _dot(pe_ref[...], w1p_ref[...])[0:1, :]
        pos = jnp.concatenate([pos, pos], axis=1)
        nh = NSA_KV * CMP_HID
        h = hh[:, :nh] + pltpu.roll(hh[:, nh:], n_ch - 1, axis=0) + pos
        outs.append(_dot(jax.nn.gelu(h).astype(BF16), w2_ref[...]))
    kc = _group_rms(outs[0], bdk_ref[...], gk_ref[...], NSA_HD)
    kc = jnp.where(keep, kc, 0.0)
    vc = jnp.where(keep, outs[1], 0.0)
    if ncp > n_ch:
        pad = jnp.zeros((ncp - n_ch, kc.shape[1]), F32)
        kc = jnp.concatenate([kc, pad], axis=0)
        vc = jnp.concatenate([vc, pad], axis=0)
    kc_ref[0] = kc.astype(BF16)
    vct_ref[0] = vc.T.astype(BF16)


def _compress(pool, table, prm, ncp, feature_major):
    nb, n_pages = table.shape
    if feature_major:
        page_rows = pool.shape[2]
        n_ch = n_pages * page_rows // CMP_STRIDE
        scratch = [pltpu.VMEM((2, n_pages * page_rows, LANES), F32)]
    else:
        n_ch = n_pages * pool.shape[1]
        scratch = []
    consts = [prm['w1k_big'], prm['w1v_big'], prm['w1k'], prm['w1v'], prm['pe_k'], prm['pe_v'],
              prm['w2k_bd'], prm['w2v_bd'], prm['bd_k'], prm['gk_cmp']]
    out_blk = pl.BlockSpec((1, ncp, LANES), lambda i, pt: (i, 0, 0))
    gs = pltpu.PrefetchScalarGridSpec(
        num_scalar_prefetch=1,
        grid=(nb,),
        in_specs=[pl.BlockSpec(memory_space=pl.ANY)] + [pl.BlockSpec(c.shape, lambda i, pt, _n=len(c.shape): (0,) * _n)
                                                         for c in consts],
        out_specs=[out_blk, pl.BlockSpec((1, LANES, ncp), lambda i, pt: (i, 0, 0))],
        scratch_shapes=[pltpu.VMEM((2, n_pages) + pool.shape[1:], F32), pltpu.SemaphoreType.DMA((2,))] + scratch,
    )
    return pl.pallas_call(
        functools.partial(_compress_kernel, n_valid=n_ch - 1, feature_major=feature_major),
        grid_spec=gs,
        out_shape=[jax.ShapeDtypeStruct((nb, ncp, LANES), BF16), jax.ShapeDtypeStruct((nb, LANES, ncp), BF16)],
        compiler_params=_cparams(("arbitrary",)),
    )(table, pool, *consts)


_WIN_TILES = WINDOW // Q_BLOCK + 1
_QCOLS = NSA_GRP * Q_BLOCK


def _bias_tiles_kernel(tab_ref, wb_ref, cb_ref):
    g = pl.program_id(0)
    wrows = wb_ref.shape[1]
    ncp = cb_ref.shape[1] // 2
    c_w = lax.broadcasted_iota(I32, (wrows, Q_BLOCK), 0)
    i_w = lax.broadcasted_iota(I32, (wrows, Q_BLOCK), 1)
    d_w = (_WIN_TILES - 1) * Q_BLOCK + i_w - c_w
    k_c = lax.broadcasted_iota(I32, (2 * ncp, Q_BLOCK), 0)
    i_c = lax.broadcasted_iota(I32, (2 * ncp, Q_BLOCK), 1)
    d_c = i_c - (CMP_LEN - 1) - CMP_STRIDE * (k_c - ncp)
    for d, ok, ref in ((d_w, (d_w >= 0) & (d_w < WINDOW), wb_ref), (d_c, d_c >= 0, cb_ref)):
        bucket = _t5_bucket(d)
        for r in range(NSA_GRP):
            h = g * NSA_GRP + r
            acc = jnp.zeros(d.shape, F32)
            for bk in range(REL_BUCKETS):
                acc = jnp.where(bucket == bk, tab_ref[bk, h], acc)
            far = tab_ref[REL_BUCKETS - 1, h]
            ref[0, :, r * Q_BLOCK:(r + 1) * Q_BLOCK] = jnp.where(ok, (acc - far) * LOG2E, NEG)


def _bias_tiles(rel_bias, ncp):
    wrows = _WIN_TILES * Q_BLOCK
    return pl.pallas_call(
        _bias_tiles_kernel,
        grid=(NSA_KV,),
        in_specs=[pl.BlockSpec(memory_space=pltpu.SMEM)],
        out_specs=[pl.BlockSpec((1, wrows, _QCOLS), lambda g: (g, 0, 0)),
                   pl.BlockSpec((1, 2 * ncp, _QCOLS), lambda g: (g, 0, 0))],
        out_shape=[jax.ShapeDtypeStruct((NSA_KV, wrows, _QCOLS), F32),
                   jax.ShapeDtypeStruct((NSA_KV, 2 * ncp, _QCOLS), F32)],
        compiler_params=_cparams(("arbitrary",)),
    )(rel_bias)


def _topk_select(score, n_top):
    lane = lax.broadcasted_iota(I32, score.shape, 1).astype(F32)
    big = float(score.shape[1])
    slot = lax.broadcasted_iota(I32, (score.shape[0], LANES), 1)
    idx = jnp.zeros((score.shape[0], LANES), F32)
    for it in range(n_top):
        m = jnp.max(score, axis=-1, keepdims=True)
        first = jnp.min(jnp.where(score == m, lane, big), axis=-1, keepdims=True)
        score = jnp.where(lane == first, -jnp.inf, score)
        idx = jnp.where(slot == it, first, idx)
    return idx


def _topk_mask_cols(score, n_top):
    sid = lax.broadcasted_iota(I32, score.shape, 0).astype(F32)
    big = float(score.shape[0])
    sel = jnp.zeros(score.shape, F32)
    for _ in range(n_top):
        m = jnp.max(score, axis=0, keepdims=True)
        first = jnp.min(jnp.where(score == m, sid, big), axis=0, keepdims=True)
        hit = sid == first
        sel = jnp.where(hit, 1.0, sel)
        score = jnp.where(hit, -jnp.inf, score)
    return sel


_FAR_TK = 1024


def _nsa_prompt_kernel(q_ref, kc_ref, vct_ref, selk_ref, selvt_ref, wink_ref, winvt_ref, gate_ref, wb_ref, cb_ref,
                       ovt_ref, ext_ref, o_ref, *, n_sel):
    bi = pl.program_id(1)
    g = pl.program_id(2)
    q0 = bi * Q_BLOCK
    ncp = kc_ref.shape[1]
    nsp = ovt_ref.shape[0]
    heads = range(NSA_GRP)

    lane = lax.broadcasted_iota(I32, (Q_BLOCK, LANES), 1)
    in_g = (lane // NSA_HD) == g
    qf = q_ref[0] * LOG2E
    qz = []
    for r in heads:
        blk = qf[:, (r // 2) * LANES:(r // 2 + 1) * LANES]
        blk = jnp.where((r % 2) == g, blk, pltpu.roll(blk, NSA_HD, axis=1))
        qz.append(jnp.where(in_g, blk, 0.0))
    qz = jnp.concatenate(qz, axis=0).astype(BF16)

    def tile4(x):
        return jnp.concatenate([x] * NSA_GRP, axis=1)

    def col_max(z):
        return jnp.max(z, axis=0, keepdims=True)

    def col_sum(z):
        return jnp.sum(z, axis=0, keepdims=True)

    c0 = pl.multiple_of(ncp - bi * (Q_BLOCK // CMP_STRIDE), Q_BLOCK // CMP_STRIDE)
    z = _dot_nt(kc_ref[0], qz) + cb_ref[0, pl.ds(c0, ncp), :]
    m = col_max(z)
    e = jnp.exp2(z - m)
    pc = e * jnp.where(m > 0.5 * NEG, 1.0 / col_sum(e), 0.0)
    oc = _dot(vct_ref[0], pc.astype(BF16))

    pcs = pc[:, 0:Q_BLOCK]
    for r in range(1, NSA_GRP):
        pcs = pcs + pc[:, r * Q_BLOCK:(r + 1) * Q_BLOCK]
    hi, lo = _split_bf16(pcs)
    ssum = _dot(ovt_ref[...], hi) + _dot(ovt_ref[...], lo)
    s_id = lax.broadcasted_iota(I32, (nsp, Q_BLOCK), 0)
    cur = (q0 + lax.broadcasted_iota(I32, (1, Q_BLOCK), 1)) // SEL_BLK
    forced = (s_id == 0) | (s_id == cur) | (s_id == cur - 1)
    score = jnp.where(forced, FORCE_SCORE, jnp.where(s_id <= cur, ssum, -1.0))
    selm = _topk_mask_cols(score, min(TOP_N, n_sel))
    selm_near = selm.astype(BF16)
    selm_far = jnp.where(s_id < 2 * bi - 2, selm, 0.0).astype(BF16)

    p0 = pl.multiple_of(jnp.maximum(q0 - Q_BLOCK, 0), Q_BLOCK)
    d0 = pl.multiple_of(q0, Q_BLOCK)
    kn = jnp.concatenate([selk_ref[0, pl.ds(p0, Q_BLOCK), :], selk_ref[0, pl.ds(d0, Q_BLOCK), :]], axis=0)
    vn = jnp.concatenate([selvt_ref[0, :, pl.ds(p0, Q_BLOCK)], selvt_ref[0, :, pl.ds(d0, Q_BLOCK)]], axis=1)
    ex_near = jnp.concatenate([ext_ref[pl.ds(p0, Q_BLOCK), :], ext_ref[pl.ds(d0, Q_BLOCK), :]], axis=0)
    krow = lax.broadcasted_iota(I32, (2 * Q_BLOCK, 1), 0)
    madd = (_dot(ex_near, selm_near) - 1.0) * (-NEG) + jnp.where((krow < Q_BLOCK) & (bi == 0), NEG, 0.0)
    wrows = wb_ref.shape[1]
    z = _dot_nt(kn, qz) + wb_ref[0, wrows - 2 * Q_BLOCK:wrows, :] + tile4(madd)
    m = col_max(z)
    e = jnp.exp2(z - m)
    carry = (m, col_sum(e), _dot(vn, e.astype(BF16)))

    def far_body(t, carry):
        m, l, acc = carry
        k0 = pl.multiple_of(t * _FAR_TK, _FAR_TK)
        madd = (_dot(ext_ref[pl.ds(k0, _FAR_TK), :], selm_far) - 1.0) * (-NEG)
        z = _dot_nt(selk_ref[0, pl.ds(k0, _FAR_TK), :], qz) + tile4(madd)
        m_new = jnp.maximum(m, col_max(z))
        alpha = jnp.exp2(m - m_new)
        e = jnp.exp2(z - m_new)
        return (m_new, alpha * l + col_sum(e),
                alpha * acc + _dot(selvt_ref[0, :, pl.ds(k0, _FAR_TK)], e.astype(BF16)))

    far_blocks = _FAR_TK // Q_BLOCK
    _, l, acc = lax.fori_loop(0, (bi + far_blocks - 2) // far_blocks, far_body, carry)
    osel = acc / l

    ks, vs = [], []
    for t in range(_WIN_TILES):
        st = pl.multiple_of(jnp.maximum(q0 - (_WIN_TILES - 1 - t) * Q_BLOCK, 0), Q_BLOCK)
        ks.append(wink_ref[0, pl.ds(st, Q_BLOCK), :])
        vs.append(winvt_ref[0, :, pl.ds(st, Q_BLOCK)])
    wrow = lax.broadcasted_iota(I32, (wrows, 1), 0)
    before_start = jnp.where(wrow < (_WIN_TILES - 1 - bi) * Q_BLOCK, NEG, 0.0)
    z = _dot_nt(jnp.concatenate(ks, axis=0), qz) + wb_ref[0] + before_start
    e = jnp.exp2(z - col_max(z))
    ow = _dot(jnp.concatenate(vs, axis=1), e.astype(BF16)) / col_sum(e)

    gates = gate_ref[0]

    def gate_row(br):
        return jnp.concatenate([gates[r * N_BRANCH + br:r * N_BRANCH + br + 1, :] for r in heads], axis=1)

    mixed = gate_row(0) * oc + gate_row(1) * osel + gate_row(2) * ow
    mixed = jnp.where(g == 0, mixed[:NSA_HD], mixed[NSA_HD:])
    out = []
    for j in range(NSA_GRP // 2):
        pair = jnp.concatenate([mixed[:, (2 * j) * Q_BLOCK:(2 * j + 1) * Q_BLOCK],
                                mixed[:, (2 * j + 1) * Q_BLOCK:(2 * j + 2) * Q_BLOCK]], axis=0)
        out.append(pair.T)
    o_ref[0] = jnp.concatenate(out, axis=1).astype(BF16)


def _nsa_prompt(q, kc, vct, selk, selvt, wink, winvt, gates_t, wb, cb, ovt, ext):
    b, t, _ = q.shape
    ncp = kc.shape[1]
    gw = NSA_GRP * NSA_HD
    per_b = lambda shape: pl.BlockSpec((1,) + shape, lambda i, j, g: (i, 0, 0))
    per_g = lambda a: pl.BlockSpec((1,) + a.shape[1:], lambda i, j, g: (g, 0, 0))
    return pl.pallas_call(
        functools.partial(_nsa_prompt_kernel, n_sel=t // SEL_BLK),
        grid=(b, t // Q_BLOCK, NSA_KV),
        in_specs=[pl.BlockSpec((1, Q_BLOCK, gw), lambda i, j, g: (i, j, g)),
                  per_b((ncp, LANES)), per_b((LANES, ncp)),
                  per_b((t, LANES)), per_b((LANES, t)), per_b((t, LANES)), per_b((LANES, t)),
                  pl.BlockSpec((1, LANES, Q_BLOCK), lambda i, j, g: (i, g, j)),
                  per_g(wb), per_g(cb),
                  pl.BlockSpec(ovt.shape, lambda i, j, g: (0, 0)),
                  pl.BlockSpec(ext.shape, lambda i, j, g: (0, 0))],
        out_specs=pl.BlockSpec((1, Q_BLOCK, gw), lambda i, j, g: (i, j, g)),
        out_shape=jax.ShapeDtypeStruct((b, t, NSA_KV * gw), BF16),
        compiler_params=_cparams(("parallel", "parallel", "arbitrary")),
    )(q, kc, vct, selk, selvt, wink, winvt, gates_t, wb, cb, ovt, ext)


def _mem_kv_kernel(m_ref, g_ref, w_ref, bd_ref, gk_ref, o_ref):
    x = m_ref[...]
    xn = (x * lax.rsqrt(jnp.mean(x * x, axis=-1, keepdims=True) + EPS) * g_ref[...]).astype(BF16)
    z = _dot(xn, w_ref[...])
    half = MEM_HEADS * MEM_HD
    kn = _group_rms(z[:, :half], bd_ref[...], gk_ref[...], MEM_HD)
    o_ref[...] = jnp.concatenate([kn, z[:, half:]], axis=1)


def _mem_kv(mem2d, prm):
    n = mem2d.shape[0]
    args = [mem2d, prm['norm_mem'], prm['w_mem_kv'], prm['bd_m'], prm['gmk']]
    w = 2 * MEM_HEADS * MEM_HD
    return pl.pallas_call(
        _mem_kv_kernel,
        grid=(1,),
        in_specs=[_full(a.shape) for a in args],
        out_specs=_full((n, w)),
        out_shape=jax.ShapeDtypeStruct((n, w), F32),
        compiler_params=_cparams(("arbitrary",)),
    )(*args)


def _mem_attn_kernel(q_ref, kv_ref, o_ref):
    half = MEM_HEADS * MEM_HD
    out = []
    for h in range(MEM_HEADS):
        sl = slice(h * MEM_HD, (h + 1) * MEM_HD)
        k = kv_ref[0, :, sl].astype(BF16)
        v = kv_ref[0, :, half + h * MEM_HD:half + (h + 1) * MEM_HD].astype(BF16)
        s = _dot_nt(q_ref[0, :, sl], k) * (MEM_HD ** -0.5)
        e = jnp.exp(s - jnp.max(s, axis=-1, keepdims=True))
        p = e / jnp.sum(e, axis=-1, keepdims=True)
        out.append(_dot(p.astype(BF16), v))
    o_ref[0] = jnp.concatenate(out, axis=1).astype(BF16)


def _mem_attn(mq, mkv, tq):
    b, t, w = mq.shape
    m = mkv.shape[1]
    return pl.pallas_call(
        _mem_attn_kernel,
        grid=(b, t // tq),
        in_specs=[pl.BlockSpec((1, tq, w), lambda i, j: (i, j, 0)),
                  pl.BlockSpec((1, m, 2 * w), lambda i, j: (i, 0, 0))],
        out_specs=pl.BlockSpec((1, tq, w), lambda i, j: (i, j, 0)),
        out_shape=jax.ShapeDtypeStruct((b, t, w), BF16),
        compiler_params=_cparams(("parallel", "parallel")),
    )(mq, mkv)


_SCORE_BATCH = 16


def _row_bias(dist, tab_ref):
    bucket = _t5_bucket(dist)
    acc = jnp.zeros((NSA_HEADS, dist.shape[1]), F32)
    for bk in range(REL_BUCKETS):
        acc = jnp.where(bucket == bk, tab_ref[:, bk:bk + 1], acc)
    return acc


def _sample_score_kernel(qz_ref, kc_ref, vct_ref, tab_ref, ov_ref, oc_ref, idx_ref, *, past, n_sel):
    sb = qz_ref.shape[0]
    ncp = kc_ref.shape[1]
    nsp = ov_ref.shape[1]
    cpos = lax.broadcasted_iota(I32, (1, ncp), 1) * CMP_STRIDE + (CMP_LEN - 1)
    bias = _row_bias(past - cpos, tab_ref)
    valid = jnp.broadcast_to(cpos <= past, (NSA_HEADS, ncp))
    hrow = lax.broadcasted_iota(I32, (NSA_HEADS, 1), 0)
    pcs = []
    for j in range(sb):
        pc = _masked_softmax(_dot_nt(qz_ref[j].astype(BF16), kc_ref[j]) + bias, valid)
        oc_ref[j] = _dot_nt(pc.astype(BF16), vct_ref[j])
        grp = [jnp.sum(jnp.where(hrow // NSA_GRP == g, pc, 0.0), axis=0, keepdims=True) for g in range(NSA_KV)]
        pcs.append(jnp.concatenate(grp + [jnp.zeros((NSA_HEADS - NSA_KV, ncp), F32)], axis=0))
    hi, lo = _split_bf16(jnp.concatenate(pcs, axis=0))
    ssum = _dot(hi, ov_ref[...]) + _dot(lo, ov_ref[...])
    blk_id = lax.broadcasted_iota(I32, ssum.shape, 1)
    cur = past // SEL_BLK
    forced = (blk_id == 0) | (blk_id == cur) | (blk_id == cur - 1)
    score = jnp.where(forced, FORCE_SCORE, jnp.where(blk_id <= cur, ssum, -1.0))
    score = jnp.where(blk_id < n_sel, score, -jnp.inf)
    idx = _topk_select(score, min(TOP_N, n_sel))
    idx_ref[...] = idx.reshape(sb, NSA_HEADS, LANES)


def _sample_score(qz, kc, vct, tab8, ov, past, n_sel):
    nb = qz.shape[0]
    ncp = kc.shape[1]
    sb = math.gcd(nb, _SCORE_BATCH)
    blk8 = pl.BlockSpec((sb, NSA_HEADS, LANES), lambda i: (i, 0, 0))
    blkc = pl.BlockSpec((sb, ncp, LANES), lambda i: (i, 0, 0))
    return pl.pallas_call(
        functools.partial(_sample_score_kernel, past=past, n_sel=n_sel),
        grid=(nb // sb,),
        in_specs=[blk8, blkc, pl.BlockSpec((sb, LANES, ncp), lambda i: (i, 0, 0)), _full(tab8.shape),
                  _full(ov.shape)],
        out_specs=[blk8, blk8],
        out_shape=[jax.ShapeDtypeStruct((nb, NSA_HEADS, LANES), F32)] * 2,
        compiler_params=_cparams(("parallel",)),
    )(qz, kc, vct, tab8, ov)


def _sample_attn_kernel(pt_ref, idx_ref, pool_ref, q8_ref, qz_ref, idxv_ref, ex_ref, oc_ref, gate_ref, selnew_ref,
                        winnew_ref, winnewt_ref, win_ref, mq_ref, mem_ref, tab_ref,
                        onsa_ref, omem_ref, wout_ref, buf, sem, *, past):
    b = pl.program_id(0)
    nb = pl.num_programs(0)
    page_rows = pool_ref.shape[4]
    n_top = buf.shape[4] // page_rows
    cur = past // SEL_BLK
    blocks_pp = page_rows // SEL_BLK

    def blk_copy(step, slot, j):
        g, n = j // n_top, j % n_top
        blk = jnp.minimum(idx_ref[step, j], cur - 1)
        page = pt_ref[step, blk // blocks_pp]
        c0 = pl.multiple_of(n * page_rows, page_rows)
        return pltpu.make_async_copy(pool_ref.at[page, :, g], buf.at[slot, g, :, :, pl.ds(c0, page_rows)],
                                     sem.at[slot])

    def fetch(step, slot):
        def body(j, c):
            blk_copy(step, slot, j).start()
            return c
        lax.fori_loop(0, NSA_KV * n_top, body, 0)

    slot = b % 2

    @pl.when(b == 0)
    def _():
        fetch(0, 0)

    @pl.when(b + 1 < nb)
    def _():
        fetch(b + 1, 1 - slot)

    hrow = lax.broadcasted_iota(I32, (NSA_HEADS, 1), 0)
    hgrp = hrow // NSA_GRP
    tab0 = tab_ref[:, 0:1]

    def half_of_group(x):
        return jnp.where(hgrp == 0, x[:, :NSA_HD], x[:, NSA_HD:])

    qz = qz_ref[0]
    xw = win_ref[0]
    wb_rows = xw.shape[1]
    wn = winnew_ref[0]
    j = lax.broadcasted_iota(I32, (1, wb_rows), 1)
    dw = wb_rows - j
    s = _dot(qz.astype(BF16), xw[:LANES].astype(BF16)) + _row_bias(dw, tab_ref)
    s_new = jnp.sum(qz * wn[:, :LANES], axis=-1, keepdims=True) + tab0
    mask = jnp.broadcast_to(dw < WINDOW, s.shape)
    zm = jnp.where(mask, s, NEG)
    m = jnp.maximum(jnp.max(zm, axis=-1, keepdims=True), s_new)
    e = jnp.exp(zm - m) * mask.astype(F32)
    e_new = jnp.exp(s_new - m)
    ow = (_dot_nt(e.astype(BF16), xw[LANES:].astype(BF16)) + e_new * wn[:, LANES:]) / jnp.maximum(
        jnp.sum(e, axis=-1, keepdims=True) + e_new, 1e-30)
    ow = half_of_group(ow)
    seq_lane = lax.broadcasted_iota(I32, winnewt_ref.shape, 1)
    new_col = jnp.sum(jnp.where(seq_lane == b, winnewt_ref[...], 0.0), axis=-1, keepdims=True)
    out_lane = lax.broadcasted_iota(I32, xw.shape, 1)
    wout_ref[0] = jnp.where(out_lane == wb_rows - 1, new_col, pltpu.roll(xw, wb_rows - 1, axis=1))

    mq = mq_ref[0].astype(BF16)
    n_mem = mem_ref.shape[1] // (2 * MEM_HEADS)
    omem = jnp.zeros((NSA_HEADS, MEM_HD), F32)
    for h in range(MEM_HEADS):
        kh = mem_ref[0, pl.ds(h, n_mem, stride=2 * MEM_HEADS), :].astype(BF16)
        vh = mem_ref[0, pl.ds(MEM_HEADS + h, n_mem, stride=2 * MEM_HEADS), :].astype(BF16)
        sm = _dot_nt(mq, kh) * (MEM_HD ** -0.5)
        em = jnp.exp(sm - jnp.max(sm, axis=-1, keepdims=True))
        pm = em / jnp.sum(em, axis=-1, keepdims=True)
        omem = jnp.where(hrow == h, _dot(pm.astype(BF16), vh), omem)
    omem_ref[0] = omem

    def wait_body(jj, c):
        blk_copy(b, slot, jj).wait()
        return c

    lax.fori_loop(0, NSA_KV * n_top, wait_body, 0)
    q8 = q8_ref[0]
    q8b = q8.astype(BF16)
    sn = selnew_ref[0]
    ncols = n_top * page_rows
    col = lax.broadcasted_iota(I32, (1, ncols), 1)
    idxe = _dot(idxv_ref[0].astype(BF16), ex_ref[...])
    osel = jnp.zeros((NSA_HEADS, NSA_HD), F32)
    for g in range(NSA_KV):
        kt = buf[slot, g, 0].astype(BF16)
        vt = buf[slot, g, 1].astype(BF16)
        blk = idxe[g:g + 1, :].astype(I32)
        in_page = col % page_rows
        spos = blk * SEL_BLK + in_page % SEL_BLK
        mask = (blk < cur) & (in_page // SEL_BLK == blk % blocks_pp)
        sg = _dot(q8b, kt) + _row_bias(past - spos, tab_ref)
        s_new = jnp.sum(q8 * sn[:, g * NSA_HD:(g + 1) * NSA_HD], axis=-1, keepdims=True) + tab0
        new_sel = jnp.max(jnp.where(blk == cur, 1.0, 0.0), axis=-1, keepdims=True) > 0.5
        maskb = jnp.broadcast_to(mask, sg.shape)
        zm = jnp.where(maskb, sg, NEG)
        z_new = jnp.where(new_sel, s_new, NEG)
        m = jnp.maximum(jnp.max(zm, axis=-1, keepdims=True), z_new)
        e = jnp.exp(zm - m) * maskb.astype(F32)
        e_new = jnp.exp(z_new - m) * new_sel.astype(F32)
        v_new = sn[:, (NSA_KV + g) * NSA_HD:(NSA_KV + g + 1) * NSA_HD]
        og = (_dot_nt(e.astype(BF16), vt) + e_new * v_new) / jnp.maximum(
            jnp.sum(e, axis=-1, keepdims=True) + e_new, 1e-30)
        osel = jnp.where(hgrp == g, og, osel)

    gates = gate_ref[0]
    onsa_ref[0] = gates[:, 0:1] * half_of_group(oc_ref[0]) + gates[:, 1:2] * osel + gates[:, 2:3] * ow


def _sample_attn(table, idx2d, pool, q8, qz, idxv, ex, oc, gate8, selnew, winnew, winnewt, win, mq8, mem, tab8, past,
                 n_top):
    nb = q8.shape[0]
    page_rows = pool.shape[4]
    blk3 = lambda a: pl.BlockSpec((1,) + a.shape[1:], lambda i, pt, ix: (i, 0, 0))
    whole = lambda a: pl.BlockSpec(a.shape, lambda i, pt, ix: (0, 0))
    gs = pltpu.PrefetchScalarGridSpec(
        num_scalar_prefetch=2,
        grid=(nb,),
        in_specs=[pl.BlockSpec(memory_space=pl.ANY), blk3(q8), blk3(qz), blk3(idxv), whole(ex), blk3(oc), blk3(gate8),
                  blk3(selnew), blk3(winnew), whole(winnewt), blk3(win), blk3(mq8), blk3(mem), whole(tab8)],
        out_specs=[pl.BlockSpec((1, NSA_HEADS, NSA_HD), lambda i, pt, ix: (i, 0, 0)),
                   pl.BlockSpec((1, NSA_HEADS, MEM_HD), lambda i, pt, ix: (i, 0, 0)),
                   pl.BlockSpec((1,) + win.shape[1:], lambda i, pt, ix: (i, 0, 0))],
        scratch_shapes=[pltpu.VMEM((2, NSA_KV, 2, NSA_HD, n_top * page_rows), F32), pltpu.SemaphoreType.DMA((2,))],
    )
    return pl.pallas_call(
        functools.partial(_sample_attn_kernel, past=past),
        grid_spec=gs,
        out_shape=[jax.ShapeDtypeStruct((nb, NSA_HEADS, NSA_HD), F32),
                   jax.ShapeDtypeStruct((nb, NSA_HEADS, MEM_HD), F32),
                   jax.ShapeDtypeStruct(win.shape, F32)],
        compiler_params=_cparams(("arbitrary",)),
    )(table, idx2d, pool, q8, qz, idxv, ex, oc, gate8, selnew, winnew, winnewt, win, mq8, mem, tab8)


def _merge_ffn_kernel(x_ref, ol_ref, on_ref, om_ref, mg_ref, wa_ref, wb_ref, wm_ref, wo_ref, gf_ref, wg_ref,
                      wu_ref, wd_ref, y_ref):
    d = x_ref.shape[1]
    mg = mg_ref[...]
    z = (mg[:, 0:d] * _dot(ol_ref[...], wa_ref[...]) + mg[:, d:2 * d] * _dot(on_ref[...], wb_ref[...])
         + mg[:, 2 * d:3 * d] * _dot(om_ref[...], wm_ref[...]))
    h = x_ref[...] + _dot(z.astype(BF16), wo_ref[...])
    f = (h * lax.rsqrt(jnp.mean(h * h, axis=-1, keepdims=True) + EPS) * gf_ref[...]).astype(BF16)
    a = jax.nn.silu(_dot(f, wg_ref[...])) * _dot(f, wu_ref[...])
    y_ref[...] = h + _dot(a.astype(BF16), wd_ref[...])


def _merge_ffn(x2d, ol, on, om, mg, prm, tm):
    n, d = x2d.shape
    row = lambda a: pl.BlockSpec((tm, a.shape[1]), lambda i: (i, 0))
    consts = [prm['w_up_a'], prm['w_up_b'], prm['w_up_m'], prm['w_o'], prm['norm_ffn'], prm['w_ffn_gate'],
              prm['w_ffn_up'], prm['w_ffn_down']]
    acts = [x2d, ol, on, om, mg]
    return pl.pallas_call(
        _merge_ffn_kernel,
        grid=(n // tm,),
        in_specs=[row(a) for a in acts] + [pl.BlockSpec(c.shape, lambda i: (0, 0), pipeline_mode=pl.Buffered(1))
                                           for c in consts],
        out_specs=pl.BlockSpec((tm, d), lambda i: (i, 0)),
        out_shape=jax.ShapeDtypeStruct((n, d), F32),
        compiler_params=_cparams(("parallel",)),
    )(*acts, *consts)


def _round_up(n, m):
    return -(-n // m) * m


def _overlap(ncp, n_cmp, nsp):
    cs = jnp.arange(ncp)[:, None] * CMP_STRIDE
    ss = jnp.arange(nsp)[None, :] * SEL_BLK
    hit = (cs < ss + SEL_BLK) & (cs + CMP_LEN > ss) & (jnp.arange(ncp)[:, None] < n_cmp)
    return hit.astype(BF16)


def _prep_params(norm_mix, w_in, conv_w, conv_b, w_lru_a, b_lru_a, w_lru_i, b_lru_i, lru_lambda, g_nsa_q, g_nsa_k,
                 pe_cmp_k, w_cmp_k1, w_cmp_k2, pe_cmp_v, w_cmp_v1, w_cmp_v2, norm_mem, w_mem_kv, g_mem_q, g_mem_k,
                 w_up_a, w_up_b, w_up_m, w_o, norm_ffn, w_ffn_gate, w_ffn_up, w_ffn_down):
    row = lambda v: v.reshape(1, -1).astype(F32)
    n_gg = N_BRANCH * NSA_GRP
    w = w_in
    zpad = jnp.zeros((w.shape[0], LANES - n_gg), w.dtype)
    ng0 = _OFF_NG
    w_packed = jnp.concatenate(
        [w[:, :ng0], w[:, ng0:ng0 + n_gg], zpad, w[:, ng0 + n_gg:ng0 + 2 * n_gg], zpad, w[:, ng0 + 2 * n_gg:]],
        axis=1).astype(BF16)
    eye = jnp.eye(NSA_KV, dtype=F32)

    def w1_big(w1):
        wr = w1.reshape(2, CMP_STRIDE, NSA_HD, CMP_HID)
        big = jnp.einsum('hrdj,gk->rgdhkj', wr, eye)
        return big.reshape(CMP_STRIDE * NSA_KV * NSA_HD, 2 * NSA_KV * CMP_HID).astype(BF16)

    def w2_bd(w2):
        return jnp.einsum('jd,gk->gjkd', w2, eye).reshape(NSA_KV * CMP_HID, NSA_KV * NSA_HD).astype(BF16)

    pe8 = lambda pe: jnp.broadcast_to(pe.reshape(1, -1), (8, pe.size)).astype(BF16)
    return dict(
        norm_mix=row(norm_mix), w_in=w_packed,
        bd_q=_block_diag_ones(NSA_HEADS * NSA_HD, NSA_HD), bd_k=_block_diag_ones(NSA_KV * NSA_HD, NSA_HD),
        bd_m=_block_diag_ones(MEM_HEADS * MEM_HD, MEM_HD),
        gq=row(jnp.tile(g_nsa_q, NSA_HEADS)), gk_cmp=row(jnp.tile(g_nsa_k[0], NSA_KV)),
        gk_sel=row(jnp.tile(g_nsa_k[1], NSA_KV)), gk_win=row(jnp.tile(g_nsa_k[2], NSA_KV)),
        gmq=row(jnp.tile(g_mem_q, MEM_HEADS)), gmk=row(jnp.tile(g_mem_k, MEM_HEADS)),
        conv_w=conv_w.astype(F32), conv_b=row(conv_b), w_lru_a=w_lru_a.astype(BF16), w_lru_i=w_lru_i.astype(BF16),
        b_lru_a=row(b_lru_a), b_lru_i=row(b_lru_i), lru_lambda=row(lru_lambda),
        w1k_big=w1_big(w_cmp_k1), w1v_big=w1_big(w_cmp_v1), w1k=w_cmp_k1.astype(BF16), w1v=w_cmp_v1.astype(BF16),
        pe_k=pe8(pe_cmp_k), pe_v=pe8(pe_cmp_v), w2k_bd=w2_bd(w_cmp_k2), w2v_bd=w2_bd(w_cmp_v2),
        norm_mem=row(norm_mem), w_mem_kv=w_mem_kv.astype(BF16),
        w_up_a=w_up_a.astype(BF16), w_up_b=w_up_b.astype(BF16), w_up_m=w_up_m.astype(BF16), w_o=w_o.astype(BF16),
        norm_ffn=row(norm_ffn), w_ffn_gate=w_ffn_gate.astype(BF16), w_ffn_up=w_ffn_up.astype(BF16),
        w_ffn_down=w_ffn_down.astype(BF16))


def _feature_major(cache):
    return jnp.transpose(cache, (0, 2, 3, 4, 1))


def kernel(x_prompt, x_sample, mem_prompt, cache_cmp_kv, cache_sel_kv, page_table, cache_win_kv, cache_mem_kv,
           state_lru_h, state_conv, rel_bias, norm_mix, w_in, conv_w, conv_b, w_lru_a, b_lru_a, w_lru_i, b_lru_i,
           lru_lambda, g_nsa_q, g_nsa_k, pe_cmp_k, w_cmp_k1, w_cmp_k2, pe_cmp_v, w_cmp_v1, w_cmp_v2, norm_mem,
           w_mem_kv, g_mem_q, g_mem_k, w_up_a, w_up_b, w_up_m, w_o, norm_ffn, w_ffn_gate, w_ffn_up, w_ffn_down):
    assert norm_mix.shape[0] == 1 and x_sample.shape[1] == 1
    weights = (norm_mix, w_in, conv_w, conv_b, w_lru_a, b_lru_a, w_lru_i, b_lru_i, lru_lambda, g_nsa_q, g_nsa_k,
               pe_cmp_k, w_cmp_k1, w_cmp_k2, pe_cmp_v, w_cmp_v1, w_cmp_v2, norm_mem, w_mem_kv, g_mem_q, g_mem_k,
               w_up_a, w_up_b, w_up_m, w_o, norm_ffn, w_ffn_gate, w_ffn_up, w_ffn_down)
    prm = _prep_params(*[w[0] for w in weights])
    bsz, t, d = x_prompt.shape
    db = x_sample.shape[0]
    n_pages = page_table.shape[1]
    page_rows = cache_cmp_kv.shape[2]
    past = n_pages * page_rows
    n_sel = -(-(past + 1) // SEL_BLK)
    assert t % _FAR_TK == 0 and page_rows % LANES == 0 and n_sel <= 256

    xp2 = x_prompt.reshape(bsz * t, d)
    (lx, gg, q, cmp_p, sel_p, win_p, selk, wink, mq, mg, selvt, winvt, ngt) = _project(xp2, prm, 256, bsz)
    o_lru, h_p, cv_p = _lru_prompt(lx.reshape(bsz, t, d), gg.reshape(bsz, t, d), prm, 256)

    chunks_pp = page_rows // CMP_STRIDE
    pages_p = t // page_rows
    ncp_p = _round_up(t // CMP_STRIDE, LANES)
    pool_p = cmp_p.reshape(bsz * pages_p, chunks_pp, CHUNK_W)
    table_p = jnp.arange(bsz * pages_p, dtype=I32).reshape(bsz, pages_p)
    kc_p, vct_p = _compress(pool_p, table_p, prm, ncp_p, feature_major=False)

    nsp_p = _round_up(t // SEL_BLK, LANES)
    wb, cb = _bias_tiles(rel_bias.astype(F32), ncp_p)
    ovt_p = _overlap(ncp_p, t // CMP_STRIDE - 1, nsp_p).T
    ext_p = ((jnp.arange(t) // SEL_BLK)[:, None] == jnp.arange(nsp_p)[None, :]).astype(BF16)
    o_nsa = _nsa_prompt(q.reshape(bsz, t, -1), kc_p, vct_p, selk.reshape(bsz, t, LANES), selvt,
                        wink.reshape(bsz, t, LANES), winvt, ngt, wb, cb, ovt_p, ext_p)

    m_rows = mem_prompt.shape[1]
    mkv = _mem_kv(mem_prompt.reshape(bsz * m_rows, d), prm)
    o_mem = _mem_attn(mq.reshape(bsz, t, -1), mkv.reshape(bsz, m_rows, -1), 256)

    y_p = _merge_ffn(xp2, o_lru.reshape(bsz * t, d), o_nsa.reshape(bsz * t, -1), o_mem.reshape(bsz * t, -1), mg,
                     prm, 256)

    xs2 = x_sample.reshape(db, d)
    (lx_s, gg_s, q_s, cmp_s, sel_s, win_s, _, _, mq_s, mg_s, _, _, ngt_s) = _project(xs2, prm, db, 1)
    cv0 = jnp.transpose(state_conv[0], (1, 0, 2))
    o_lru_s, h_s, cv_s = _lru_sample(lx_s, gg_s, cv0, state_lru_h[0], prm)

    ncp_s = _round_up(past // CMP_STRIDE, LANES)
    pool_c = _feature_major(cache_cmp_kv[0]).reshape(-1, KV_W, page_rows)
    kc_s, vct_s = _compress(pool_c, page_table, prm, ncp_s, feature_major=True)

    nsp_s = _round_up(n_sel, LANES)
    n_top = min(TOP_N, n_sel)
    ov_s = _overlap(ncp_s, (past + 1) // CMP_STRIDE - 1, nsp_s)
    tab8 = jnp.zeros((NSA_HEADS, LANES), F32).at[:, :REL_BUCKETS].set(rel_bias.astype(F32).T)
    q8 = q_s.reshape(db, NSA_HEADS, NSA_HD)
    qz = jnp.concatenate([jnp.where(jnp.arange(NSA_HEADS)[None, :, None] // NSA_GRP == gi, q8, 0.0)
                          for gi in range(NSA_KV)], axis=2)
    oc_s, idx = _sample_score(qz, kc_s, vct_s, tab8, ov_s, past, n_sel)
    idx2d = idx[:, :NSA_KV, :n_top].astype(I32).reshape(db, NSA_KV * n_top)

    n_gg = N_BRANCH * NSA_GRP
    gate8 = jnp.concatenate([ngt_s[0, gi * LANES:gi * LANES + n_gg, :] for gi in range(NSA_KV)], axis=0).T
    gate8 = jnp.pad(gate8.reshape(db, NSA_HEADS, N_BRANCH), ((0, 0), (0, 0), (0, LANES - N_BRANCH)))
    mq8 = jnp.pad(mq_s.astype(F32).reshape(db, MEM_HEADS, MEM_HD), ((0, 0), (0, NSA_HEADS - MEM_HEADS), (0, 0)))
    ex_s = (jnp.arange(LANES)[:, None] == (jnp.arange(n_top * page_rows) // page_rows)[None, :]).astype(BF16)
    pool_s = _feature_major(cache_sel_kv[0])
    win_t = _feature_major(cache_win_kv[0])
    wb_rows = win_t.shape[-1]
    mem_s = cache_mem_kv[0]
    n_mem = mem_s.shape[1]
    o_nsa8, o_mem8, win_new = _sample_attn(
        page_table, idx2d, pool_s, q8, qz, idx, ex_s, oc_s, gate8, sel_s.reshape(db, 1, KV_W),
        win_s.reshape(db, 1, KV_W), win_s.T, win_t.reshape(db, KV_W, wb_rows), mq8,
        mem_s.reshape(db, n_mem * 2 * MEM_HEADS, MEM_HD), tab8, past, n_top)
    o_nsa_s = o_nsa8.reshape(db, NSA_HEADS * NSA_HD)
    o_mem_s = o_mem8[:, :MEM_HEADS].reshape(db, MEM_HEADS * MEM_HD)
    y_s = _merge_ffn(xs2, o_lru_s, o_nsa_s.astype(BF16), o_mem_s.astype(BF16), mg_s, prm, db)

    kv6 = lambda a, lead: a.reshape((1,) + lead + (2, NSA_KV, NSA_HD))
    w_keep = min(WINDOW, t)
    win_out = jnp.transpose(win_new.reshape(db, 2, NSA_KV, NSA_HD, wb_rows), (0, 4, 1, 2, 3))
    return (y_p.reshape(bsz, t, d), y_s.reshape(db, 1, d),
            kv6(cmp_p, (bsz, t)), kv6(cmp_s, (db, 1)), kv6(sel_p, (bsz, t)), kv6(sel_s, (db, 1)),
            kv6(win_p.reshape(bsz, t, KV_W)[:, t - w_keep:], (bsz, w_keep)), win_out[None],
            mkv.reshape(1, bsz, m_rows, 2, MEM_HEADS, MEM_HD),
            h_p.reshape(1, bsz, d), h_s.reshape(1, db, d),
            cv_p.reshape(1, bsz, CONV_W - 1, d), jnp.transpose(cv_s, (1, 0, 2))[None])
```

```python
import functools
import math

import jax
import jax.numpy as jnp
import numpy as np
from jax import lax
from jax.experimental import pallas as pl
from jax.experimental.pallas import tpu as pltpu

F32 = jnp.float32
BF16 = jnp.bfloat16
I32 = jnp.int32

EPS = 1e-6
NEG = -1e30
FORCE_SCORE = 1e6
LOG2E = math.log2(math.e)
LRU_C = 8.0
LRU_BLOCKS = 4
CONV_W = 4
NSA_HEADS = 8
NSA_KV = 2
NSA_GRP = NSA_HEADS // NSA_KV
NSA_HD = 64
CMP_STRIDE = 16
CMP_LEN = 2 * CMP_STRIDE
CMP_HID = 2 * NSA_HD
SEL_BLK = 64
TOP_N = 16
WINDOW = 512
Q_BLOCK = 128
MEM_HEADS = 4
MEM_HD = 128
REL_BUCKETS = 32
REL_MAX_DIST = 128
N_BRANCH = 3

LANES = 128
KV_W = 2 * NSA_KV * NSA_HD
CHUNK_W = CMP_STRIDE * KV_W
VMEM_LIMIT = 56 * 1024 * 1024


def _cparams(sem):
    return pltpu.CompilerParams(dimension_semantics=sem, vmem_limit_bytes=VMEM_LIMIT)


def _full(shape):
    n = len(shape)
    return pl.BlockSpec(shape, lambda *_: (0,) * n)


def _dot(a, b):
    return jnp.dot(a, b, preferred_element_type=F32)


def _dot_nt(a, b):
    return lax.dot_general(a, b, (((1,), (1,)), ((), ())), preferred_element_type=F32)


def _group_rms(z, ones_bd, gain, width):
    ss = _dot((z * z).astype(BF16), ones_bd)
    return z * lax.rsqrt(ss * (1.0 / width) + EPS) * gain


def _masked_softmax(z, mask):
    zm = jnp.where(mask, z, NEG)
    e = jnp.exp(zm - jnp.max(zm, axis=-1, keepdims=True)) * mask.astype(F32)
    return e / jnp.maximum(jnp.sum(e, axis=-1, keepdims=True), 1e-30)


def _t5_thresholds():
    exact = REL_BUCKETS // 2
    n = np.arange(1, 2 * REL_MAX_DIST + 1, dtype=np.float32)
    scaled = np.log(n / np.float32(exact)) / np.float32(math.log(REL_MAX_DIST / exact)) * np.float32(REL_BUCKETS - exact)
    bucket = np.minimum(exact + scaled.astype(np.int32), REL_BUCKETS - 1)
    assert np.all(np.diff(bucket[exact - 1:]) >= 0) and bucket[-1] == REL_BUCKETS - 1
    return [int(np.argmax(bucket >= b)) + 1 for b in range(exact + 1, REL_BUCKETS)]


_T5_THRESHOLDS = _t5_thresholds()


def _t5_bucket(dist):
    n = jnp.maximum(dist, 0)
    exact = REL_BUCKETS // 2
    large = exact + sum((n >= t).astype(I32) for t in _T5_THRESHOLDS)
    return jnp.where(n < exact, n, large)


def _block_diag_ones(n, width):
    i = jnp.arange(n) // width
    return (i[:, None] == i[None, :]).astype(BF16)


def _split_bf16(x):
    hi = x.astype(BF16)
    return hi, (x - hi.astype(F32)).astype(BF16)


_D = 1024
_NG_W = NSA_KV * LANES
_OFF_LX, _OFF_LG, _OFF_Q, _OFF_KV, _OFF_NG, _OFF_MQ, _OFF_MG, _OFF_END = (
    0, 1024, 2048, 2560, 3328, 3584, 4096, 7168)


def _proj_kernel(x_ref, g_ref, w_ref, bdq_ref, bdk_ref, bdm_ref, gq_ref, gks_ref, gkw_ref, gmq_ref,
                 lx_ref, lg_ref, q_ref, cmp_ref, selk_ref, wink_ref, mq_ref, mg_ref,
                 cmpt_ref, selt_ref, wint_ref, selvt_ref, winvt_ref, ngt_ref):
    x = x_ref[...]
    xn = (x * lax.rsqrt(jnp.mean(x * x, axis=-1, keepdims=True) + EPS) * g_ref[...]).astype(BF16)

    def seg(a, b):
        return _dot(xn, w_ref[:, a:b])

    lx_ref[...] = seg(_OFF_LX, _OFF_LG)
    lg_ref[...] = jax.nn.gelu(seg(_OFF_LG, _OFF_Q))
    zq = seg(_OFF_Q, _OFF_KV)
    q_ref[...] = _group_rms(zq, bdq_ref[...], gq_ref[...], NSA_HD) * (NSA_HD ** -0.5)
    zc = seg(_OFF_KV, _OFF_KV + KV_W)
    cmp_ref[...] = zc
    cmpt_ref[0] = zc.T
    for off, gain_ref, t_ref, k_ref, vt_ref in ((_OFF_KV + KV_W, gks_ref, selt_ref, selk_ref, selvt_ref),
                                                (_OFF_KV + 2 * KV_W, gkw_ref, wint_ref, wink_ref, winvt_ref)):
        z = seg(off, off + KV_W)
        kn = _group_rms(z[:, :LANES], bdk_ref[...], gain_ref[...], NSA_HD)
        vt = z[:, LANES:].T
        t_ref[0, :LANES, :] = kn.T
        t_ref[0, LANES:, :] = vt
        k_ref[...] = kn.astype(BF16)
        vt_ref[0] = vt.astype(BF16)
    ngt_ref[0] = jax.nn.sigmoid(seg(_OFF_NG, _OFF_MQ)).T
    zm = seg(_OFF_MQ, _OFF_MG)
    mq_ref[...] = _group_rms(zm, bdm_ref[...], gmq_ref[...], MEM_HD).astype(BF16)
    mg_ref[...] = jax.nn.sigmoid(seg(_OFF_MG, _OFF_END))


def _project(x2d, prm, tm, bsz):
    n = x2d.shape[0]
    t = n // bsz
    per_b = t // tm
    row = lambda w: pl.BlockSpec((tm, w), lambda i: (i, 0))
    fmaj = lambda w: pl.BlockSpec((1, w, tm), lambda i: (i // per_b, 0, i % per_b))
    consts = [prm['norm_mix'], prm['w_in'], prm['bd_q'], prm['bd_k'], prm['bd_m'],
              prm['gq'], prm['gk_sel'], prm['gk_win'], prm['gmq']]
    widths = [(_D, F32), (_D, F32), (512, F32), (KV_W, F32), (LANES, BF16), (LANES, BF16), (512, BF16), (3 * _D, F32)]
    fwidths = [(KV_W, F32), (KV_W, F32), (KV_W, F32), (LANES, BF16), (LANES, BF16), (_NG_W, F32)]
    return pl.pallas_call(
        _proj_kernel,
        grid=(n // tm,),
        in_specs=[row(_D)] + [_full(c.shape) for c in consts],
        out_specs=[row(w) for w, _ in widths] + [fmaj(w) for w, _ in fwidths],
        out_shape=[jax.ShapeDtypeStruct((n, w), dt) for w, dt in widths]
                  + [jax.ShapeDtypeStruct((bsz, w, t), dt) for w, dt in fwidths],
        compiler_params=_cparams(("parallel",)),
    )(x2d, *consts)


def _lru_gates(xc, wa_ref, wi_ref, ba_ref, bi_ref, lam_ref):
    bw = xc.shape[1] // LRU_BLOCKS
    ra, ri = [], []
    for n in range(LRU_BLOCKS):
        xr = xc[:, n * bw:(n + 1) * bw].astype(BF16)
        ra.append(_dot(xr, wa_ref[n]))
        ri.append(_dot(xr, wi_ref[n]))
    r = jax.nn.sigmoid(jnp.concatenate(ra, axis=1) + ba_ref[...])
    i = jax.nn.sigmoid(jnp.concatenate(ri, axis=1) + bi_ref[...])
    lam = -lam_ref[...]
    softplus = jnp.maximum(lam, 0.0) + jnp.log1p(jnp.exp(-jnp.abs(lam)))
    log_a = -LRU_C * r * softplus
    a = jnp.exp(log_a)
    b = jnp.sqrt(jnp.tanh(-log_a) * (a * a + 1.0)) * (i * xc)
    return a, b


def _lru_scan(a, b):
    tt = a.shape[0]
    row = lax.broadcasted_iota(I32, a.shape, 0)
    k = 1
    while k < tt:
        keep = row >= k
        b = b + a * jnp.where(keep, pltpu.roll(b, k, axis=0), 0.0)
        a = a * jnp.where(keep, pltpu.roll(a, k, axis=0), 1.0)
        k *= 2
    return a, b


def _lru_prompt_kernel(x_ref, gg_ref, cw_ref, cb_ref, wa_ref, wi_ref, ba_ref, bi_ref, lam_ref,
                       o_ref, h_ref, cv_ref, xbuf, hcar):
    t = pl.program_id(1)
    tt = x_ref.shape[1]

    @pl.when(t == 0)
    def _():
        xbuf[0:8, :] = jnp.zeros((8, xbuf.shape[1]), F32)
        hcar[...] = jnp.zeros(hcar.shape, F32)

    x = x_ref[0]
    xbuf[8:8 + tt, :] = x
    xc = cb_ref[...] + xbuf[5:5 + tt, :] * cw_ref[0:1, :]
    for k in range(1, CONV_W):
        xc = xc + xbuf[5 + k:5 + k + tt, :] * cw_ref[k:k + 1, :]
    a, b = _lru_gates(xc, wa_ref, wi_ref, ba_ref, bi_ref, lam_ref)
    ap, hs = _lru_scan(a, b)
    h = hs + ap * hcar[...]
    o_ref[0] = (h * gg_ref[0]).astype(BF16)
    hcar[...] = h[tt - 1:tt, :]
    xbuf[0:8, :] = x[tt - 8:tt, :]

    @pl.when(t == pl.num_programs(1) - 1)
    def _():
        h_ref[0] = h[tt - 1:tt, :]
        cv_ref[0] = x[tt - (CONV_W - 1):tt, :]


def _lru_prompt(lx, gg, prm, tt):
    b, t, w = lx.shape
    blk = pl.BlockSpec((1, tt, w), lambda i, j: (i, j, 0))
    consts = [prm['conv_w'], prm['conv_b'], prm['w_lru_a'], prm['w_lru_i'], prm['b_lru_a'], prm['b_lru_i'],
              prm['lru_lambda']]
    return pl.pallas_call(
        _lru_prompt_kernel,
        grid=(b, t // tt),
        in_specs=[blk, blk] + [_full(c.shape) for c in consts],
        out_specs=[blk, pl.BlockSpec((1, 1, w), lambda i, j: (i, 0, 0)),
                   pl.BlockSpec((1, CONV_W - 1, w), lambda i, j: (i, 0, 0))],
        out_shape=[jax.ShapeDtypeStruct((b, t, w), BF16), jax.ShapeDtypeStruct((b, 1, w), F32),
                   jax.ShapeDtypeStruct((b, CONV_W - 1, w), F32)],
        scratch_shapes=[pltpu.VMEM((tt + 8, w), F32), pltpu.VMEM((1, w), F32)],
        compiler_params=_cparams(("parallel", "arbitrary")),
    )(lx, gg, *consts)


def _lru_sample_kernel(x_ref, gg_ref, cv0_ref, h0_ref, cw_ref, cb_ref, wa_ref, wi_ref, ba_ref, bi_ref, lam_ref,
                       o_ref, h_ref, cv_ref):
    x = x_ref[...]
    xc = cb_ref[...] + x * cw_ref[CONV_W - 1:CONV_W, :]
    for k in range(CONV_W - 1):
        xc = xc + cv0_ref[k] * cw_ref[k:k + 1, :]
    a, b = _lru_gates(xc, wa_ref, wi_ref, ba_ref, bi_ref, lam_ref)
    h = a * h0_ref[...] + b
    o_ref[...] = (h * gg_ref[...]).astype(BF16)
    h_ref[...] = h
    for k in range(CONV_W - 2):
        cv_ref[k] = cv0_ref[k + 1]
    cv_ref[CONV_W - 2] = x


def _lru_sample(lx, gg, cv0, h0, prm):
    n, w = lx.shape
    consts = [prm['conv_w'], prm['conv_b'], prm['w_lru_a'], prm['w_lru_i'], prm['b_lru_a'], prm['b_lru_i'],
              prm['lru_lambda']]
    args = [lx, gg, cv0, h0] + consts
    return pl.pallas_call(
        _lru_sample_kernel,
        grid=(1,),
        in_specs=[_full(a.shape) for a in args],
        out_specs=[_full((n, w)), _full((n, w)), _full((CONV_W - 1, n, w))],
        out_shape=[jax.ShapeDtypeStruct((n, w), BF16), jax.ShapeDtypeStruct((n, w), F32),
                   jax.ShapeDtypeStruct((CONV_W - 1, n, w), F32)],
        compiler_params=_cparams(("arbitrary",)),
    )(*args)


_PAGE_SECTIONS = 4


def _compress_kernel(pt_ref, pool_ref, w1k_ref, w1v_ref, w1kp_ref, w1vp_ref, pek_ref, pev_ref, w2k_ref, w2v_ref,
                     bdk_ref, gk_ref, kc_ref, vct_ref, buf, sem, *scratch, n_valid, feature_major):
    b = pl.program_id(0)
    nb = pl.num_programs(0)
    n_pages = pt_ref.shape[1]
    ncp = kc_ref.shape[1]

    def page_copy(step, slot, p):
        return pltpu.make_async_copy(pool_ref.at[pt_ref[step, p]], buf.at[slot, p], sem.at[slot])

    def fetch(step, slot):
        def body(p, c):
            page_copy(step, slot, p).start()
            return c
        lax.fori_loop(0, n_pages, body, 0)

    slot = b % 2

    @pl.when(b == 0)
    def _():
        fetch(0, 0)

    @pl.when(b + 1 < nb)
    def _():
        fetch(b + 1, 1 - slot)

    def wait_body(p, c):
        page_copy(b, slot, p).wait()
        return c

    lax.fori_loop(0, n_pages, wait_body, 0)

    n_sec = math.gcd(n_pages, _PAGE_SECTIONS)
    sec_pages = n_pages // n_sec
    if feature_major:
        zbuf, = scratch
        page_rows = buf.shape[3]
        n_ch = n_pages * page_rows // CMP_STRIDE
        sec_ch = n_ch // n_sec

        def chunk_rows(kv, sec):
            for p in range(sec * sec_pages, (sec + 1) * sec_pages):
                zbuf[kv, p * page_rows:(p + 1) * page_rows, :] = buf[slot, p, kv * LANES:(kv + 1) * LANES, :].T
            return jnp.concatenate(
                [zbuf[kv, pl.ds(sec * sec_ch * CMP_STRIDE + r, sec_ch, stride=CMP_STRIDE), :].astype(BF16)
                 for r in range(CMP_STRIDE)], axis=1)
    else:
        rows_pp = buf.shape[2]
        n_ch = n_pages * rows_pp
        sec_ch = n_ch // n_sec

        def chunk_rows(kv, sec):
            pages = slice(sec * sec_pages, (sec + 1) * sec_pages)
            return jnp.concatenate(
                [buf[slot, pages, :, r * KV_W + kv * LANES:r * KV_W + (kv + 1) * LANES]
                 .reshape(sec_ch, LANES).astype(BF16) for r in range(CMP_STRIDE)], axis=1)

    row = lax.broadcasted_iota(I32, (n_ch, 1), 0)
    keep = row < n_valid
    outs = []
    for kv, w1_ref, w1p_ref, pe_ref, w2_ref in ((0, w1k_ref, w1kp_ref, pek_ref, w2k_ref),
                                                 (1, w1v_ref, w1vp_ref, pev_ref, w2v_ref)):
        hh = jnp.concatenate([_dot(chunk_rows(kv, sec), w1_ref[...]) for sec in range(n_sec)], axis=0)
        pos = _dot(pe_ref[...], w1p_ref[...])[0:1, :]
        pos = jnp.concatenate([pos, pos], axis=1)
        nh = NSA_KV * CMP_HID
        h = hh[:, :nh] + pltpu.roll(hh[:, nh:], n_ch - 1, axis=0) + pos
        outs.append(_dot(jax.nn.gelu(h).astype(BF16), w2_ref[...]))
    kc = _group_rms(outs[0], bdk_ref[...], gk_ref[...], NSA_HD)
    kc = jnp.where(keep, kc, 0.0)
    vc = jnp.where(keep, outs[1], 0.0)
    if ncp > n_ch:
        pad = jnp.zeros((ncp - n_ch, kc.shape[1]), F32)
        kc = jnp.concatenate([kc, pad], axis=0)
        vc = jnp.concatenate([vc, pad], axis=0)
    kc_ref[0] = kc.astype(BF16)
    vct_ref[0] = vc.T.astype(BF16)


def _compress(pool, table, prm, ncp, feature_major):
    nb, n_pages = table.shape
    if feature_major:
        page_rows = pool.shape[2]
        n_ch = n_pages * page_rows // CMP_STRIDE
        scratch = [pltpu.VMEM((2, n_pages * page_rows, LANES), F32)]
    else:
        n_ch = n_pages * pool.shape[1]
        scratch = []
    consts = [prm['w1k_big'], prm['w1v_big'], prm['w1k'], prm['w1v'], prm['pe_k'], prm['pe_v'],
              prm['w2k_bd'], prm['w2v_bd'], prm['bd_k'], prm['gk_cmp']]
    out_blk = pl.BlockSpec((1, ncp, LANES), lambda i, pt: (i, 0, 0))
    gs = pltpu.PrefetchScalarGridSpec(
        num_scalar_prefetch=1,
        grid=(nb,),
        in_specs=[pl.BlockSpec(memory_space=pl.ANY)] + [pl.BlockSpec(c.shape, lambda i, pt, _n=len(c.shape): (0,) * _n)
                                                         for c in consts],
        out_specs=[out_blk, pl.BlockSpec((1, LANES, ncp), lambda i, pt: (i, 0, 0))],
        scratch_shapes=[pltpu.VMEM((2, n_pages) + pool.shape[1:], F32), pltpu.SemaphoreType.DMA((2,))] + scratch,
    )
    return pl.pallas_call(
        functools.partial(_compress_kernel, n_valid=n_ch - 1, feature_major=feature_major),
        grid_spec=gs,
        out_shape=[jax.ShapeDtypeStruct((nb, ncp, LANES), BF16), jax.ShapeDtypeStruct((nb, LANES, ncp), BF16)],
        compiler_params=_cparams(("arbitrary",)),
    )(table, pool, *consts)


_WIN_TILES = WINDOW // Q_BLOCK + 1
_QCOLS = NSA_GRP * Q_BLOCK


def _bias_tiles_kernel(tab_ref, wb_ref, cb_ref):
    g = pl.program_id(0)
    wrows = wb_ref.shape[1]
    ncp = cb_ref.shape[1] // 2
    c_w = lax.broadcasted_iota(I32, (wrows, Q_BLOCK), 0)
    i_w = lax.broadcasted_iota(I32, (wrows, Q_BLOCK), 1)
    d_w = (_WIN_TILES - 1) * Q_BLOCK + i_w - c_w
    k_c = lax.broadcasted_iota(I32, (2 * ncp, Q_BLOCK), 0)
    i_c = lax.broadcasted_iota(I32, (2 * ncp, Q_BLOCK), 1)
    d_c = i_c - (CMP_LEN - 1) - CMP_STRIDE * (k_c - ncp)
    for d, ok, ref in ((d_w, (d_w >= 0) & (d_w < WINDOW), wb_ref), (d_c, d_c >= 0, cb_ref)):
        bucket = _t5_bucket(d)
        for r in range(NSA_GRP):
            h = g * NSA_GRP + r
            acc = jnp.zeros(d.shape, F32)
            for bk in range(REL_BUCKETS):
                acc = jnp.where(bucket == bk, tab_ref[bk, h], acc)
            far = tab_ref[REL_BUCKETS - 1, h]
            ref[0, :, r * Q_BLOCK:(r + 1) * Q_BLOCK] = jnp.where(ok, (acc - far) * LOG2E, NEG)


def _bias_tiles(rel_bias, ncp):
    wrows = _WIN_TILES * Q_BLOCK
    return pl.pallas_call(
        _bias_tiles_kernel,
        grid=(NSA_KV,),
        in_specs=[pl.BlockSpec(memory_space=pltpu.SMEM)],
        out_specs=[pl.BlockSpec((1, wrows, _QCOLS), lambda g: (g, 0, 0)),
                   pl.BlockSpec((1, 2 * ncp, _QCOLS), lambda g: (g, 0, 0))],
        out_shape=[jax.ShapeDtypeStruct((NSA_KV, wrows, _QCOLS), F32),
                   jax.ShapeDtypeStruct((NSA_KV, 2 * ncp, _QCOLS), F32)],
        compiler_params=_cparams(("arbitrary",)),
    )(rel_bias)


def _topk_select(score, n_top):
    lane = lax.broadcasted_iota(I32, score.shape, 1).astype(F32)
    big = float(score.shape[1])
    slot = lax.broadcasted_iota(I32, (score.shape[0], LANES), 1)
    idx = jnp.zeros((score.shape[0], LANES), F32)
    for it in range(n_top):
        m = jnp.max(score, axis=-1, keepdims=True)
        first = jnp.min(jnp.where(score == m, lane, big), axis=-1, keepdims=True)
        score = jnp.where(lane == first, -jnp.inf, score)
        idx = jnp.where(slot == it, first, idx)
    return idx


def _topk_mask_cols(score, n_top):
    sid = lax.broadcasted_iota(I32, score.shape, 0).astype(F32)
    big = float(score.shape[0])
    sel = jnp.zeros(score.shape, F32)
    for _ in range(n_top):
        m = jnp.max(score, axis=0, keepdims=True)
        first = jnp.min(jnp.where(score == m, sid, big), axis=0, keepdims=True)
        hit = sid == first
        sel = jnp.where(hit, 1.0, sel)
        score = jnp.where(hit, -jnp.inf, score)
    return sel


_FAR_TK = 1024
_FAR_SPLIT = 4
_ONES_ROWS = 16


def _nsa_prompt_kernel(q_ref, kc_ref, vct_ref, selk_ref, selvt_ref, wink_ref, winvt_ref, gate_ref, wb_ref, cb_ref,
                       ovt_ref, ext_ref, o_ref, *, n_sel):
    bi = pl.program_id(1)
    g = pl.program_id(2)
    q0 = bi * Q_BLOCK
    ncp = kc_ref.shape[1]
    nsp = ovt_ref.shape[0]
    heads = range(NSA_GRP)

    lane = lax.broadcasted_iota(I32, (Q_BLOCK, LANES), 1)
    in_g = (lane // NSA_HD) == g
    qf = q_ref[0] * LOG2E
    qz = []
    for r in heads:
        blk = qf[:, (r // 2) * LANES:(r // 2 + 1) * LANES]
        blk = jnp.where((r % 2) == g, blk, pltpu.roll(blk, NSA_HD, axis=1))
        qz.append(jnp.where(in_g, blk, 0.0))
    qz = jnp.concatenate(qz, axis=0).astype(BF16)

    def col_max(z):
        while z.shape[0] > 64 and z.shape[0] % 16 == 0:
            half = z.shape[0] // 2
            z = jnp.maximum(z[:half], z[half:])
        return jnp.max(z, axis=0, keepdims=True)

    def col_sum(z):
        return jnp.sum(z, axis=0, keepdims=True)

    c0 = pl.multiple_of(ncp - bi * (Q_BLOCK // CMP_STRIDE), Q_BLOCK // CMP_STRIDE)
    z = _dot_nt(kc_ref[0], qz) + cb_ref[0, pl.ds(c0, ncp), :]
    m = col_max(z)
    e = jnp.exp2(z - m)
    pc = e * jnp.where(m > 0.5 * NEG, 1.0 / col_sum(e), 0.0)
    oc = _dot(vct_ref[0], pc.astype(BF16))

    pcs = pc[:, 0:Q_BLOCK]
    for r in range(1, NSA_GRP):
        pcs = pcs + pc[:, r * Q_BLOCK:(r + 1) * Q_BLOCK]
    hi, lo = _split_bf16(pcs)
    ssum = _dot(ovt_ref[...], hi) + _dot(ovt_ref[...], lo)
    s_id = lax.broadcasted_iota(I32, (nsp, Q_BLOCK), 0)
    cur = (q0 + lax.broadcasted_iota(I32, (1, Q_BLOCK), 1)) // SEL_BLK
    forced = (s_id == 0) | (s_id == cur) | (s_id == cur - 1)
    score = jnp.where(forced, FORCE_SCORE, jnp.where(s_id <= cur, ssum, -1.0))
    selm = _topk_mask_cols(score, min(TOP_N, n_sel)).T

    def with_mask(sel):
        u = ((sel - 1.0) * (-NEG)).astype(BF16)
        return jnp.concatenate([qz, jnp.concatenate([u] * NSA_GRP, axis=0)], axis=1)

    blk_id = lax.broadcasted_iota(I32, (Q_BLOCK, nsp), 1)
    qz_near = with_mask(selm)
    qz_far = with_mask(jnp.where(blk_id < 2 * bi - 2, selm, 0.0))

    p0 = pl.multiple_of(jnp.maximum(q0 - Q_BLOCK, 0), Q_BLOCK)
    d0 = pl.multiple_of(q0, Q_BLOCK)
    kn = jnp.concatenate(
        [jnp.concatenate([selk_ref[0, pl.ds(k0, Q_BLOCK), :], ext_ref[pl.ds(k0, Q_BLOCK), :]], axis=1)
         for k0 in (p0, d0)], axis=0)
    vn = jnp.concatenate([selvt_ref[0, :, pl.ds(p0, Q_BLOCK)], selvt_ref[0, :, pl.ds(d0, Q_BLOCK)]], axis=1)
    krow = lax.broadcasted_iota(I32, (2 * Q_BLOCK, 1), 0)
    wrows = wb_ref.shape[1]
    z = _dot_nt(kn, qz_near) + wb_ref[0, wrows - 2 * Q_BLOCK:wrows, :]
    z = z + jnp.where((krow < Q_BLOCK) & (bi == 0), NEG, 0.0)
    def with_ones(vt):
        return jnp.concatenate([vt, jnp.ones((_ONES_ROWS, vt.shape[1]), BF16)], axis=0)

    m = col_max(z)
    carry = (m, _dot(with_ones(vn), jnp.exp2((z - m).astype(BF16))))

    far_blocks = _FAR_TK // Q_BLOCK
    n_far = (bi + far_blocks - 2) // far_blocks

    sub = _FAR_TK // _FAR_SPLIT

    def far_body(t, carry):
        m, acc = carry
        starts = [pl.multiple_of(t * _FAR_TK + s * sub, sub) for s in range(_FAR_SPLIT)]
        zs = [_dot_nt(jnp.concatenate([selk_ref[0, pl.ds(k0, sub), :], ext_ref[pl.ds(k0, sub), :]], axis=1), qz_far)
              for k0 in starts]
        for k0, z in zip(starts, zs):
            m_new = jnp.maximum(m, col_max(z))
            e = jnp.exp2((z - m_new).astype(BF16))
            acc = jnp.exp2(m - m_new) * acc + _dot(with_ones(selvt_ref[0, :, pl.ds(k0, sub)]), e)
            m = m_new
        return m, acc

    _, acc = lax.fori_loop(0, n_far, far_body, carry)
    osel = acc[:LANES] / acc[LANES:LANES + 1]

    ks, vs = [], []
    for t in range(_WIN_TILES):
        st = pl.multiple_of(jnp.maximum(q0 - (_WIN_TILES - 1 - t) * Q_BLOCK, 0), Q_BLOCK)
        ks.append(wink_ref[0, pl.ds(st, Q_BLOCK), :])
        vs.append(winvt_ref[0, :, pl.ds(st, Q_BLOCK)])
    wrow = lax.broadcasted_iota(I32, (wrows, 1), 0)
    before_start = jnp.where(wrow < (_WIN_TILES - 1 - bi) * Q_BLOCK, NEG, 0.0)
    z = _dot_nt(jnp.concatenate(ks, axis=0), qz) + wb_ref[0] + before_start
    ow = _dot(with_ones(jnp.concatenate(vs, axis=1)), jnp.exp2((z - col_max(z)).astype(BF16)))
    ow = ow[:LANES] / ow[LANES:LANES + 1]

    gates = gate_ref[0]

    def gate_row(br):
        return jnp.concatenate([gates[r * N_BRANCH + br:r * N_BRANCH + br + 1, :] for r in heads], axis=1)

    mixed = gate_row(0) * oc + gate_row(1) * osel + gate_row(2) * ow
    mixed = jnp.where(g == 0, mixed[:NSA_HD], mixed[NSA_HD:])
    out = []
    for j in range(NSA_GRP // 2):
        pair = jnp.concatenate([mixed[:, (2 * j) * Q_BLOCK:(2 * j + 1) * Q_BLOCK],
                                mixed[:, (2 * j + 1) * Q_BLOCK:(2 * j + 2) * Q_BLOCK]], axis=0)
        out.append(pair.T)
    o_ref[0] = jnp.concatenate(out, axis=1).astype(BF16)


def _nsa_prompt(q, kc, vct, selk, selvt, wink, winvt, gates_t, wb, cb, ovt, ext):
    b, t, _ = q.shape
    ncp = kc.shape[1]
    gw = NSA_GRP * NSA_HD
    per_b = lambda shape: pl.BlockSpec((1,) + shape, lambda i, j, g: (i, 0, 0))
    per_g = lambda a: pl.BlockSpec((1,) + a.shape[1:], lambda i, j, g: (g, 0, 0))
    return pl.pallas_call(
        functools.partial(_nsa_prompt_kernel, n_sel=t // SEL_BLK),
        grid=(b, t // Q_BLOCK, NSA_KV),
        in_specs=[pl.BlockSpec((1, Q_BLOCK, gw), lambda i, j, g: (i, j, g)),
                  per_b((ncp, LANES)), per_b((LANES, ncp)),
                  per_b((t, LANES)), per_b((LANES, t)), per_b((t, LANES)), per_b((LANES, t)),
                  pl.BlockSpec((1, LANES, Q_BLOCK), lambda i, j, g: (i, g, j)),
                  per_g(wb), per_g(cb),
                  pl.BlockSpec(ovt.shape, lambda i, j, g: (0, 0)),
                  pl.BlockSpec(ext.shape, lambda i, j, g: (0, 0))],
        out_specs=pl.BlockSpec((1, Q_BLOCK, gw), lambda i, j, g: (i, j, g)),
        out_shape=jax.ShapeDtypeStruct((b, t, NSA_KV * gw), BF16),
        compiler_params=_cparams(("parallel", "parallel", "arbitrary")),
    )(q, kc, vct, selk, selvt, wink, winvt, gates_t, wb, cb, ovt, ext)


def _mem_kv_kernel(m_ref, g_ref, w_ref, bd_ref, gk_ref, o_ref):
    x = m_ref[...]
    xn = (x * lax.rsqrt(jnp.mean(x * x, axis=-1, keepdims=True) + EPS) * g_ref[...]).astype(BF16)
    z = _dot(xn, w_ref[...])
    half = MEM_HEADS * MEM_HD
    kn = _group_rms(z[:, :half], bd_ref[...], gk_ref[...], MEM_HD)
    o_ref[...] = jnp.concatenate([kn, z[:, half:]], axis=1)


def _mem_kv(mem2d, prm):
    n = mem2d.shape[0]
    args = [mem2d, prm['norm_mem'], prm['w_mem_kv'], prm['bd_m'], prm['gmk']]
    w = 2 * MEM_HEADS * MEM_HD
    return pl.pallas_call(
        _mem_kv_kernel,
        grid=(1,),
        in_specs=[_full(a.shape) for a in args],
        out_specs=_full((n, w)),
        out_shape=jax.ShapeDtypeStruct((n, w), F32),
        compiler_params=_cparams(("arbitrary",)),
    )(*args)


def _mem_attn_kernel(q_ref, kv_ref, o_ref):
    half = MEM_HEADS * MEM_HD
    out = []
    for h in range(MEM_HEADS):
        sl = slice(h * MEM_HD, (h + 1) * MEM_HD)
        k = kv_ref[0, :, sl].astype(BF16)
        v = kv_ref[0, :, half + h * MEM_HD:half + (h + 1) * MEM_HD].astype(BF16)
        s = _dot_nt(q_ref[0, :, sl], k) * (MEM_HD ** -0.5)
        e = jnp.exp(s - jnp.max(s, axis=-1, keepdims=True))
        p = e / jnp.sum(e, axis=-1, keepdims=True)
        out.append(_dot(p.astype(BF16), v))
    o_ref[0] = jnp.concatenate(out, axis=1).astype(BF16)


def _mem_attn(mq, mkv, tq):
    b, t, w = mq.shape
    m = mkv.shape[1]
    return pl.pallas_call(
        _mem_attn_kernel,
        grid=(b, t // tq),
        in_specs=[pl.BlockSpec((1, tq, w), lambda i, j: (i, j, 0)),
                  pl.BlockSpec((1, m, 2 * w), lambda i, j: (i, 0, 0))],
        out_specs=pl.BlockSpec((1, tq, w), lambda i, j: (i, j, 0)),
        out_shape=jax.ShapeDtypeStruct((b, t, w), BF16),
        compiler_params=_cparams(("parallel", "parallel")),
    )(mq, mkv)


_SCORE_BATCH = 16
_ATTN_BATCH = 2


def _row_bias(dist, tab_ref):
    bucket = _t5_bucket(dist)
    acc = jnp.zeros((NSA_HEADS, dist.shape[1]), F32)
    for bk in range(REL_BUCKETS):
        acc = jnp.where(bucket == bk, tab_ref[:, bk:bk + 1], acc)
    return acc


def _sample_score_kernel(qz_ref, kc_ref, vct_ref, tab_ref, ov_ref, oc_ref, idx_ref, *, past, n_sel):
    sb = qz_ref.shape[0]
    ncp = kc_ref.shape[1]
    nsp = ov_ref.shape[1]
    cpos = lax.broadcasted_iota(I32, (1, ncp), 1) * CMP_STRIDE + (CMP_LEN - 1)
    bias = _row_bias(past - cpos, tab_ref)
    valid = jnp.broadcast_to(cpos <= past, (NSA_HEADS, ncp))
    hrow = lax.broadcasted_iota(I32, (NSA_HEADS, 1), 0)
    pcs = []
    for j in range(sb):
        pc = _masked_softmax(_dot_nt(qz_ref[j].astype(BF16), kc_ref[j]) + bias, valid)
        oc_ref[j] = _dot_nt(pc.astype(BF16), vct_ref[j])
        grp = [jnp.sum(jnp.where(hrow // NSA_GRP == g, pc, 0.0), axis=0, keepdims=True) for g in range(NSA_KV)]
        pcs.append(jnp.concatenate(grp + [jnp.zeros((NSA_HEADS - NSA_KV, ncp), F32)], axis=0))
    hi, lo = _split_bf16(jnp.concatenate(pcs, axis=0))
    ssum = _dot(hi, ov_ref[...]) + _dot(lo, ov_ref[...])
    blk_id = lax.broadcasted_iota(I32, ssum.shape, 1)
    cur = past // SEL_BLK
    forced = (blk_id == 0) | (blk_id == cur) | (blk_id == cur - 1)
    score = jnp.where(forced, FORCE_SCORE, jnp.where(blk_id <= cur, ssum, -1.0))
    score = jnp.where(blk_id < n_sel, score, -jnp.inf)
    idx = _topk_select(score, min(TOP_N, n_sel))
    idx_ref[...] = idx.reshape(sb, NSA_HEADS, LANES)


def _sample_score(qz, kc, vct, tab8, ov, past, n_sel):
    nb = qz.shape[0]
    ncp = kc.shape[1]
    sb = math.gcd(nb, _SCORE_BATCH)
    blk8 = pl.BlockSpec((sb, NSA_HEADS, LANES), lambda i: (i, 0, 0))
    blkc = pl.BlockSpec((sb, ncp, LANES), lambda i: (i, 0, 0))
    return pl.pallas_call(
        functools.partial(_sample_score_kernel, past=past, n_sel=n_sel),
        grid=(nb // sb,),
        in_specs=[blk8, blkc, pl.BlockSpec((sb, LANES, ncp), lambda i: (i, 0, 0)), _full(tab8.shape),
                  _full(ov.shape)],
        out_specs=[blk8, blk8],
        out_shape=[jax.ShapeDtypeStruct((nb, NSA_HEADS, LANES), F32)] * 2,
        compiler_params=_cparams(("parallel",)),
    )(qz, kc, vct, tab8, ov)


def _sample_attn_kernel(pt_ref, idx_ref, pool_ref, q8_ref, qz_ref, idxv_ref, ex_ref, oc_ref, gate_ref, selnew_ref,
                        winnew_ref, winnewt_ref, win_ref, mq_ref, mem_ref, tab_ref,
                        onsa_ref, omem_ref, wout_ref, buf, sem, *, past):
    b = pl.program_id(0)
    nb = pl.num_programs(0)
    sb = buf.shape[1]
    page_rows = pool_ref.shape[4]
    n_top = buf.shape[5] // page_rows
    cur = past // SEL_BLK
    blocks_pp = page_rows // SEL_BLK
    copies = [(u, g, n) for u in range(sb) for g in range(NSA_KV) for n in range(n_top)]

    def blk_copy(step, slot, u, g, n):
        seq = step * sb + u
        blk = jnp.minimum(idx_ref[seq, g * n_top + n], cur - 1)
        page = pt_ref[seq, blk // blocks_pp]
        return pltpu.make_async_copy(pool_ref.at[page, :, g],
                                     buf.at[slot, u, g, :, :, n * page_rows:(n + 1) * page_rows], sem.at[slot])

    def fetch(step, slot):
        for c in copies:
            blk_copy(step, slot, *c).start()

    slot = b % 2

    @pl.when(b == 0)
    def _():
        fetch(0, 0)

    @pl.when(b + 1 < nb)
    def _():
        fetch(b + 1, 1 - slot)

    for c in copies:
        blk_copy(b, slot, *c).wait()
    for u in range(sb):
        _sample_attn_one(u, b * sb + u, slot, past, n_top, page_rows, q8_ref, qz_ref, idxv_ref, ex_ref, oc_ref,
                         gate_ref, selnew_ref, winnew_ref, winnewt_ref, win_ref, mq_ref, mem_ref, tab_ref,
                         onsa_ref, omem_ref, wout_ref, buf)


def _sample_attn_one(u, seq, slot, past, n_top, page_rows, q8_ref, qz_ref, idxv_ref, ex_ref, oc_ref, gate_ref,
                     selnew_ref, winnew_ref, winnewt_ref, win_ref, mq_ref, mem_ref, tab_ref,
                     onsa_ref, omem_ref, wout_ref, buf):
    cur = past // SEL_BLK
    blocks_pp = page_rows // SEL_BLK
    hrow = lax.broadcasted_iota(I32, (NSA_HEADS, 1), 0)
    hgrp = hrow // NSA_GRP
    tab0 = tab_ref[:, 0:1]

    def half_of_group(x):
        return jnp.where(hgrp == 0, x[:, :NSA_HD], x[:, NSA_HD:])

    qz = qz_ref[u]
    xw = win_ref[u]
    wb_rows = xw.shape[1]
    wn = winnew_ref[u]
    j = lax.broadcasted_iota(I32, (1, wb_rows), 1)
    dw = wb_rows - j
    s = _dot(qz.astype(BF16), xw[:LANES].astype(BF16)) + _row_bias(dw, tab_ref)
    s_new = jnp.sum(qz * wn[:, :LANES], axis=-1, keepdims=True) + tab0
    mask = jnp.broadcast_to(dw < WINDOW, s.shape)
    zm = jnp.where(mask, s, NEG)
    m = jnp.maximum(jnp.max(zm, axis=-1, keepdims=True), s_new)
    e = jnp.exp(zm - m) * mask.astype(F32)
    e_new = jnp.exp(s_new - m)
    ow = (_dot_nt(e.astype(BF16), xw[LANES:].astype(BF16)) + e_new * wn[:, LANES:]) / jnp.maximum(
        jnp.sum(e, axis=-1, keepdims=True) + e_new, 1e-30)
    ow = half_of_group(ow)
    seq_lane = lax.broadcasted_iota(I32, winnewt_ref.shape, 1)
    new_col = jnp.sum(jnp.where(seq_lane == seq, winnewt_ref[...], 0.0), axis=-1, keepdims=True)
    out_lane = lax.broadcasted_iota(I32, xw.shape, 1)
    wout_ref[u] = jnp.where(out_lane == wb_rows - 1, new_col, pltpu.roll(xw, wb_rows - 1, axis=1))

    mq = mq_ref[u].astype(BF16)
    n_mem = mem_ref.shape[1] // (2 * MEM_HEADS)
    omem = jnp.zeros((NSA_HEADS, MEM_HD), F32)
    for h in range(MEM_HEADS):
        kh = mem_ref[u, pl.ds(h, n_mem, stride=2 * MEM_HEADS), :].astype(BF16)
        vh = mem_ref[u, pl.ds(MEM_HEADS + h, n_mem, stride=2 * MEM_HEADS), :].astype(BF16)
        sm = _dot_nt(mq, kh) * (MEM_HD ** -0.5)
        em = jnp.exp(sm - jnp.max(sm, axis=-1, keepdims=True))
        pm = em / jnp.sum(em, axis=-1, keepdims=True)
        omem = jnp.where(hrow == h, _dot(pm.astype(BF16), vh), omem)
    omem_ref[u] = omem

    q8 = q8_ref[u]
    q8b = q8.astype(BF16)
    sn = selnew_ref[u]
    ncols = n_top * page_rows
    col = lax.broadcasted_iota(I32, (1, ncols), 1)
    idxe = _dot(idxv_ref[u].astype(BF16), ex_ref[...])
    osel = jnp.zeros((NSA_HEADS, NSA_HD), F32)
    for g in range(NSA_KV):
        kt = buf[slot, u, g, 0].astype(BF16)
        vt = buf[slot, u, g, 1].astype(BF16)
        blk = idxe[g:g + 1, :].astype(I32)
        in_page = col % page_rows
        spos = blk * SEL_BLK + in_page % SEL_BLK
        mask = (blk < cur) & (in_page // SEL_BLK == blk % blocks_pp)
        sg = _dot(q8b, kt) + _row_bias(past - spos, tab_ref)
        s_new = jnp.sum(q8 * sn[:, g * NSA_HD:(g + 1) * NSA_HD], axis=-1, keepdims=True) + tab0
        new_sel = jnp.max(jnp.where(blk == cur, 1.0, 0.0), axis=-1, keepdims=True) > 0.5
        maskb = jnp.broadcast_to(mask, sg.shape)
        zm = jnp.where(maskb, sg, NEG)
        z_new = jnp.where(new_sel, s_new, NEG)
        m = jnp.maximum(jnp.max(zm, axis=-1, keepdims=True), z_new)
        e = jnp.exp(zm - m) * maskb.astype(F32)
        e_new = jnp.exp(z_new - m) * new_sel.astype(F32)
        v_new = sn[:, (NSA_KV + g) * NSA_HD:(NSA_KV + g + 1) * NSA_HD]
        og = (_dot_nt(e.astype(BF16), vt) + e_new * v_new) / jnp.maximum(
            jnp.sum(e, axis=-1, keepdims=True) + e_new, 1e-30)
        osel = jnp.where(hgrp == g, og, osel)

    gates = gate_ref[u]
    onsa_ref[u] = gates[:, 0:1] * half_of_group(oc_ref[u]) + gates[:, 1:2] * osel + gates[:, 2:3] * ow


def _sample_attn(table, idx2d, pool, q8, qz, idxv, ex, oc, gate8, selnew, winnew, winnewt, win, mq8, mem, tab8, past,
                 n_top):
    nb = q8.shape[0]
    sb = math.gcd(nb, _ATTN_BATCH)
    page_rows = pool.shape[4]
    blk3 = lambda a: pl.BlockSpec((sb,) + a.shape[1:], lambda i, pt, ix: (i, 0, 0))
    whole = lambda a: pl.BlockSpec(a.shape, lambda i, pt, ix: (0, 0))
    gs = pltpu.PrefetchScalarGridSpec(
        num_scalar_prefetch=2,
        grid=(nb // sb,),
        in_specs=[pl.BlockSpec(memory_space=pl.ANY), blk3(q8), blk3(qz), blk3(idxv), whole(ex), blk3(oc), blk3(gate8),
                  blk3(selnew), blk3(winnew), whole(winnewt), blk3(win), blk3(mq8), blk3(mem), whole(tab8)],
        out_specs=[pl.BlockSpec((sb, NSA_HEADS, NSA_HD), lambda i, pt, ix: (i, 0, 0)),
                   pl.BlockSpec((sb, NSA_HEADS, MEM_HD), lambda i, pt, ix: (i, 0, 0)),
                   pl.BlockSpec((sb,) + win.shape[1:], lambda i, pt, ix: (i, 0, 0))],
        scratch_shapes=[pltpu.VMEM((2, sb, NSA_KV, 2, NSA_HD, n_top * page_rows), F32),
                        pltpu.SemaphoreType.DMA((2,))],
    )
    return pl.pallas_call(
        functools.partial(_sample_attn_kernel, past=past),
        grid_spec=gs,
        out_shape=[jax.ShapeDtypeStruct((nb, NSA_HEADS, NSA_HD), F32),
                   jax.ShapeDtypeStruct((nb, NSA_HEADS, MEM_HD), F32),
                   jax.ShapeDtypeStruct(win.shape, F32)],
        compiler_params=_cparams(("arbitrary",)),
    )(table, idx2d, pool, q8, qz, idxv, ex, oc, gate8, selnew, winnew, winnewt, win, mq8, mem, tab8)


def _merge_ffn_kernel(x_ref, ol_ref, on_ref, om_ref, mg_ref, wa_ref, wb_ref, wm_ref, wo_ref, gf_ref, wg_ref,
                      wu_ref, wd_ref, y_ref):
    d = x_ref.shape[1]
    mg = mg_ref[...]
    z = (mg[:, 0:d] * _dot(ol_ref[...], wa_ref[...]) + mg[:, d:2 * d] * _dot(on_ref[...], wb_ref[...])
         + mg[:, 2 * d:3 * d] * _dot(om_ref[...], wm_ref[...]))
    h = x_ref[...] + _dot(z.astype(BF16), wo_ref[...])
    f = (h * lax.rsqrt(jnp.mean(h * h, axis=-1, keepdims=True) + EPS) * gf_ref[...]).astype(BF16)
    a = jax.nn.silu(_dot(f, wg_ref[...])) * _dot(f, wu_ref[...])
    y_ref[...] = h + _dot(a.astype(BF16), wd_ref[...])


def _merge_ffn(x2d, ol, on, om, mg, prm, tm):
    n, d = x2d.shape
    row = lambda a: pl.BlockSpec((tm, a.shape[1]), lambda i: (i, 0))
    consts = [prm['w_up_a'], prm['w_up_b'], prm['w_up_m'], prm['w_o'], prm['norm_ffn'], prm['w_ffn_gate'],
              prm['w_ffn_up'], prm['w_ffn_down']]
    acts = [x2d, ol, on, om, mg]
    return pl.pallas_call(
        _merge_ffn_kernel,
        grid=(n // tm,),
        in_specs=[row(a) for a in acts] + [pl.BlockSpec(c.shape, lambda i: (0, 0), pipeline_mode=pl.Buffered(1))
                                           for c in consts],
        out_specs=pl.BlockSpec((tm, d), lambda i: (i, 0)),
        out_shape=jax.ShapeDtypeStruct((n, d), F32),
        compiler_params=_cparams(("parallel",)),
    )(*acts, *consts)


def _round_up(n, m):
    return -(-n // m) * m


def _overlap(ncp, n_cmp, nsp):
    cs = jnp.arange(ncp)[:, None] * CMP_STRIDE
    ss = jnp.arange(nsp)[None, :] * SEL_BLK
    hit = (cs < ss + SEL_BLK) & (cs + CMP_LEN > ss) & (jnp.arange(ncp)[:, None] < n_cmp)
    return hit.astype(BF16)


def _prep_params(norm_mix, w_in, conv_w, conv_b, w_lru_a, b_lru_a, w_lru_i, b_lru_i, lru_lambda, g_nsa_q, g_nsa_k,
                 pe_cmp_k, w_cmp_k1, w_cmp_k2, pe_cmp_v, w_cmp_v1, w_cmp_v2, norm_mem, w_mem_kv, g_mem_q, g_mem_k,
                 w_up_a, w_up_b, w_up_m, w_o, norm_ffn, w_ffn_gate, w_ffn_up, w_ffn_down):
    row = lambda v: v.reshape(1, -1).astype(F32)
    n_gg = N_BRANCH * NSA_GRP
    w = w_in
    zpad = jnp.zeros((w.shape[0], LANES - n_gg), w.dtype)
    ng0 = _OFF_NG
    w_packed = jnp.concatenate(
        [w[:, :ng0], w[:, ng0:ng0 + n_gg], zpad, w[:, ng0 + n_gg:ng0 + 2 * n_gg], zpad, w[:, ng0 + 2 * n_gg:]],
        axis=1).astype(BF16)
    eye = jnp.eye(NSA_KV, dtype=F32)

    def w1_big(w1):
        wr = w1.reshape(2, CMP_STRIDE, NSA_HD, CMP_HID)
        big = jnp.einsum('hrdj,gk->rgdhkj', wr, eye)
        return big.reshape(CMP_STRIDE * NSA_KV * NSA_HD, 2 * NSA_KV * CMP_HID).astype(BF16)

    def w2_bd(w2):
        return jnp.einsum('jd,gk->gjkd', w2, eye).reshape(NSA_KV * CMP_HID, NSA_KV * NSA_HD).astype(BF16)

    pe8 = lambda pe: jnp.broadcast_to(pe.reshape(1, -1), (8, pe.size)).astype(BF16)
    return dict(
        norm_mix=row(norm_mix), w_in=w_packed,
        bd_q=_block_diag_ones(NSA_HEADS * NSA_HD, NSA_HD), bd_k=_block_diag_ones(NSA_KV * NSA_HD, NSA_HD),
        bd_m=_block_diag_ones(MEM_HEADS * MEM_HD, MEM_HD),
        gq=row(jnp.tile(g_nsa_q, NSA_HEADS)), gk_cmp=row(jnp.tile(g_nsa_k[0], NSA_KV)),
        gk_sel=row(jnp.tile(g_nsa_k[1], NSA_KV)), gk_win=row(jnp.tile(g_nsa_k[2], NSA_KV)),
        gmq=row(jnp.tile(g_mem_q, MEM_HEADS)), gmk=row(jnp.tile(g_mem_k, MEM_HEADS)),
        conv_w=conv_w.astype(F32), conv_b=row(conv_b), w_lru_a=w_lru_a.astype(BF16), w_lru_i=w_lru_i.astype(BF16),
        b_lru_a=row(b_lru_a), b_lru_i=row(b_lru_i), lru_lambda=row(lru_lambda),
        w1k_big=w1_big(w_cmp_k1), w1v_big=w1_big(w_cmp_v1), w1k=w_cmp_k1.astype(BF16), w1v=w_cmp_v1.astype(BF16),
        pe_k=pe8(pe_cmp_k), pe_v=pe8(pe_cmp_v), w2k_bd=w2_bd(w_cmp_k2), w2v_bd=w2_bd(w_cmp_v2),
        norm_mem=row(norm_mem), w_mem_kv=w_mem_kv.astype(BF16),
        w_up_a=w_up_a.astype(BF16), w_up_b=w_up_b.astype(BF16), w_up_m=w_up_m.astype(BF16), w_o=w_o.astype(BF16),
        norm_ffn=row(norm_ffn), w_ffn_gate=w_ffn_gate.astype(BF16), w_ffn_up=w_ffn_up.astype(BF16),
        w_ffn_down=w_ffn_down.astype(BF16))


def _feature_major(cache):
    return jnp.transpose(cache, (0, 2, 3, 4, 1))


def kernel(x_prompt, x_sample, mem_prompt, cache_cmp_kv, cache_sel_kv, page_table, cache_win_kv, cache_mem_kv,
           state_lru_h, state_conv, rel_bias, norm_mix, w_in, conv_w, conv_b, w_lru_a, b_lru_a, w_lru_i, b_lru_i,
           lru_lambda, g_nsa_q, g_nsa_k, pe_cmp_k, w_cmp_k1, w_cmp_k2, pe_cmp_v, w_cmp_v1, w_cmp_v2, norm_mem,
           w_mem_kv, g_mem_q, g_mem_k, w_up_a, w_up_b, w_up_m, w_o, norm_ffn, w_ffn_gate, w_ffn_up, w_ffn_down):
    assert norm_mix.shape[0] == 1 and x_sample.shape[1] == 1
    weights = (norm_mix, w_in, conv_w, conv_b, w_lru_a, b_lru_a, w_lru_i, b_lru_i, lru_lambda, g_nsa_q, g_nsa_k,
               pe_cmp_k, w_cmp_k1, w_cmp_k2, pe_cmp_v, w_cmp_v1, w_cmp_v2, norm_mem, w_mem_kv, g_mem_q, g_mem_k,
               w_up_a, w_up_b, w_up_m, w_o, norm_ffn, w_ffn_gate, w_ffn_up, w_ffn_down)
    prm = _prep_params(*[w[0] for w in weights])
    bsz, t, d = x_prompt.shape
    db = x_sample.shape[0]
    n_pages = page_table.shape[1]
    page_rows = cache_cmp_kv.shape[2]
    past = n_pages * page_rows
    n_sel = -(-(past + 1) // SEL_BLK)
    assert t % _FAR_TK == 0 and page_rows % LANES == 0 and n_sel <= 256

    xp2 = x_prompt.reshape(bsz * t, d)
    (lx, gg, q, cmp_p, selk, wink, mq, mg, cmpt_p, selt_p, wint_p, selvt, winvt, ngt) = _project(xp2, prm, 256, bsz)
    o_lru, h_p, cv_p = _lru_prompt(lx.reshape(bsz, t, d), gg.reshape(bsz, t, d), prm, 256)

    chunks_pp = page_rows // CMP_STRIDE
    pages_p = t // page_rows
    ncp_p = _round_up(t // CMP_STRIDE, LANES)
    pool_p = cmp_p.reshape(bsz * pages_p, chunks_pp, CHUNK_W)
    table_p = jnp.arange(bsz * pages_p, dtype=I32).reshape(bsz, pages_p)
    kc_p, vct_p = _compress(pool_p, table_p, prm, ncp_p, feature_major=False)

    nsp_p = _round_up(t // SEL_BLK, LANES)
    wb, cb = _bias_tiles(rel_bias.astype(F32), ncp_p)
    ovt_p = _overlap(ncp_p, t // CMP_STRIDE - 1, nsp_p).T
    ext_p = ((jnp.arange(t) // SEL_BLK)[:, None] == jnp.arange(nsp_p)[None, :]).astype(BF16)
    o_nsa = _nsa_prompt(q.reshape(bsz, t, -1), kc_p, vct_p, selk.reshape(bsz, t, LANES), selvt,
                        wink.reshape(bsz, t, LANES), winvt, ngt, wb, cb, ovt_p, ext_p)

    m_rows = mem_prompt.shape[1]
    mkv = _mem_kv(mem_prompt.reshape(bsz * m_rows, d), prm)
    o_mem = _mem_attn(mq.reshape(bsz, t, -1), mkv.reshape(bsz, m_rows, -1), 256)

    y_p = _merge_ffn(xp2, o_lru.reshape(bsz * t, d), o_nsa.reshape(bsz * t, -1), o_mem.reshape(bsz * t, -1), mg,
                     prm, 256)

    xs2 = x_sample.reshape(db, d)
    (lx_s, gg_s, q_s, _, _, _, mq_s, mg_s, cmpt_s, selt_s, wint_s, _, _, ngt_s) = _project(xs2, prm, db, 1)
    sel_s, win_s = selt_s[0].T, wint_s[0].T
    cv0 = jnp.transpose(state_conv[0], (1, 0, 2))
    o_lru_s, h_s, cv_s = _lru_sample(lx_s, gg_s, cv0, state_lru_h[0], prm)

    ncp_s = _round_up(past // CMP_STRIDE, LANES)
    pool_c = _feature_major(cache_cmp_kv[0]).reshape(-1, KV_W, page_rows)
    kc_s, vct_s = _compress(pool_c, page_table, prm, ncp_s, feature_major=True)

    nsp_s = _round_up(n_sel, LANES)
    n_top = min(TOP_N, n_sel)
    ov_s = _overlap(ncp_s, (past + 1) // CMP_STRIDE - 1, nsp_s)
    tab8 = jnp.zeros((NSA_HEADS, LANES), F32).at[:, :REL_BUCKETS].set(rel_bias.astype(F32).T)
    q8 = q_s.reshape(db, NSA_HEADS, NSA_HD)
    qz = jnp.concatenate([jnp.where(jnp.arange(NSA_HEADS)[None, :, None] // NSA_GRP == gi, q8, 0.0)
                          for gi in range(NSA_KV)], axis=2)
    oc_s, idx = _sample_score(qz, kc_s, vct_s, tab8, ov_s, past, n_sel)
    idx2d = idx[:, :NSA_KV, :n_top].astype(I32).reshape(db, NSA_KV * n_top)

    n_gg = N_BRANCH * NSA_GRP
    gate8 = jnp.concatenate([ngt_s[0, gi * LANES:gi * LANES + n_gg, :] for gi in range(NSA_KV)], axis=0).T
    gate8 = jnp.pad(gate8.reshape(db, NSA_HEADS, N_BRANCH), ((0, 0), (0, 0), (0, LANES - N_BRANCH)))
    mq8 = jnp.pad(mq_s.astype(F32).reshape(db, MEM_HEADS, MEM_HD), ((0, 0), (0, NSA_HEADS - MEM_HEADS), (0, 0)))
    ex_s = (jnp.arange(LANES)[:, None] == (jnp.arange(n_top * page_rows) // page_rows)[None, :]).astype(BF16)
    pool_s = _feature_major(cache_sel_kv[0])
    win_t = _feature_major(cache_win_kv[0])
    wb_rows = win_t.shape[-1]
    mem_s = cache_mem_kv[0]
    n_mem = mem_s.shape[1]
    o_nsa8, o_mem8, win_new = _sample_attn(
        page_table, idx2d, pool_s, q8, qz, idx, ex_s, oc_s, gate8, sel_s.reshape(db, 1, KV_W),
        win_s.reshape(db, 1, KV_W), win_s.T, win_t.reshape(db, KV_W, wb_rows), mq8,
        mem_s.reshape(db, n_mem * 2 * MEM_HEADS, MEM_HD), tab8, past, n_top)
    o_nsa_s = o_nsa8.reshape(db, NSA_HEADS * NSA_HD)
    o_mem_s = o_mem8[:, :MEM_HEADS].reshape(db, MEM_HEADS * MEM_HD)
    y_s = _merge_ffn(xs2, o_lru_s, o_nsa_s.astype(BF16), o_mem_s.astype(BF16), mg_s, prm, db)

    def rows_major(a):
        n, _, rows = a.shape
        return jnp.transpose(a.reshape(n, 2, NSA_KV, NSA_HD, rows), (0, 4, 1, 2, 3))[None]

    def new_rows(a):
        return jnp.transpose(a.reshape(2, NSA_KV, NSA_HD, db), (3, 0, 1, 2))[None, :, None]

    w_keep = min(WINDOW, t)
    return (y_p.reshape(bsz, t, d), y_s.reshape(db, 1, d),
            rows_major(cmpt_p), new_rows(cmpt_s), rows_major(selt_p), new_rows(selt_s),
            rows_major(wint_p[:, :, t - w_keep:]), rows_major(win_new),
            mkv.reshape(1, bsz, m_rows, 2, MEM_HEADS, MEM_HD),
            h_p.reshape(1, bsz, d), h_s.reshape(1, db, d),
            cv_p.reshape(1, bsz, CONV_W - 1, d), jnp.transpose(cv_s, (1, 0, 2))[None])
```

```python
import functools
import math

import jax
import jax.numpy as jnp
import numpy as np
from jax import lax
from jax.experimental import pallas as pl
from jax.experimental.pallas import tpu as pltpu

F32 = jnp.float32
BF16 = jnp.bfloat16
I32 = jnp.int32

EPS = 1e-6
NEG = -1e30
FORCE_SCORE = 1e6
LOG2E = math.log2(math.e)
LRU_C = 8.0
LRU_BLOCKS = 4
CONV_W = 4
NSA_HEADS = 8
NSA_KV = 2
NSA_GRP = NSA_HEADS // NSA_KV
NSA_HD = 64
CMP_STRIDE = 16
CMP_LEN = 2 * CMP_STRIDE
CMP_HID = 2 * NSA_HD
SEL_BLK = 64
TOP_N = 16
WINDOW = 512
Q_BLOCK = 128
MEM_HEADS = 4
MEM_HD = 128
REL_BUCKETS = 32
REL_MAX_DIST = 128
N_BRANCH = 3

LANES = 128
KV_W = 2 * NSA_KV * NSA_HD
CHUNK_W = CMP_STRIDE * KV_W
VMEM_LIMIT = 56 * 1024 * 1024


def _cparams(sem):
    return pltpu.CompilerParams(dimension_semantics=sem, vmem_limit_bytes=VMEM_LIMIT)


def _full(shape):
    n = len(shape)
    return pl.BlockSpec(shape, lambda *_: (0,) * n)


def _dot(a, b):
    return jnp.dot(a, b, preferred_element_type=F32)


def _dot_nt(a, b):
    return lax.dot_general(a, b, (((1,), (1,)), ((), ())), preferred_element_type=F32)


def _group_rms(z, ones_bd, gain, width):
    ss = _dot((z * z).astype(BF16), ones_bd)
    return z * lax.rsqrt(ss * (1.0 / width) + EPS) * gain


def _masked_softmax(z, mask):
    zm = jnp.where(mask, z, NEG)
    e = jnp.exp(zm - jnp.max(zm, axis=-1, keepdims=True)) * mask.astype(F32)
    return e / jnp.maximum(jnp.sum(e, axis=-1, keepdims=True), 1e-30)


def _t5_thresholds():
    exact = REL_BUCKETS // 2
    n = np.arange(1, 2 * REL_MAX_DIST + 1, dtype=np.float32)
    scaled = np.log(n / np.float32(exact)) / np.float32(math.log(REL_MAX_DIST / exact)) * np.float32(REL_BUCKETS - exact)
    bucket = np.minimum(exact + scaled.astype(np.int32), REL_BUCKETS - 1)
    assert np.all(np.diff(bucket[exact - 1:]) >= 0) and bucket[-1] == REL_BUCKETS - 1
    return [int(np.argmax(bucket >= b)) + 1 for b in range(exact + 1, REL_BUCKETS)]


_T5_THRESHOLDS = _t5_thresholds()


def _t5_bucket(dist):
    n = jnp.maximum(dist, 0)
    exact = REL_BUCKETS // 2
    large = exact + sum((n >= t).astype(I32) for t in _T5_THRESHOLDS)
    return jnp.where(n < exact, n, large)


def _block_diag_ones(n, width):
    i = jnp.arange(n) // width
    return (i[:, None] == i[None, :]).astype(BF16)


def _split_bf16(x):
    hi = x.astype(BF16)
    return hi, (x - hi.astype(F32)).astype(BF16)


_D = 1024
_NG_W = NSA_KV * LANES
_OFF_LX, _OFF_LG, _OFF_Q, _OFF_KV, _OFF_NG, _OFF_MQ, _OFF_MG, _OFF_END = (
    0, 1024, 2048, 2560, 3328, 3584, 4096, 7168)


def _proj_kernel(x_ref, g_ref, w_ref, bdq_ref, bdk_ref, bdm_ref, gq_ref, gks_ref, gkw_ref, gmq_ref,
                 lx_ref, lg_ref, q_ref, cmp_ref, selk_ref, wink_ref, mq_ref, mg_ref,
                 cmpt_ref, selt_ref, wint_ref, selvt_ref, winvt_ref, ngt_ref):
    x = x_ref[...]
    xn = (x * lax.rsqrt(jnp.mean(x * x, axis=-1, keepdims=True) + EPS) * g_ref[...]).astype(BF16)

    def seg(a, b):
        return _dot(xn, w_ref[:, a:b])

    lx_ref[...] = seg(_OFF_LX, _OFF_LG)
    lg_ref[...] = jax.nn.gelu(seg(_OFF_LG, _OFF_Q))
    zq = seg(_OFF_Q, _OFF_KV)
    q_ref[...] = _group_rms(zq, bdq_ref[...], gq_ref[...], NSA_HD) * (NSA_HD ** -0.5)
    zc = seg(_OFF_KV, _OFF_KV + KV_W)
    cmp_ref[...] = zc
    cmpt_ref[0] = zc.T
    for off, gain_ref, t_ref, k_ref, vt_ref in ((_OFF_KV + KV_W, gks_ref, selt_ref, selk_ref, selvt_ref),
                                                (_OFF_KV + 2 * KV_W, gkw_ref, wint_ref, wink_ref, winvt_ref)):
        z = seg(off, off + KV_W)
        kn = _group_rms(z[:, :LANES], bdk_ref[...], gain_ref[...], NSA_HD)
        vt = z[:, LANES:].T
        t_ref[0, :LANES, :] = kn.T
        t_ref[0, LANES:, :] = vt
        k_ref[...] = kn.astype(BF16)
        vt_ref[0] = vt.astype(BF16)
    ngt_ref[0] = jax.nn.sigmoid(seg(_OFF_NG, _OFF_MQ)).T
    zm = seg(_OFF_MQ, _OFF_MG)
    mq_ref[...] = _group_rms(zm, bdm_ref[...], gmq_ref[...], MEM_HD).astype(BF16)
    mg_ref[...] = jax.nn.sigmoid(seg(_OFF_MG, _OFF_END))


def _project(x2d, prm, tm, bsz):
    n = x2d.shape[0]
    t = n // bsz
    per_b = t // tm
    row = lambda w: pl.BlockSpec((tm, w), lambda i: (i, 0))
    fmaj = lambda w: pl.BlockSpec((1, w, tm), lambda i: (i // per_b, 0, i % per_b))
    consts = [prm['norm_mix'], prm['w_in'], prm['bd_q'], prm['bd_k'], prm['bd_m'],
              prm['gq'], prm['gk_sel'], prm['gk_win'], prm['gmq']]
    widths = [(_D, F32), (_D, F32), (512, F32), (KV_W, F32), (LANES, BF16), (LANES, BF16), (512, BF16), (3 * _D, F32)]
    fwidths = [(KV_W, F32), (KV_W, F32), (KV_W, F32), (LANES, BF16), (LANES, BF16), (_NG_W, F32)]
    return pl.pallas_call(
        _proj_kernel,
        grid=(n // tm,),
        in_specs=[row(_D)] + [_full(c.shape) for c in consts],
        out_specs=[row(w) for w, _ in widths] + [fmaj(w) for w, _ in fwidths],
        out_shape=[jax.ShapeDtypeStruct((n, w), dt) for w, dt in widths]
                  + [jax.ShapeDtypeStruct((bsz, w, t), dt) for w, dt in fwidths],
        compiler_params=_cparams(("parallel",)),
    )(x2d, *consts)


def _lru_gates(xc, wa_ref, wi_ref, ba_ref, bi_ref, lam_ref):
    bw = xc.shape[1] // LRU_BLOCKS
    ra, ri = [], []
    for n in range(LRU_BLOCKS):
        xr = xc[:, n * bw:(n + 1) * bw].astype(BF16)
        ra.append(_dot(xr, wa_ref[n]))
        ri.append(_dot(xr, wi_ref[n]))
    r = jax.nn.sigmoid(jnp.concatenate(ra, axis=1) + ba_ref[...])
    i = jax.nn.sigmoid(jnp.concatenate(ri, axis=1) + bi_ref[...])
    lam = -lam_ref[...]
    softplus = jnp.maximum(lam, 0.0) + jnp.log1p(jnp.exp(-jnp.abs(lam)))
    log_a = -LRU_C * r * softplus
    a = jnp.exp(log_a)
    b = jnp.sqrt(jnp.tanh(-log_a) * (a * a + 1.0)) * (i * xc)
    return a, b


def _lru_scan(a, b):
    tt = a.shape[0]
    row = lax.broadcasted_iota(I32, a.shape, 0)
    k = 1
    while k < tt:
        keep = row >= k
        b = b + a * jnp.where(keep, pltpu.roll(b, k, axis=0), 0.0)
        a = a * jnp.where(keep, pltpu.roll(a, k, axis=0), 1.0)
        k *= 2
    return a, b


def _lru_prompt_kernel(x_ref, gg_ref, cw_ref, cb_ref, wa_ref, wi_ref, ba_ref, bi_ref, lam_ref,
                       o_ref, h_ref, cv_ref, xbuf, hcar):
    t = pl.program_id(1)
    tt = x_ref.shape[1]

    @pl.when(t == 0)
    def _():
        xbuf[0:8, :] = jnp.zeros((8, xbuf.shape[1]), F32)
        hcar[...] = jnp.zeros(hcar.shape, F32)

    x = x_ref[0]
    xbuf[8:8 + tt, :] = x
    xc = cb_ref[...] + xbuf[5:5 + tt, :] * cw_ref[0:1, :]
    for k in range(1, CONV_W):
        xc = xc + xbuf[5 + k:5 + k + tt, :] * cw_ref[k:k + 1, :]
    a, b = _lru_gates(xc, wa_ref, wi_ref, ba_ref, bi_ref, lam_ref)
    ap, hs = _lru_scan(a, b)
    h = hs + ap * hcar[...]
    o_ref[0] = (h * gg_ref[0]).astype(BF16)
    hcar[...] = h[tt - 1:tt, :]
    xbuf[0:8, :] = x[tt - 8:tt, :]

    @pl.when(t == pl.num_programs(1) - 1)
    def _():
        h_ref[0] = h[tt - 1:tt, :]
        cv_ref[0] = x[tt - (CONV_W - 1):tt, :]


def _lru_prompt(lx, gg, prm, tt):
    b, t, w = lx.shape
    blk = pl.BlockSpec((1, tt, w), lambda i, j: (i, j, 0))
    consts = [prm['conv_w'], prm['conv_b'], prm['w_lru_a'], prm['w_lru_i'], prm['b_lru_a'], prm['b_lru_i'],
              prm['lru_lambda']]
    return pl.pallas_call(
        _lru_prompt_kernel,
        grid=(b, t // tt),
        in_specs=[blk, blk] + [_full(c.shape) for c in consts],
        out_specs=[blk, pl.BlockSpec((1, 1, w), lambda i, j: (i, 0, 0)),
                   pl.BlockSpec((1, CONV_W - 1, w), lambda i, j: (i, 0, 0))],
        out_shape=[jax.ShapeDtypeStruct((b, t, w), BF16), jax.ShapeDtypeStruct((b, 1, w), F32),
                   jax.ShapeDtypeStruct((b, CONV_W - 1, w), F32)],
        scratch_shapes=[pltpu.VMEM((tt + 8, w), F32), pltpu.VMEM((1, w), F32)],
        compiler_params=_cparams(("parallel", "arbitrary")),
    )(lx, gg, *consts)


def _lru_sample_kernel(x_ref, gg_ref, cv0_ref, h0_ref, cw_ref, cb_ref, wa_ref, wi_ref, ba_ref, bi_ref, lam_ref,
                       o_ref, h_ref, cv_ref):
    x = x_ref[...]
    xc = cb_ref[...] + x * cw_ref[CONV_W - 1:CONV_W, :]
    for k in range(CONV_W - 1):
        xc = xc + cv0_ref[k] * cw_ref[k:k + 1, :]
    a, b = _lru_gates(xc, wa_ref, wi_ref, ba_ref, bi_ref, lam_ref)
    h = a * h0_ref[...] + b
    o_ref[...] = (h * gg_ref[...]).astype(BF16)
    h_ref[...] = h
    for k in range(CONV_W - 2):
        cv_ref[k] = cv0_ref[k + 1]
    cv_ref[CONV_W - 2] = x


def _lru_sample(lx, gg, cv0, h0, prm):
    n, w = lx.shape
    consts = [prm['conv_w'], prm['conv_b'], prm['w_lru_a'], prm['w_lru_i'], prm['b_lru_a'], prm['b_lru_i'],
              prm['lru_lambda']]
    args = [lx, gg, cv0, h0] + consts
    return pl.pallas_call(
        _lru_sample_kernel,
        grid=(1,),
        in_specs=[_full(a.shape) for a in args],
        out_specs=[_full((n, w)), _full((n, w)), _full((CONV_W - 1, n, w))],
        out_shape=[jax.ShapeDtypeStruct((n, w), BF16), jax.ShapeDtypeStruct((n, w), F32),
                   jax.ShapeDtypeStruct((CONV_W - 1, n, w), F32)],
        compiler_params=_cparams(("arbitrary",)),
    )(*args)


_PAGE_SECTIONS = 4


def _compress_kernel(pt_ref, pool_ref, w1k_ref, w1v_ref, w1kp_ref, w1vp_ref, pek_ref, pev_ref, w2k_ref, w2v_ref,
                     bdk_ref, gk_ref, kc_ref, vct_ref, buf, sem, *scratch, n_valid, feature_major):
    b = pl.program_id(0)
    nb = pl.num_programs(0)
    n_pages = pt_ref.shape[1]
    ncp = kc_ref.shape[1]

    def page_copy(step, slot, p):
        return pltpu.make_async_copy(pool_ref.at[pt_ref[step, p]], buf.at[slot, p], sem.at[slot])

    def fetch(step, slot):
        def body(p, c):
            page_copy(step, slot, p).start()
            return c
        lax.fori_loop(0, n_pages, body, 0)

    slot = b % 2

    @pl.when(b == 0)
    def _():
        fetch(0, 0)

    @pl.when(b + 1 < nb)
    def _():
        fetch(b + 1, 1 - slot)

    def wait_body(p, c):
        page_copy(b, slot, p).wait()
        return c

    lax.fori_loop(0, n_pages, wait_body, 0)

    n_sec = math.gcd(n_pages, _PAGE_SECTIONS)
    sec_pages = n_pages // n_sec
    if feature_major:
        zbuf, = scratch
        page_rows = buf.shape[3]
        n_ch = n_pages * page_rows // CMP_STRIDE
        sec_ch = n_ch // n_sec

        def chunk_rows(kv, sec):
            for p in range(sec * sec_pages, (sec + 1) * sec_pages):
                zbuf[kv, p * page_rows:(p + 1) * page_rows, :] = buf[slot, p, kv * LANES:(kv + 1) * LANES, :].T
            return jnp.concatenate(
                [zbuf[kv, pl.ds(sec * sec_ch * CMP_STRIDE + r, sec_ch, stride=CMP_STRIDE), :].astype(BF16)
                 for r in range(CMP_STRIDE)], axis=1)
    else:
        rows_pp = buf.shape[2]
        n_ch = n_pages * rows_pp
        sec_ch = n_ch // n_sec

        def chunk_rows(kv, sec):
            pages = slice(sec * sec_pages, (sec + 1) * sec_pages)
            return jnp.concatenate(
                [buf[slot, pages, :, r * KV_W + kv * LANES:r * KV_W + (kv + 1) * LANES]
                 .reshape(sec_ch, LANES).astype(BF16) for r in range(CMP_STRIDE)], axis=1)

    row = lax.broadcasted_iota(I32, (n_ch, 1), 0)
    keep = row < n_valid
    outs = []
    for kv, w1_ref, w1p_ref, pe_ref, w2_ref in ((0, w1k_ref, w1kp_ref, pek_ref, w2k_ref),
                                                 (1, w1v_ref, w1vp_ref, pev_ref, w2v_ref)):
        hh = jnp.concatenate([_dot(chunk_rows(kv, sec), w1_ref[...]) for sec in range(n_sec)], axis=0)
        pos = _dot(pe_ref[...], w1p_ref[...])[0:1, :]
        pos = jnp.concatenate([pos, pos], axis=1)
        nh = NSA_KV * CMP_HID
        h = hh[:, :nh] + pltpu.roll(hh[:, nh:], n_ch - 1, axis=0) + pos
        outs.append(_dot(jax.nn.gelu(h).astype(BF16), w2_ref[...]))
    kc = _group_rms(outs[0], bdk_ref[...], gk_ref[...], NSA_HD)
    kc = jnp.where(keep, kc, 0.0)
    vc = jnp.where(keep, outs[1], 0.0)
    if ncp > n_ch:
        pad = jnp.zeros((ncp - n_ch, kc.shape[1]), F32)
        kc = jnp.concatenate([kc, pad], axis=0)
        vc = jnp.concatenate([vc, pad], axis=0)
    kc_ref[0] = kc.astype(BF16)
    vct_ref[0] = vc.T.astype(BF16)


def _compress(pool, table, prm, ncp, feature_major):
    nb, n_pages = table.shape
    if feature_major:
        page_rows = pool.shape[2]
        n_ch = n_pages * page_rows // CMP_STRIDE
        scratch = [pltpu.VMEM((2, n_pages * page_rows, LANES), F32)]
    else:
        n_ch = n_pages * pool.shape[1]
        scratch = []
    consts = [prm['w1k_big'], prm['w1v_big'], prm['w1k'], prm['w1v'], prm['pe_k'], prm['pe_v'],
              prm['w2k_bd'], prm['w2v_bd'], prm['bd_k'], prm['gk_cmp']]
    out_blk = pl.BlockSpec((1, ncp, LANES), lambda i, pt: (i, 0, 0))
    gs = pltpu.PrefetchScalarGridSpec(
        num_scalar_prefetch=1,
        grid=(nb,),
        in_specs=[pl.BlockSpec(memory_space=pl.ANY)] + [pl.BlockSpec(c.shape, lambda i, pt, _n=len(c.shape): (0,) * _n)
                                                         for c in consts],
        out_specs=[out_blk, pl.BlockSpec((1, LANES, ncp), lambda i, pt: (i, 0, 0))],
        scratch_shapes=[pltpu.VMEM((2, n_pages) + pool.shape[1:], F32), pltpu.SemaphoreType.DMA((2,))] + scratch,
    )
    return pl.pallas_call(
        functools.partial(_compress_kernel, n_valid=n_ch - 1, feature_major=feature_major),
        grid_spec=gs,
        out_shape=[jax.ShapeDtypeStruct((nb, ncp, LANES), BF16), jax.ShapeDtypeStruct((nb, LANES, ncp), BF16)],
        compiler_params=_cparams(("arbitrary",)),
    )(table, pool, *consts)


_WIN_TILES = WINDOW // Q_BLOCK + 1
_QCOLS = NSA_HEADS * Q_BLOCK


def _bias_tiles_kernel(tab_ref, wb_ref, cb_ref):
    g = pl.program_id(0)
    wrows = wb_ref.shape[0]
    ncp = cb_ref.shape[0] // 2
    c_w = lax.broadcasted_iota(I32, (wrows, Q_BLOCK), 0)
    i_w = lax.broadcasted_iota(I32, (wrows, Q_BLOCK), 1)
    d_w = (_WIN_TILES - 1) * Q_BLOCK + i_w - c_w
    k_c = lax.broadcasted_iota(I32, (2 * ncp, Q_BLOCK), 0)
    i_c = lax.broadcasted_iota(I32, (2 * ncp, Q_BLOCK), 1)
    d_c = i_c - (CMP_LEN - 1) - CMP_STRIDE * (k_c - ncp)
    for d, ok, ref in ((d_w, (d_w >= 0) & (d_w < WINDOW), wb_ref), (d_c, d_c >= 0, cb_ref)):
        bucket = _t5_bucket(d)
        for r in range(NSA_GRP):
            h = g * NSA_GRP + r
            acc = jnp.zeros(d.shape, F32)
            for bk in range(REL_BUCKETS):
                acc = jnp.where(bucket == bk, tab_ref[bk, h], acc)
            far = tab_ref[REL_BUCKETS - 1, h]
            ref[:, r * Q_BLOCK:(r + 1) * Q_BLOCK] = jnp.where(ok, (acc - far) * LOG2E, NEG)


def _bias_tiles(rel_bias, ncp):
    wrows = _WIN_TILES * Q_BLOCK
    gcols = NSA_GRP * Q_BLOCK
    return pl.pallas_call(
        _bias_tiles_kernel,
        grid=(NSA_KV,),
        in_specs=[pl.BlockSpec(memory_space=pltpu.SMEM)],
        out_specs=[pl.BlockSpec((wrows, gcols), lambda g: (0, g)),
                   pl.BlockSpec((2 * ncp, gcols), lambda g: (0, g))],
        out_shape=[jax.ShapeDtypeStruct((wrows, _QCOLS), F32),
                   jax.ShapeDtypeStruct((2 * ncp, _QCOLS), F32)],
        compiler_params=_cparams(("arbitrary",)),
    )(rel_bias)


def _topk_select(score, n_top):
    lane = lax.broadcasted_iota(I32, score.shape, 1).astype(F32)
    big = float(score.shape[1])
    slot = lax.broadcasted_iota(I32, (score.shape[0], LANES), 1)
    idx = jnp.zeros((score.shape[0], LANES), F32)
    for it in range(n_top):
        m = jnp.max(score, axis=-1, keepdims=True)
        first = jnp.min(jnp.where(score == m, lane, big), axis=-1, keepdims=True)
        score = jnp.where(lane == first, -jnp.inf, score)
        idx = jnp.where(slot == it, first, idx)
    return idx


def _topk_mask_cols(score, n_top):
    sid = lax.broadcasted_iota(I32, score.shape, 0).astype(F32)
    big = float(score.shape[0])
    sel = jnp.zeros(score.shape, F32)
    for _ in range(n_top):
        m = jnp.max(score, axis=0, keepdims=True)
        first = jnp.min(jnp.where(score == m, sid, big), axis=0, keepdims=True)
        hit = sid == first
        sel = jnp.where(hit, 1.0, sel)
        score = jnp.where(hit, -jnp.inf, score)
    return sel


_FAR_TK = 1024
_FAR_SPLIT = 4
_ONES_ROWS = 16


def _nsa_prompt_kernel(q_ref, kc_ref, vct_ref, selk_ref, selvt_ref, wink_ref, winvt_ref, gate_ref, wb_ref, cb_ref,
                       ovt_ref, ext_ref, o_ref, *, n_sel):
    bi = pl.program_id(1)
    q0 = bi * Q_BLOCK
    ncp = kc_ref.shape[1]
    nsp = ovt_ref.shape[0]
    gcols = NSA_GRP * Q_BLOCK

    lane = lax.broadcasted_iota(I32, (Q_BLOCK, LANES), 1)
    qf = q_ref[0] * LOG2E
    qz = []
    for h in range(NSA_HEADS):
        g = h // NSA_GRP
        blk = qf[:, (h // 2) * LANES:(h // 2 + 1) * LANES]
        if h % 2 != g:
            blk = pltpu.roll(blk, NSA_HD, axis=1)
        qz.append(jnp.where((lane // NSA_HD) == g, blk, 0.0))
    qz = jnp.concatenate(qz, axis=0).astype(BF16)

    def col_max(z):
        while z.shape[0] > 64 and z.shape[0] % 16 == 0:
            half = z.shape[0] // 2
            z = jnp.maximum(z[:half], z[half:])
        return jnp.max(z, axis=0, keepdims=True)

    def col_sum(z):
        return jnp.sum(z, axis=0, keepdims=True)

    c0 = pl.multiple_of(ncp - bi * (Q_BLOCK // CMP_STRIDE), Q_BLOCK // CMP_STRIDE)
    z = _dot_nt(kc_ref[0], qz) + cb_ref[pl.ds(c0, ncp), :]
    m = col_max(z)
    e = jnp.exp2(z - m)
    pc = e * jnp.where(m > 0.5 * NEG, 1.0 / col_sum(e), 0.0)
    oc = _dot(vct_ref[0], pc.astype(BF16))

    pcs = []
    for g in range(NSA_KV):
        acc = pc[:, g * gcols:g * gcols + Q_BLOCK]
        for r in range(1, NSA_GRP):
            acc = acc + pc[:, g * gcols + r * Q_BLOCK:g * gcols + (r + 1) * Q_BLOCK]
        pcs.append(acc)
    hi, lo = _split_bf16(jnp.concatenate(pcs, axis=1))
    ssum = _dot(ovt_ref[...], hi) + _dot(ovt_ref[...], lo)
    s_id = lax.broadcasted_iota(I32, (nsp, NSA_KV * Q_BLOCK), 0)
    cur = (q0 + lax.broadcasted_iota(I32, (1, NSA_KV * Q_BLOCK), 1) % Q_BLOCK) // SEL_BLK
    forced = (s_id == 0) | (s_id == cur) | (s_id == cur - 1)
    score = jnp.where(forced, FORCE_SCORE, jnp.where(s_id <= cur, ssum, -1.0))
    selm = _topk_mask_cols(score, min(TOP_N, n_sel)).T

    def with_mask(sel):
        u = ((sel - 1.0) * (-NEG)).astype(BF16)
        rows = [u[g * Q_BLOCK:(g + 1) * Q_BLOCK] for g in range(NSA_KV) for _ in range(NSA_GRP)]
        return jnp.concatenate([qz, jnp.concatenate(rows, axis=0)], axis=1)

    blk_id = lax.broadcasted_iota(I32, (NSA_KV * Q_BLOCK, nsp), 1)
    qz_near = with_mask(selm)
    qz_far = with_mask(jnp.where(blk_id < 2 * bi - 2, selm, 0.0))

    p0 = pl.multiple_of(jnp.maximum(q0 - Q_BLOCK, 0), Q_BLOCK)
    d0 = pl.multiple_of(q0, Q_BLOCK)
    kn = jnp.concatenate(
        [jnp.concatenate([selk_ref[0, pl.ds(k0, Q_BLOCK), :], ext_ref[pl.ds(k0, Q_BLOCK), :]], axis=1)
         for k0 in (p0, d0)], axis=0)
    vn = jnp.concatenate([selvt_ref[0, :, pl.ds(p0, Q_BLOCK)], selvt_ref[0, :, pl.ds(d0, Q_BLOCK)]], axis=1)
    krow = lax.broadcasted_iota(I32, (2 * Q_BLOCK, 1), 0)
    wrows = wb_ref.shape[0]
    z = _dot_nt(kn, qz_near) + wb_ref[wrows - 2 * Q_BLOCK:wrows, :]
    z = z + jnp.where((krow < Q_BLOCK) & (bi == 0), NEG, 0.0)
    def with_ones(vt):
        return jnp.concatenate([vt, jnp.ones((_ONES_ROWS, vt.shape[1]), BF16)], axis=0)

    m = col_max(z)
    carry = (m, _dot(with_ones(vn), jnp.exp2((z - m).astype(BF16))))

    far_blocks = _FAR_TK // Q_BLOCK
    n_far = (bi + far_blocks - 2) // far_blocks

    sub = _FAR_TK // _FAR_SPLIT

    def far_body(t, carry):
        m, acc = carry
        starts = [pl.multiple_of(t * _FAR_TK + s * sub, sub) for s in range(_FAR_SPLIT)]
        zs = [_dot_nt(jnp.concatenate([selk_ref[0, pl.ds(k0, sub), :], ext_ref[pl.ds(k0, sub), :]], axis=1), qz_far)
              for k0 in starts]
        for k0, z in zip(starts, zs):
            m_new = jnp.maximum(m, col_max(z))
            e = jnp.exp2((z - m_new).astype(BF16))
            acc = jnp.exp2(m - m_new) * acc + _dot(with_ones(selvt_ref[0, :, pl.ds(k0, sub)]), e)
            m = m_new
        return m, acc

    _, acc = lax.fori_loop(0, n_far, far_body, carry)
    osel = acc[:LANES] / acc[LANES:LANES + 1]

    ks, vs = [], []
    for t in range(_WIN_TILES):
        st = pl.multiple_of(jnp.maximum(q0 - (_WIN_TILES - 1 - t) * Q_BLOCK, 0), Q_BLOCK)
        ks.append(wink_ref[0, pl.ds(st, Q_BLOCK), :])
        vs.append(winvt_ref[0, :, pl.ds(st, Q_BLOCK)])
    wrow = lax.broadcasted_iota(I32, (wrows, 1), 0)
    before_start = jnp.where(wrow < (_WIN_TILES - 1 - bi) * Q_BLOCK, NEG, 0.0)
    z = _dot_nt(jnp.concatenate(ks, axis=0), qz) + wb_ref[...] + before_start
    ow = _dot(with_ones(jnp.concatenate(vs, axis=1)), jnp.exp2((z - col_max(z)).astype(BF16)))
    ow = ow[:LANES] / ow[LANES:LANES + 1]

    gates = gate_ref[0]

    def gate_row(br):
        rows = [g * LANES + r * N_BRANCH + br for g in range(NSA_KV) for r in range(NSA_GRP)]
        return jnp.concatenate([gates[c:c + 1, :] for c in rows], axis=1)

    mixed = gate_row(0) * oc + gate_row(1) * osel + gate_row(2) * ow
    out = []
    for j in range(NSA_HEADS // 2):
        g = (2 * j) // NSA_GRP
        pair = jnp.concatenate([mixed[g * NSA_HD:(g + 1) * NSA_HD, (2 * j) * Q_BLOCK:(2 * j + 1) * Q_BLOCK],
                                mixed[g * NSA_HD:(g + 1) * NSA_HD, (2 * j + 1) * Q_BLOCK:(2 * j + 2) * Q_BLOCK]],
                               axis=0)
        out.append(pair.T)
    o_ref[0] = jnp.concatenate(out, axis=1).astype(BF16)


def _nsa_prompt(q, kc, vct, selk, selvt, wink, winvt, gates_t, wb, cb, ovt, ext):
    b, t, hw = q.shape
    ncp = kc.shape[1]
    per_b = lambda shape: pl.BlockSpec((1,) + shape, lambda i, j: (i, 0, 0))
    const = lambda a: pl.BlockSpec(a.shape, lambda i, j: (0, 0), pipeline_mode=pl.Buffered(1))
    return pl.pallas_call(
        functools.partial(_nsa_prompt_kernel, n_sel=t // SEL_BLK),
        grid=(b, t // Q_BLOCK),
        in_specs=[pl.BlockSpec((1, Q_BLOCK, hw), lambda i, j: (i, j, 0)),
                  per_b((ncp, LANES)), per_b((LANES, ncp)),
                  per_b((t, LANES)), per_b((LANES, t)), per_b((t, LANES)), per_b((LANES, t)),
                  pl.BlockSpec((1, _NG_W, Q_BLOCK), lambda i, j: (i, 0, j)),
                  const(wb), const(cb), const(ovt), const(ext)],
        out_specs=pl.BlockSpec((1, Q_BLOCK, hw), lambda i, j: (i, j, 0)),
        out_shape=jax.ShapeDtypeStruct((b, t, hw), BF16),
        compiler_params=_cparams(("parallel", "parallel")),
    )(q, kc, vct, selk, selvt, wink, winvt, gates_t, wb, cb, ovt, ext)


def _mem_kv_kernel(m_ref, g_ref, w_ref, bd_ref, gk_ref, o_ref):
    x = m_ref[...]
    xn = (x * lax.rsqrt(jnp.mean(x * x, axis=-1, keepdims=True) + EPS) * g_ref[...]).astype(BF16)
    z = _dot(xn, w_ref[...])
    half = MEM_HEADS * MEM_HD
    kn = _group_rms(z[:, :half], bd_ref[...], gk_ref[...], MEM_HD)
    o_ref[...] = jnp.concatenate([kn, z[:, half:]], axis=1)


def _mem_kv(mem2d, prm):
    n = mem2d.shape[0]
    args = [mem2d, prm['norm_mem'], prm['w_mem_kv'], prm['bd_m'], prm['gmk']]
    w = 2 * MEM_HEADS * MEM_HD
    return pl.pallas_call(
        _mem_kv_kernel,
        grid=(1,),
        in_specs=[_full(a.shape) for a in args],
        out_specs=_full((n, w)),
        out_shape=jax.ShapeDtypeStruct((n, w), F32),
        compiler_params=_cparams(("arbitrary",)),
    )(*args)


def _mem_attn_kernel(q_ref, kv_ref, o_ref):
    half = MEM_HEADS * MEM_HD
    out = []
    for h in range(MEM_HEADS):
        sl = slice(h * MEM_HD, (h + 1) * MEM_HD)
        k = kv_ref[0, :, sl].astype(BF16)
        v = kv_ref[0, :, half + h * MEM_HD:half + (h + 1) * MEM_HD].astype(BF16)
        s = _dot_nt(q_ref[0, :, sl], k) * (MEM_HD ** -0.5)
        e = jnp.exp(s - jnp.max(s, axis=-1, keepdims=True))
        p = e / jnp.sum(e, axis=-1, keepdims=True)
        out.append(_dot(p.astype(BF16), v))
    o_ref[0] = jnp.concatenate(out, axis=1).astype(BF16)


def _mem_attn(mq, mkv, tq):
    b, t, w = mq.shape
    m = mkv.shape[1]
    return pl.pallas_call(
        _mem_attn_kernel,
        grid=(b, t // tq),
        in_specs=[pl.BlockSpec((1, tq, w), lambda i, j: (i, j, 0)),
                  pl.BlockSpec((1, m, 2 * w), lambda i, j: (i, 0, 0))],
        out_specs=pl.BlockSpec((1, tq, w), lambda i, j: (i, j, 0)),
        out_shape=jax.ShapeDtypeStruct((b, t, w), BF16),
        compiler_params=_cparams(("parallel", "parallel")),
    )(mq, mkv)


_SCORE_BATCH = 16
_ATTN_BATCH = 2


def _row_bias(dist, tab_ref):
    bucket = _t5_bucket(dist)
    acc = jnp.zeros((NSA_HEADS, dist.shape[1]), F32)
    for bk in range(REL_BUCKETS):
        acc = jnp.where(bucket == bk, tab_ref[:, bk:bk + 1], acc)
    return acc


def _sample_score_kernel(qz_ref, kc_ref, vct_ref, tab_ref, ov_ref, oc_ref, idx_ref, *, past, n_sel):
    sb = qz_ref.shape[0]
    ncp = kc_ref.shape[1]
    nsp = ov_ref.shape[1]
    cpos = lax.broadcasted_iota(I32, (1, ncp), 1) * CMP_STRIDE + (CMP_LEN - 1)
    bias = _row_bias(past - cpos, tab_ref)
    valid = jnp.broadcast_to(cpos <= past, (NSA_HEADS, ncp))
    hrow = lax.broadcasted_iota(I32, (NSA_HEADS, 1), 0)
    pcs = []
    for j in range(sb):
        pc = _masked_softmax(_dot_nt(qz_ref[j].astype(BF16), kc_ref[j]) + bias, valid)
        oc_ref[j] = _dot_nt(pc.astype(BF16), vct_ref[j])
        grp = [jnp.sum(jnp.where(hrow // NSA_GRP == g, pc, 0.0), axis=0, keepdims=True) for g in range(NSA_KV)]
        pcs.append(jnp.concatenate(grp + [jnp.zeros((NSA_HEADS - NSA_KV, ncp), F32)], axis=0))
    hi, lo = _split_bf16(jnp.concatenate(pcs, axis=0))
    ssum = _dot(hi, ov_ref[...]) + _dot(lo, ov_ref[...])
    blk_id = lax.broadcasted_iota(I32, ssum.shape, 1)
    cur = past // SEL_BLK
    forced = (blk_id == 0) | (blk_id == cur) | (blk_id == cur - 1)
    score = jnp.where(forced, FORCE_SCORE, jnp.where(blk_id <= cur, ssum, -1.0))
    score = jnp.where(blk_id < n_sel, score, -jnp.inf)
    idx = _topk_select(score, min(TOP_N, n_sel))
    idx_ref[...] = idx.reshape(sb, NSA_HEADS, LANES)


def _sample_score(qz, kc, vct, tab8, ov, past, n_sel):
    nb = qz.shape[0]
    ncp = kc.shape[1]
    sb = math.gcd(nb, _SCORE_BATCH)
    blk8 = pl.BlockSpec((sb, NSA_HEADS, LANES), lambda i: (i, 0, 0))
    blkc = pl.BlockSpec((sb, ncp, LANES), lambda i: (i, 0, 0))
    return pl.pallas_call(
        functools.partial(_sample_score_kernel, past=past, n_sel=n_sel),
        grid=(nb // sb,),
        in_specs=[blk8, blkc, pl.BlockSpec((sb, LANES, ncp), lambda i: (i, 0, 0)), _full(tab8.shape),
                  _full(ov.shape)],
        out_specs=[blk8, blk8],
        out_shape=[jax.ShapeDtypeStruct((nb, NSA_HEADS, LANES), F32)] * 2,
        compiler_params=_cparams(("parallel",)),
    )(qz, kc, vct, tab8, ov)


def _sample_attn_kernel(pt_ref, idx_ref, pool_ref, q8_ref, qz_ref, idxv_ref, ex_ref, oc_ref, gate_ref, selnew_ref,
                        winnew_ref, winnewt_ref, win_ref, mq_ref, mem_ref, tab_ref,
                        onsa_ref, omem_ref, wout_ref, buf, sem, *, past):
    b = pl.program_id(0)
    nb = pl.num_programs(0)
    sb = buf.shape[1]
    page_rows = pool_ref.shape[4]
    n_top = buf.shape[5] // page_rows
    cur = past // SEL_BLK
    blocks_pp = page_rows // SEL_BLK
    copies = [(u, g, n) for u in range(sb) for g in range(NSA_KV) for n in range(n_top)]

    def blk_copy(step, slot, u, g, n):
        seq = step * sb + u
        blk = jnp.minimum(idx_ref[seq, g * n_top + n], cur - 1)
        page = pt_ref[seq, blk // blocks_pp]
        return pltpu.make_async_copy(pool_ref.at[page, :, g],
                                     buf.at[slot, u, g, :, :, n * page_rows:(n + 1) * page_rows], sem.at[slot])

    def fetch(step, slot):
        for c in copies:
            blk_copy(step, slot, *c).start()

    slot = b % 2

    @pl.when(b == 0)
    def _():
        fetch(0, 0)

    @pl.when(b + 1 < nb)
    def _():
        fetch(b + 1, 1 - slot)

    for c in copies:
        blk_copy(b, slot, *c).wait()
    for u in range(sb):
        _sample_attn_one(u, b * sb + u, slot, past, n_top, page_rows, q8_ref, qz_ref, idxv_ref, ex_ref, oc_ref,
                         gate_ref, selnew_ref, winnew_ref, winnewt_ref, win_ref, mq_ref, mem_ref, tab_ref,
                         onsa_ref, omem_ref, wout_ref, buf)


def _sample_attn_one(u, seq, slot, past, n_top, page_rows, q8_ref, qz_ref, idxv_ref, ex_ref, oc_ref, gate_ref,
                     selnew_ref, winnew_ref, winnewt_ref, win_ref, mq_ref, mem_ref, tab_ref,
                     onsa_ref, omem_ref, wout_ref, buf):
    cur = past // SEL_BLK
    blocks_pp = page_rows // SEL_BLK
    hrow = lax.broadcasted_iota(I32, (NSA_HEADS, 1), 0)
    hgrp = hrow // NSA_GRP
    tab0 = tab_ref[:, 0:1]

    def half_of_group(x):
        return jnp.where(hgrp == 0, x[:, :NSA_HD], x[:, NSA_HD:])

    qz = qz_ref[u]
    xw = win_ref[u]
    wb_rows = xw.shape[1]
    wn = winnew_ref[u]
    j = lax.broadcasted_iota(I32, (1, wb_rows), 1)
    dw = wb_rows - j
    s = _dot(qz.astype(BF16), xw[:LANES].astype(BF16)) + _row_bias(dw, tab_ref)
    s_new = jnp.sum(qz * wn[:, :LANES], axis=-1, keepdims=True) + tab0
    mask = jnp.broadcast_to(dw < WINDOW, s.shape)
    zm = jnp.where(mask, s, NEG)
    m = jnp.maximum(jnp.max(zm, axis=-1, keepdims=True), s_new)
    e = jnp.exp(zm - m) * mask.astype(F32)
    e_new = jnp.exp(s_new - m)
    ow = (_dot_nt(e.astype(BF16), xw[LANES:].astype(BF16)) + e_new * wn[:, LANES:]) / jnp.maximum(
        jnp.sum(e, axis=-1, keepdims=True) + e_new, 1e-30)
    ow = half_of_group(ow)
    seq_lane = lax.broadcasted_iota(I32, winnewt_ref.shape, 1)
    new_col = jnp.sum(jnp.where(seq_lane == seq, winnewt_ref[...], 0.0), axis=-1, keepdims=True)
    out_lane = lax.broadcasted_iota(I32, xw.shape, 1)
    wout_ref[u] = jnp.where(out_lane == wb_rows - 1, new_col, pltpu.roll(xw, wb_rows - 1, axis=1))

    mq = mq_ref[u].astype(BF16)
    n_mem = mem_ref.shape[1] // (2 * MEM_HEADS)
    omem = jnp.zeros((NSA_HEADS, MEM_HD), F32)
    for h in range(MEM_HEADS):
        kh = mem_ref[u, pl.ds(h, n_mem, stride=2 * MEM_HEADS), :].astype(BF16)
        vh = mem_ref[u, pl.ds(MEM_HEADS + h, n_mem, stride=2 * MEM_HEADS), :].astype(BF16)
        sm = _dot_nt(mq, kh) * (MEM_HD ** -0.5)
        em = jnp.exp(sm - jnp.max(sm, axis=-1, keepdims=True))
        pm = em / jnp.sum(em, axis=-1, keepdims=True)
        omem = jnp.where(hrow == h, _dot(pm.astype(BF16), vh), omem)
    omem_ref[u] = omem

    q8 = q8_ref[u]
    q8b = q8.astype(BF16)
    sn = selnew_ref[u]
    ncols = n_top * page_rows
    col = lax.broadcasted_iota(I32, (1, ncols), 1)
    idxe = _dot(idxv_ref[u].astype(BF16), ex_ref[...])
    osel = jnp.zeros((NSA_HEADS, NSA_HD), F32)
    for g in range(NSA_KV):
        kt = buf[slot, u, g, 0].astype(BF16)
        vt = buf[slot, u, g, 1].astype(BF16)
        blk = idxe[g:g + 1, :].astype(I32)
        in_page = col % page_rows
        spos = blk * SEL_BLK + in_page % SEL_BLK
        mask = (blk < cur) & (in_page // SEL_BLK == blk % blocks_pp)
        sg = _dot(q8b, kt) + _row_bias(past - spos, tab_ref)
        s_new = jnp.sum(q8 * sn[:, g * NSA_HD:(g + 1) * NSA_HD], axis=-1, keepdims=True) + tab0
        new_sel = jnp.max(jnp.where(blk == cur, 1.0, 0.0), axis=-1, keepdims=True) > 0.5
        maskb = jnp.broadcast_to(mask, sg.shape)
        zm = jnp.where(maskb, sg, NEG)
        z_new = jnp.where(new_sel, s_new, NEG)
        m = jnp.maximum(jnp.max(zm, axis=-1, keepdims=True), z_new)
        e = jnp.exp(zm - m) * maskb.astype(F32)
        e_new = jnp.exp(z_new - m) * new_sel.astype(F32)
        v_new = sn[:, (NSA_KV + g) * NSA_HD:(NSA_KV + g + 1) * NSA_HD]
        og = (_dot_nt(e.astype(BF16), vt) + e_new * v_new) / jnp.maximum(
            jnp.sum(e, axis=-1, keepdims=True) + e_new, 1e-30)
        osel = jnp.where(hgrp == g, og, osel)

    gates = gate_ref[u]
    onsa_ref[u] = gates[:, 0:1] * half_of_group(oc_ref[u]) + gates[:, 1:2] * osel + gates[:, 2:3] * ow


def _sample_attn(table, idx2d, pool, q8, qz, idxv, ex, oc, gate8, selnew, winnew, winnewt, win, mq8, mem, tab8, past,
                 n_top):
    nb = q8.shape[0]
    sb = math.gcd(nb, _ATTN_BATCH)
    page_rows = pool.shape[4]
    blk3 = lambda a: pl.BlockSpec((sb,) + a.shape[1:], lambda i, pt, ix: (i, 0, 0))
    whole = lambda a: pl.BlockSpec(a.shape, lambda i, pt, ix: (0, 0))
    gs = pltpu.PrefetchScalarGridSpec(
        num_scalar_prefetch=2,
        grid=(nb // sb,),
        in_specs=[pl.BlockSpec(memory_space=pl.ANY), blk3(q8), blk3(qz), blk3(idxv), whole(ex), blk3(oc), blk3(gate8),
                  blk3(selnew), blk3(winnew), whole(winnewt), blk3(win), blk3(mq8), blk3(mem), whole(tab8)],
        out_specs=[pl.BlockSpec((sb, NSA_HEADS, NSA_HD), lambda i, pt, ix: (i, 0, 0)),
                   pl.BlockSpec((sb, NSA_HEADS, MEM_HD), lambda i, pt, ix: (i, 0, 0)),
                   pl.BlockSpec((sb,) + win.shape[1:], lambda i, pt, ix: (i, 0, 0))],
        scratch_shapes=[pltpu.VMEM((2, sb, NSA_KV, 2, NSA_HD, n_top * page_rows), F32),
                        pltpu.SemaphoreType.DMA((2,))],
    )
    return pl.pallas_call(
        functools.partial(_sample_attn_kernel, past=past),
        grid_spec=gs,
        out_shape=[jax.ShapeDtypeStruct((nb, NSA_HEADS, NSA_HD), F32),
                   jax.ShapeDtypeStruct((nb, NSA_HEADS, MEM_HD), F32),
                   jax.ShapeDtypeStruct(win.shape, F32)],
        compiler_params=_cparams(("arbitrary",)),
    )(table, idx2d, pool, q8, qz, idxv, ex, oc, gate8, selnew, winnew, winnewt, win, mq8, mem, tab8)


def _merge_ffn_kernel(x_ref, ol_ref, on_ref, om_ref, mg_ref, wa_ref, wb_ref, wm_ref, wo_ref, gf_ref, wg_ref,
                      wu_ref, wd_ref, y_ref):
    d = x_ref.shape[1]
    mg = mg_ref[...]
    z = (mg[:, 0:d] * _dot(ol_ref[...], wa_ref[...]) + mg[:, d:2 * d] * _dot(on_ref[...], wb_ref[...])
         + mg[:, 2 * d:3 * d] * _dot(om_ref[...], wm_ref[...]))
    h = x_ref[...] + _dot(z.astype(BF16), wo_ref[...])
    f = (h * lax.rsqrt(jnp.mean(h * h, axis=-1, keepdims=True) + EPS) * gf_ref[...]).astype(BF16)
    a = jax.nn.silu(_dot(f, wg_ref[...])) * _dot(f, wu_ref[...])
    y_ref[...] = h + _dot(a.astype(BF16), wd_ref[...])


def _merge_ffn(x2d, ol, on, om, mg, prm, tm):
    n, d = x2d.shape
    row = lambda a: pl.BlockSpec((tm, a.shape[1]), lambda i: (i, 0))
    consts = [prm['w_up_a'], prm['w_up_b'], prm['w_up_m'], prm['w_o'], prm['norm_ffn'], prm['w_ffn_gate'],
              prm['w_ffn_up'], prm['w_ffn_down']]
    acts = [x2d, ol, on, om, mg]
    return pl.pallas_call(
        _merge_ffn_kernel,
        grid=(n // tm,),
        in_specs=[row(a) for a in acts] + [pl.BlockSpec(c.shape, lambda i: (0, 0), pipeline_mode=pl.Buffered(1))
                                           for c in consts],
        out_specs=pl.BlockSpec((tm, d), lambda i: (i, 0)),
        out_shape=jax.ShapeDtypeStruct((n, d), F32),
        compiler_params=_cparams(("parallel",)),
    )(*acts, *consts)


def _round_up(n, m):
    return -(-n // m) * m


def _overlap(ncp, n_cmp, nsp):
    cs = jnp.arange(ncp)[:, None] * CMP_STRIDE
    ss = jnp.arange(nsp)[None, :] * SEL_BLK
    hit = (cs < ss + SEL_BLK) & (cs + CMP_LEN > ss) & (jnp.arange(ncp)[:, None] < n_cmp)
    return hit.astype(BF16)


def _prep_params(norm_mix, w_in, conv_w, conv_b, w_lru_a, b_lru_a, w_lru_i, b_lru_i, lru_lambda, g_nsa_q, g_nsa_k,
                 pe_cmp_k, w_cmp_k1, w_cmp_k2, pe_cmp_v, w_cmp_v1, w_cmp_v2, norm_mem, w_mem_kv, g_mem_q, g_mem_k,
                 w_up_a, w_up_b, w_up_m, w_o, norm_ffn, w_ffn_gate, w_ffn_up, w_ffn_down):
    row = lambda v: v.reshape(1, -1).astype(F32)
    n_gg = N_BRANCH * NSA_GRP
    w = w_in
    zpad = jnp.zeros((w.shape[0], LANES - n_gg), w.dtype)
    ng0 = _OFF_NG
    w_packed = jnp.concatenate(
        [w[:, :ng0], w[:, ng0:ng0 + n_gg], zpad, w[:, ng0 + n_gg:ng0 + 2 * n_gg], zpad, w[:, ng0 + 2 * n_gg:]],
        axis=1).astype(BF16)
    eye = jnp.eye(NSA_KV, dtype=F32)

    def w1_big(w1):
        wr = w1.reshape(2, CMP_STRIDE, NSA_HD, CMP_HID)
        big = jnp.einsum('hrdj,gk->rgdhkj', wr, eye)
        return big.reshape(CMP_STRIDE * NSA_KV * NSA_HD, 2 * NSA_KV * CMP_HID).astype(BF16)

    def w2_bd(w2):
        return jnp.einsum('jd,gk->gjkd', w2, eye).reshape(NSA_KV * CMP_HID, NSA_KV * NSA_HD).astype(BF16)

    pe8 = lambda pe: jnp.broadcast_to(pe.reshape(1, -1), (8, pe.size)).astype(BF16)
    return dict(
        norm_mix=row(norm_mix), w_in=w_packed,
        bd_q=_block_diag_ones(NSA_HEADS * NSA_HD, NSA_HD), bd_k=_block_diag_ones(NSA_KV * NSA_HD, NSA_HD),
        bd_m=_block_diag_ones(MEM_HEADS * MEM_HD, MEM_HD),
        gq=row(jnp.tile(g_nsa_q, NSA_HEADS)), gk_cmp=row(jnp.tile(g_nsa_k[0], NSA_KV)),
        gk_sel=row(jnp.tile(g_nsa_k[1], NSA_KV)), gk_win=row(jnp.tile(g_nsa_k[2], NSA_KV)),
        gmq=row(jnp.tile(g_mem_q, MEM_HEADS)), gmk=row(jnp.tile(g_mem_k, MEM_HEADS)),
        conv_w=conv_w.astype(F32), conv_b=row(conv_b), w_lru_a=w_lru_a.astype(BF16), w_lru_i=w_lru_i.astype(BF16),
        b_lru_a=row(b_lru_a), b_lru_i=row(b_lru_i), lru_lambda=row(lru_lambda),
        w1k_big=w1_big(w_cmp_k1), w1v_big=w1_big(w_cmp_v1), w1k=w_cmp_k1.astype(BF16), w1v=w_cmp_v1.astype(BF16),
        pe_k=pe8(pe_cmp_k), pe_v=pe8(pe_cmp_v), w2k_bd=w2_bd(w_cmp_k2), w2v_bd=w2_bd(w_cmp_v2),
        norm_mem=row(norm_mem), w_mem_kv=w_mem_kv.astype(BF16),
        w_up_a=w_up_a.astype(BF16), w_up_b=w_up_b.astype(BF16), w_up_m=w_up_m.astype(BF16), w_o=w_o.astype(BF16),
        norm_ffn=row(norm_ffn), w_ffn_gate=w_ffn_gate.astype(BF16), w_ffn_up=w_ffn_up.astype(BF16),
        w_ffn_down=w_ffn_down.astype(BF16))


def _feature_major(cache):
    return jnp.transpose(cache, (0, 2, 3, 4, 1))


def kernel(x_prompt, x_sample, mem_prompt, cache_cmp_kv, cache_sel_kv, page_table, cache_win_kv, cache_mem_kv,
           state_lru_h, state_conv, rel_bias, norm_mix, w_in, conv_w, conv_b, w_lru_a, b_lru_a, w_lru_i, b_lru_i,
           lru_lambda, g_nsa_q, g_nsa_k, pe_cmp_k, w_cmp_k1, w_cmp_k2, pe_cmp_v, w_cmp_v1, w_cmp_v2, norm_mem,
           w_mem_kv, g_mem_q, g_mem_k, w_up_a, w_up_b, w_up_m, w_o, norm_ffn, w_ffn_gate, w_ffn_up, w_ffn_down):
    assert norm_mix.shape[0] == 1 and x_sample.shape[1] == 1
    weights = (norm_mix, w_in, conv_w, conv_b, w_lru_a, b_lru_a, w_lru_i, b_lru_i, lru_lambda, g_nsa_q, g_nsa_k,
               pe_cmp_k, w_cmp_k1, w_cmp_k2, pe_cmp_v, w_cmp_v1, w_cmp_v2, norm_mem, w_mem_kv, g_mem_q, g_mem_k,
               w_up_a, w_up_b, w_up_m, w_o, norm_ffn, w_ffn_gate, w_ffn_up, w_ffn_down)
    prm = _prep_params(*[w[0] for w in weights])
    bsz, t, d = x_prompt.shape
    db = x_sample.shape[0]
    n_pages = page_table.shape[1]
    page_rows = cache_cmp_kv.shape[2]
    past = n_pages * page_rows
    n_sel = -(-(past + 1) // SEL_BLK)
    assert t % _FAR_TK == 0 and page_rows % LANES == 0 and n_sel <= 256

    xp2 = x_prompt.reshape(bsz * t, d)
    (lx, gg, q, cmp_p, selk, wink, mq, mg, cmpt_p, selt_p, wint_p, selvt, winvt, ngt) = _project(xp2, prm, 256, bsz)
    o_lru, h_p, cv_p = _lru_prompt(lx.reshape(bsz, t, d), gg.reshape(bsz, t, d), prm, 256)

    chunks_pp = page_rows // CMP_STRIDE
    pages_p = t // page_rows
    ncp_p = _round_up(t // CMP_STRIDE, LANES)
    pool_p = cmp_p.reshape(bsz * pages_p, chunks_pp, CHUNK_W)
    table_p = jnp.arange(bsz * pages_p, dtype=I32).reshape(bsz, pages_p)
    kc_p, vct_p = _compress(pool_p, table_p, prm, ncp_p, feature_major=False)

    nsp_p = _round_up(t // SEL_BLK, LANES)
    wb, cb = _bias_tiles(rel_bias.astype(F32), ncp_p)
    ovt_p = _overlap(ncp_p, t // CMP_STRIDE - 1, nsp_p).T
    ext_p = ((jnp.arange(t) // SEL_BLK)[:, None] == jnp.arange(nsp_p)[None, :]).astype(BF16)
    o_nsa = _nsa_prompt(q.reshape(bsz, t, -1), kc_p, vct_p, selk.reshape(bsz, t, LANES), selvt,
                        wink.reshape(bsz, t, LANES), winvt, ngt, wb, cb, ovt_p, ext_p)

    m_rows = mem_prompt.shape[1]
    mkv = _mem_kv(mem_prompt.reshape(bsz * m_rows, d), prm)
    o_mem = _mem_attn(mq.reshape(bsz, t, -1), mkv.reshape(bsz, m_rows, -1), 256)

    y_p = _merge_ffn(xp2, o_lru.reshape(bsz * t, d), o_nsa.reshape(bsz * t, -1), o_mem.reshape(bsz * t, -1), mg,
                     prm, 256)

    xs2 = x_sample.reshape(db, d)
    (lx_s, gg_s, q_s, _, _, _, mq_s, mg_s, cmpt_s, selt_s, wint_s, _, _, ngt_s) = _project(xs2, prm, db, 1)
    sel_s, win_s = selt_s[0].T, wint_s[0].T
    cv0 = jnp.transpose(state_conv[0], (1, 0, 2))
    o_lru_s, h_s, cv_s = _lru_sample(lx_s, gg_s, cv0, state_lru_h[0], prm)

    ncp_s = _round_up(past // CMP_STRIDE, LANES)
    pool_c = _feature_major(cache_cmp_kv[0]).reshape(-1, KV_W, page_rows)
    kc_s, vct_s = _compress(pool_c, page_table, prm, ncp_s, feature_major=True)

    nsp_s = _round_up(n_sel, LANES)
    n_top = min(TOP_N, n_sel)
    ov_s = _overlap(ncp_s, (past + 1) // CMP_STRIDE - 1, nsp_s)
    tab8 = jnp.zeros((NSA_HEADS, LANES), F32).at[:, :REL_BUCKETS].set(rel_bias.astype(F32).T)
    q8 = q_s.reshape(db, NSA_HEADS, NSA_HD)
    qz = jnp.concatenate([jnp.where(jnp.arange(NSA_HEADS)[None, :, None] // NSA_GRP == gi, q8, 0.0)
                          for gi in range(NSA_KV)], axis=2)
    oc_s, idx = _sample_score(qz, kc_s, vct_s, tab8, ov_s, past, n_sel)
    idx2d = idx[:, :NSA_KV, :n_top].astype(I32).reshape(db, NSA_KV * n_top)

    n_gg = N_BRANCH * NSA_GRP
    gate8 = jnp.concatenate([ngt_s[0, gi * LANES:gi * LANES + n_gg, :] for gi in range(NSA_KV)], axis=0).T
    gate8 = jnp.pad(gate8.reshape(db, NSA_HEADS, N_BRANCH), ((0, 0), (0, 0), (0, LANES - N_BRANCH)))
    mq8 = jnp.pad(mq_s.astype(F32).reshape(db, MEM_HEADS, MEM_HD), ((0, 0), (0, NSA_HEADS - MEM_HEADS), (0, 0)))
    ex_s = (jnp.arange(LANES)[:, None] == (jnp.arange(n_top * page_rows) // page_rows)[None, :]).astype(BF16)
    pool_s = _feature_major(cache_sel_kv[0])
    win_t = _feature_major(cache_win_kv[0])
    wb_rows = win_t.shape[-1]
    mem_s = cache_mem_kv[0]
    n_mem = mem_s.shape[1]
    o_nsa8, o_mem8, win_new = _sample_attn(
        page_table, idx2d, pool_s, q8, qz, idx, ex_s, oc_s, gate8, sel_s.reshape(db, 1, KV_W),
        win_s.reshape(db, 1, KV_W), win_s.T, win_t.reshape(db, KV_W, wb_rows), mq8,
        mem_s.reshape(db, n_mem * 2 * MEM_HEADS, MEM_HD), tab8, past, n_top)
    o_nsa_s = o_nsa8.reshape(db, NSA_HEADS * NSA_HD)
    o_mem_s = o_mem8[:, :MEM_HEADS].reshape(db, MEM_HEADS * MEM_HD)
    y_s = _merge_ffn(xs2, o_lru_s, o_nsa_s.astype(BF16), o_mem_s.astype(BF16), mg_s, prm, db)

    def rows_major(a):
        n, _, rows = a.shape
        return jnp.transpose(a.reshape(n, 2, NSA_KV, NSA_HD, rows), (0, 4, 1, 2, 3))[None]

    def new_rows(a):
        return jnp.transpose(a.reshape(2, NSA_KV, NSA_HD, db), (3, 0, 1, 2))[None, :, None]

    w_keep = min(WINDOW, t)
    return (y_p.reshape(bsz, t, d), y_s.reshape(db, 1, d),
            rows_major(cmpt_p), new_rows(cmpt_s), rows_major(selt_p), new_rows(selt_s),
            rows_major(wint_p[:, :, t - w_keep:]), rows_major(win_new),
            mkv.reshape(1, bsz, m_rows, 2, MEM_HEADS, MEM_HD),
            h_p.reshape(1, bsz, d), h_s.reshape(1, db, d),
            cv_p.reshape(1, bsz, CONV_W - 1, d), jnp.transpose(cv_s, (1, 0, 2))[None])
```

```python
import functools
import math

import jax
import jax.numpy as jnp
import numpy as np
from jax import lax
from jax.experimental import pallas as pl
from jax.experimental.pallas import tpu as pltpu

F32 = jnp.float32
BF16 = jnp.bfloat16
I32 = jnp.int32

EPS = 1e-6
NEG = -1e30
FORCE_SCORE = 1e6
LOG2E = math.log2(math.e)
LRU_C = 8.0
LRU_BLOCKS = 4
CONV_W = 4
NSA_HEADS = 8
NSA_KV = 2
NSA_GRP = NSA_HEADS // NSA_KV
NSA_HD = 64
CMP_STRIDE = 16
CMP_LEN = 2 * CMP_STRIDE
CMP_HID = 2 * NSA_HD
SEL_BLK = 64
TOP_N = 16
WINDOW = 512
Q_BLOCK = 128
MEM_HEADS = 4
MEM_HD = 128
REL_BUCKETS = 32
REL_MAX_DIST = 128
N_BRANCH = 3

LANES = 128
KV_W = 2 * NSA_KV * NSA_HD
CHUNK_W = CMP_STRIDE * KV_W
VMEM_LIMIT = 56 * 1024 * 1024


def _cparams(sem):
    return pltpu.CompilerParams(dimension_semantics=sem, vmem_limit_bytes=VMEM_LIMIT)


def _full(shape):
    n = len(shape)
    return pl.BlockSpec(shape, lambda *_: (0,) * n)


def _dot(a, b):
    return jnp.dot(a, b, preferred_element_type=F32)


def _dot_nt(a, b):
    return lax.dot_general(a, b, (((1,), (1,)), ((), ())), preferred_element_type=F32)


def _group_rms(z, ones_bd, gain, width):
    ss = _dot((z * z).astype(BF16), ones_bd)
    return z * lax.rsqrt(ss * (1.0 / width) + EPS) * gain


def _masked_softmax(z, mask):
    zm = jnp.where(mask, z, NEG)
    e = jnp.exp(zm - jnp.max(zm, axis=-1, keepdims=True)) * mask.astype(F32)
    return e / jnp.maximum(jnp.sum(e, axis=-1, keepdims=True), 1e-30)


def _t5_thresholds():
    exact = REL_BUCKETS // 2
    n = np.arange(1, 2 * REL_MAX_DIST + 1, dtype=np.float32)
    scaled = np.log(n / np.float32(exact)) / np.float32(math.log(REL_MAX_DIST / exact)) * np.float32(REL_BUCKETS - exact)
    bucket = np.minimum(exact + scaled.astype(np.int32), REL_BUCKETS - 1)
    assert np.all(np.diff(bucket[exact - 1:]) >= 0) and bucket[-1] == REL_BUCKETS - 1
    return [int(np.argmax(bucket >= b)) + 1 for b in range(exact + 1, REL_BUCKETS)]


_T5_THRESHOLDS = _t5_thresholds()


def _t5_bucket(dist):
    n = jnp.maximum(dist, 0)
    exact = REL_BUCKETS // 2
    large = exact + sum((n >= t).astype(I32) for t in _T5_THRESHOLDS)
    return jnp.where(n < exact, n, large)


def _block_diag_ones(n, width):
    i = jnp.arange(n) // width
    return (i[:, None] == i[None, :]).astype(BF16)


def _split_bf16(x):
    hi = x.astype(BF16)
    return hi, (x - hi.astype(F32)).astype(BF16)


_D = 1024
_NG_W = NSA_KV * LANES
_OFF_LX, _OFF_LG, _OFF_Q, _OFF_KV, _OFF_NG, _OFF_MQ, _OFF_MG, _OFF_END = (
    0, 1024, 2048, 2560, 3328, 3584, 4096, 7168)


def _proj_kernel(x_ref, g_ref, w_ref, bdq_ref, bdk_ref, bdm_ref, gq_ref, gks_ref, gkw_ref, gmq_ref,
                 lx_ref, lg_ref, q_ref, cmp_ref, selk_ref, wink_ref, mq_ref, mg_ref,
                 cmpt_ref, selt_ref, wint_ref, selvt_ref, winvt_ref, ngt_ref):
    x = x_ref[...]
    xn = (x * lax.rsqrt(jnp.mean(x * x, axis=-1, keepdims=True) + EPS) * g_ref[...]).astype(BF16)

    def seg(a, b):
        return _dot(xn, w_ref[:, a:b])

    lx_ref[...] = seg(_OFF_LX, _OFF_LG)
    lg_ref[...] = jax.nn.gelu(seg(_OFF_LG, _OFF_Q))
    zq = seg(_OFF_Q, _OFF_KV)
    q_ref[...] = _group_rms(zq, bdq_ref[...], gq_ref[...], NSA_HD) * (NSA_HD ** -0.5)
    zc = seg(_OFF_KV, _OFF_KV + KV_W)
    cmp_ref[...] = zc
    cmpt_ref[0] = zc.T
    for off, gain_ref, t_ref, k_ref, vt_ref in ((_OFF_KV + KV_W, gks_ref, selt_ref, selk_ref, selvt_ref),
                                                (_OFF_KV + 2 * KV_W, gkw_ref, wint_ref, wink_ref, winvt_ref)):
        z = seg(off, off + KV_W)
        kn = _group_rms(z[:, :LANES], bdk_ref[...], gain_ref[...], NSA_HD)
        vt = z[:, LANES:].T
        t_ref[0, :LANES, :] = kn.T
        t_ref[0, LANES:, :] = vt
        k_ref[...] = kn.astype(BF16)
        vt_ref[0] = vt.astype(BF16)
    ngt_ref[0] = jax.nn.sigmoid(seg(_OFF_NG, _OFF_MQ)).T
    zm = seg(_OFF_MQ, _OFF_MG)
    mq_ref[...] = _group_rms(zm, bdm_ref[...], gmq_ref[...], MEM_HD).astype(BF16)
    mg_ref[...] = jax.nn.sigmoid(seg(_OFF_MG, _OFF_END))


def _project(x2d, prm, tm, bsz):
    n = x2d.shape[0]
    t = n // bsz
    per_b = t // tm
    row = lambda w: pl.BlockSpec((tm, w), lambda i: (i, 0))
    fmaj = lambda w: pl.BlockSpec((1, w, tm), lambda i: (i // per_b, 0, i % per_b))
    consts = [prm['norm_mix'], prm['w_in'], prm['bd_q'], prm['bd_k'], prm['bd_m'],
              prm['gq'], prm['gk_sel'], prm['gk_win'], prm['gmq']]
    widths = [(_D, F32), (_D, F32), (512, F32), (KV_W, F32), (LANES, BF16), (LANES, BF16), (512, BF16), (3 * _D, F32)]
    fwidths = [(KV_W, F32), (KV_W, F32), (KV_W, F32), (LANES, BF16), (LANES, BF16), (_NG_W, F32)]
    return pl.pallas_call(
        _proj_kernel,
        grid=(n // tm,),
        in_specs=[row(_D)] + [_full(c.shape) for c in consts],
        out_specs=[row(w) for w, _ in widths] + [fmaj(w) for w, _ in fwidths],
        out_shape=[jax.ShapeDtypeStruct((n, w), dt) for w, dt in widths]
                  + [jax.ShapeDtypeStruct((bsz, w, t), dt) for w, dt in fwidths],
        compiler_params=_cparams(("parallel",)),
    )(x2d, *consts)


def _lru_gates(xc, wa_ref, wi_ref, ba_ref, bi_ref, lam_ref):
    bw = xc.shape[1] // LRU_BLOCKS
    ra, ri = [], []
    for n in range(LRU_BLOCKS):
        xr = xc[:, n * bw:(n + 1) * bw].astype(BF16)
        ra.append(_dot(xr, wa_ref[n]))
        ri.append(_dot(xr, wi_ref[n]))
    r = jax.nn.sigmoid(jnp.concatenate(ra, axis=1) + ba_ref[...])
    i = jax.nn.sigmoid(jnp.concatenate(ri, axis=1) + bi_ref[...])
    lam = -lam_ref[...]
    softplus = jnp.maximum(lam, 0.0) + jnp.log1p(jnp.exp(-jnp.abs(lam)))
    log_a = -LRU_C * r * softplus
    a = jnp.exp(log_a)
    b = jnp.sqrt(jnp.tanh(-log_a) * (a * a + 1.0)) * (i * xc)
    return a, b


def _lru_scan(a, b):
    tt = a.shape[0]
    row = lax.broadcasted_iota(I32, a.shape, 0)
    k = 1
    while k < tt:
        keep = row >= k
        b = b + a * jnp.where(keep, pltpu.roll(b, k, axis=0), 0.0)
        a = a * jnp.where(keep, pltpu.roll(a, k, axis=0), 1.0)
        k *= 2
    return a, b


def _lru_prompt_kernel(x_ref, gg_ref, cw_ref, cb_ref, wa_ref, wi_ref, ba_ref, bi_ref, lam_ref,
                       o_ref, h_ref, cv_ref, xbuf, hcar):
    t = pl.program_id(1)
    tt = x_ref.shape[1]

    @pl.when(t == 0)
    def _():
        xbuf[0:8, :] = jnp.zeros((8, xbuf.shape[1]), F32)
        hcar[...] = jnp.zeros(hcar.shape, F32)

    x = x_ref[0]
    xbuf[8:8 + tt, :] = x
    xc = cb_ref[...] + xbuf[5:5 + tt, :] * cw_ref[0:1, :]
    for k in range(1, CONV_W):
        xc = xc + xbuf[5 + k:5 + k + tt, :] * cw_ref[k:k + 1, :]
    a, b = _lru_gates(xc, wa_ref, wi_ref, ba_ref, bi_ref, lam_ref)
    ap, hs = _lru_scan(a, b)
    h = hs + ap * hcar[...]
    o_ref[0] = (h * gg_ref[0]).astype(BF16)
    hcar[...] = h[tt - 1:tt, :]
    xbuf[0:8, :] = x[tt - 8:tt, :]

    @pl.when(t == pl.num_programs(1) - 1)
    def _():
        h_ref[0] = h[tt - 1:tt, :]
        cv_ref[0] = x[tt - (CONV_W - 1):tt, :]


def _lru_prompt(lx, gg, prm, tt):
    b, t, w = lx.shape
    blk = pl.BlockSpec((1, tt, w), lambda i, j: (i, j, 0))
    consts = [prm['conv_w'], prm['conv_b'], prm['w_lru_a'], prm['w_lru_i'], prm['b_lru_a'], prm['b_lru_i'],
              prm['lru_lambda']]
    return pl.pallas_call(
        _lru_prompt_kernel,
        grid=(b, t // tt),
        in_specs=[blk, blk] + [_full(c.shape) for c in consts],
        out_specs=[blk, pl.BlockSpec((1, 1, w), lambda i, j: (i, 0, 0)),
                   pl.BlockSpec((1, CONV_W - 1, w), lambda i, j: (i, 0, 0))],
        out_shape=[jax.ShapeDtypeStruct((b, t, w), BF16), jax.ShapeDtypeStruct((b, 1, w), F32),
                   jax.ShapeDtypeStruct((b, CONV_W - 1, w), F32)],
        scratch_shapes=[pltpu.VMEM((tt + 8, w), F32), pltpu.VMEM((1, w), F32)],
        compiler_params=_cparams(("parallel", "arbitrary")),
    )(lx, gg, *consts)


def _lru_sample_kernel(x_ref, gg_ref, cv0_ref, h0_ref, cw_ref, cb_ref, wa_ref, wi_ref, ba_ref, bi_ref, lam_ref,
                       o_ref, h_ref, cv_ref):
    x = x_ref[...]
    xc = cb_ref[...] + x * cw_ref[CONV_W - 1:CONV_W, :]
    for k in range(CONV_W - 1):
        xc = xc + cv0_ref[k] * cw_ref[k:k + 1, :]
    a, b = _lru_gates(xc, wa_ref, wi_ref, ba_ref, bi_ref, lam_ref)
    h = a * h0_ref[...] + b
    o_ref[...] = (h * gg_ref[...]).astype(BF16)
    h_ref[...] = h
    for k in range(CONV_W - 2):
        cv_ref[k] = cv0_ref[k + 1]
    cv_ref[CONV_W - 2] = x


def _lru_sample(lx, gg, cv0, h0, prm):
    n, w = lx.shape
    consts = [prm['conv_w'], prm['conv_b'], prm['w_lru_a'], prm['w_lru_i'], prm['b_lru_a'], prm['b_lru_i'],
              prm['lru_lambda']]
    args = [lx, gg, cv0, h0] + consts
    return pl.pallas_call(
        _lru_sample_kernel,
        grid=(1,),
        in_specs=[_full(a.shape) for a in args],
        out_specs=[_full((n, w)), _full((n, w)), _full((CONV_W - 1, n, w))],
        out_shape=[jax.ShapeDtypeStruct((n, w), BF16), jax.ShapeDtypeStruct((n, w), F32),
                   jax.ShapeDtypeStruct((CONV_W - 1, n, w), F32)],
        compiler_params=_cparams(("arbitrary",)),
    )(*args)


_PAGE_SECTIONS = 2


def _compress_kernel(pt_ref, pool_ref, w1k_ref, w1v_ref, w1kp_ref, w1vp_ref, pek_ref, pev_ref, w2k_ref, w2v_ref,
                     bdk_ref, gk_ref, kc_ref, vct_ref, buf, sem, *scratch, n_valid, feature_major):
    b = pl.program_id(0)
    nb = pl.num_programs(0)
    n_pages = pt_ref.shape[1]
    ncp = kc_ref.shape[1]

    def page_copy(step, slot, p):
        return pltpu.make_async_copy(pool_ref.at[pt_ref[step, p]], buf.at[slot, p], sem.at[slot])

    def fetch(step, slot):
        def body(p, c):
            page_copy(step, slot, p).start()
            return c
        lax.fori_loop(0, n_pages, body, 0)

    slot = b % 2

    @pl.when(b == 0)
    def _():
        fetch(0, 0)

    @pl.when(b + 1 < nb)
    def _():
        fetch(b + 1, 1 - slot)

    def wait_body(p, c):
        page_copy(b, slot, p).wait()
        return c

    lax.fori_loop(0, n_pages, wait_body, 0)

    n_sec = math.gcd(n_pages, _PAGE_SECTIONS)
    sec_pages = n_pages // n_sec
    if feature_major:
        zbuf, = scratch
        page_rows = buf.shape[3]
        n_ch = n_pages * page_rows // CMP_STRIDE
        sec_ch = n_ch // n_sec

        def chunk_rows(kv, sec):
            for p in range(sec * sec_pages, (sec + 1) * sec_pages):
                zbuf[kv, p * page_rows:(p + 1) * page_rows, :] = buf[slot, p, kv * LANES:(kv + 1) * LANES, :].T
            return jnp.concatenate(
                [zbuf[kv, pl.ds(sec * sec_ch * CMP_STRIDE + r, sec_ch, stride=CMP_STRIDE), :].astype(BF16)
                 for r in range(CMP_STRIDE)], axis=1)
    else:
        rows_pp = buf.shape[2]
        n_ch = n_pages * rows_pp
        sec_ch = n_ch // n_sec

        def chunk_rows(kv, sec):
            pages = slice(sec * sec_pages, (sec + 1) * sec_pages)
            return jnp.concatenate(
                [buf[slot, pages, :, r * KV_W + kv * LANES:r * KV_W + (kv + 1) * LANES]
                 .reshape(sec_ch, LANES).astype(BF16) for r in range(CMP_STRIDE)], axis=1)

    row = lax.broadcasted_iota(I32, (n_ch, 1), 0)
    keep = row < n_valid
    outs = []
    for kv, w1_ref, w1p_ref, pe_ref, w2_ref in ((0, w1k_ref, w1kp_ref, pek_ref, w2k_ref),
                                                 (1, w1v_ref, w1vp_ref, pev_ref, w2v_ref)):
        hh = jnp.concatenate([_dot(chunk_rows(kv, sec), w1_ref[...]) for sec in range(n_sec)], axis=0)
        pos = _dot(pe_ref[...], w1p_ref[...])[0:1, :]
        pos = jnp.concatenate([pos, pos], axis=1)
        nh = NSA_KV * CMP_HID
        h = hh[:, :nh] + pltpu.roll(hh[:, nh:], n_ch - 1, axis=0) + pos
        outs.append(_dot(jax.nn.gelu(h).astype(BF16), w2_ref[...]))
    kc = _group_rms(outs[0], bdk_ref[...], gk_ref[...], NSA_HD)
    kc = jnp.where(keep, kc, 0.0)
    vc = jnp.where(keep, outs[1], 0.0)
    if ncp > n_ch:
        pad = jnp.zeros((ncp - n_ch, kc.shape[1]), F32)
        kc = jnp.concatenate([kc, pad], axis=0)
        vc = jnp.concatenate([vc, pad], axis=0)
    kc_ref[0] = kc.astype(BF16)
    vct_ref[0] = vc.T.astype(BF16)


def _compress(pool, table, prm, ncp, feature_major):
    nb, n_pages = table.shape
    if feature_major:
        page_rows = pool.shape[2]
        n_ch = n_pages * page_rows // CMP_STRIDE
        scratch = [pltpu.VMEM((2, n_pages * page_rows, LANES), F32)]
    else:
        n_ch = n_pages * pool.shape[1]
        scratch = []
    consts = [prm['w1k_big'], prm['w1v_big'], prm['w1k'], prm['w1v'], prm['pe_k'], prm['pe_v'],
              prm['w2k_bd'], prm['w2v_bd'], prm['bd_k'], prm['gk_cmp']]
    out_blk = pl.BlockSpec((1, ncp, LANES), lambda i, pt: (i, 0, 0))
    gs = pltpu.PrefetchScalarGridSpec(
        num_scalar_prefetch=1,
        grid=(nb,),
        in_specs=[pl.BlockSpec(memory_space=pl.ANY)] + [pl.BlockSpec(c.shape, lambda i, pt, _n=len(c.shape): (0,) * _n)
                                                         for c in consts],
        out_specs=[out_blk, pl.BlockSpec((1, LANES, ncp), lambda i, pt: (i, 0, 0))],
        scratch_shapes=[pltpu.VMEM((2, n_pages) + pool.shape[1:], F32), pltpu.SemaphoreType.DMA((2,))] + scratch,
    )
    return pl.pallas_call(
        functools.partial(_compress_kernel, n_valid=n_ch - 1, feature_major=feature_major),
        grid_spec=gs,
        out_shape=[jax.ShapeDtypeStruct((nb, ncp, LANES), BF16), jax.ShapeDtypeStruct((nb, LANES, ncp), BF16)],
        compiler_params=_cparams(("arbitrary",)),
    )(table, pool, *consts)


_WIN_TILES = WINDOW // Q_BLOCK + 1
_QCOLS = NSA_HEADS * Q_BLOCK


def _bias_tiles_kernel(tab_ref, wb_ref, cb_ref):
    g = pl.program_id(0)
    wrows = wb_ref.shape[0]
    ncp = cb_ref.shape[0] // 2
    c_w = lax.broadcasted_iota(I32, (wrows, Q_BLOCK), 0)
    i_w = lax.broadcasted_iota(I32, (wrows, Q_BLOCK), 1)
    d_w = (_WIN_TILES - 1) * Q_BLOCK + i_w - c_w
    k_c = lax.broadcasted_iota(I32, (2 * ncp, Q_BLOCK), 0)
    i_c = lax.broadcasted_iota(I32, (2 * ncp, Q_BLOCK), 1)
    d_c = i_c - (CMP_LEN - 1) - CMP_STRIDE * (k_c - ncp)
    for d, ok, ref in ((d_w, (d_w >= 0) & (d_w < WINDOW), wb_ref), (d_c, d_c >= 0, cb_ref)):
        bucket = _t5_bucket(d)
        for r in range(NSA_GRP):
            h = g * NSA_GRP + r
            acc = jnp.zeros(d.shape, F32)
            for bk in range(REL_BUCKETS):
                acc = jnp.where(bucket == bk, tab_ref[bk, h], acc)
            far = tab_ref[REL_BUCKETS - 1, h]
            ref[:, r * Q_BLOCK:(r + 1) * Q_BLOCK] = jnp.where(ok, (acc - far) * LOG2E, NEG)


def _bias_tiles(rel_bias, ncp):
    wrows = _WIN_TILES * Q_BLOCK
    gcols = NSA_GRP * Q_BLOCK
    return pl.pallas_call(
        _bias_tiles_kernel,
        grid=(NSA_KV,),
        in_specs=[pl.BlockSpec(memory_space=pltpu.SMEM)],
        out_specs=[pl.BlockSpec((wrows, gcols), lambda g: (0, g)),
                   pl.BlockSpec((2 * ncp, gcols), lambda g: (0, g))],
        out_shape=[jax.ShapeDtypeStruct((wrows, _QCOLS), F32),
                   jax.ShapeDtypeStruct((2 * ncp, _QCOLS), F32)],
        compiler_params=_cparams(("arbitrary",)),
    )(rel_bias)


def _topk_select(score, n_top):
    lane = lax.broadcasted_iota(I32, score.shape, 1).astype(F32)
    big = float(score.shape[1])
    slot = lax.broadcasted_iota(I32, (score.shape[0], LANES), 1)
    idx = jnp.zeros((score.shape[0], LANES), F32)
    for it in range(n_top):
        m = jnp.max(score, axis=-1, keepdims=True)
        first = jnp.min(jnp.where(score == m, lane, big), axis=-1, keepdims=True)
        score = jnp.where(lane == first, -jnp.inf, score)
        idx = jnp.where(slot == it, first, idx)
    return idx


def _topk_mask_cols(score, n_top):
    sid = lax.broadcasted_iota(I32, score.shape, 0).astype(F32)
    big = float(score.shape[0])
    sel = jnp.zeros(score.shape, F32)
    for _ in range(n_top):
        m = jnp.max(score, axis=0, keepdims=True)
        first = jnp.min(jnp.where(score == m, sid, big), axis=0, keepdims=True)
        hit = sid == first
        sel = jnp.where(hit, 1.0, sel)
        score = jnp.where(hit, -jnp.inf, score)
    return sel


_FAR_TK = 1024
_FAR_SPLIT = 4
_ONES_ROWS = 16


def _nsa_prompt_kernel(q_ref, kc_ref, vct_ref, selk_ref, selvt_ref, wink_ref, winvt_ref, gate_ref, wb_ref, cb_ref,
                       ovt_ref, ext_ref, o_ref, *, n_sel):
    bi = pl.program_id(1)
    q0 = bi * Q_BLOCK
    ncp = kc_ref.shape[1]
    nsp = ovt_ref.shape[0]
    gcols = NSA_GRP * Q_BLOCK

    lane = lax.broadcasted_iota(I32, (Q_BLOCK, LANES), 1)
    qf = q_ref[0] * LOG2E
    qz = []
    for h in range(NSA_HEADS):
        g = h // NSA_GRP
        blk = qf[:, (h // 2) * LANES:(h // 2 + 1) * LANES]
        if h % 2 != g:
            blk = pltpu.roll(blk, NSA_HD, axis=1)
        qz.append(jnp.where((lane // NSA_HD) == g, blk, 0.0))
    qz = jnp.concatenate(qz, axis=0).astype(BF16)

    def col_max(z):
        while z.shape[0] > 64 and z.shape[0] % 16 == 0:
            half = z.shape[0] // 2
            z = jnp.maximum(z[:half], z[half:])
        return jnp.max(z, axis=0, keepdims=True)

    def col_sum(z):
        return jnp.sum(z, axis=0, keepdims=True)

    c0 = pl.multiple_of(ncp - bi * (Q_BLOCK // CMP_STRIDE), Q_BLOCK // CMP_STRIDE)
    z = _dot_nt(kc_ref[0], qz) + cb_ref[pl.ds(c0, ncp), :]
    m = col_max(z)
    e = jnp.exp2(z - m)
    pc = e * jnp.where(m > 0.5 * NEG, 1.0 / col_sum(e), 0.0)
    oc = _dot(vct_ref[0], pc.astype(BF16))

    pcs = []
    for g in range(NSA_KV):
        acc = pc[:, g * gcols:g * gcols + Q_BLOCK]
        for r in range(1, NSA_GRP):
            acc = acc + pc[:, g * gcols + r * Q_BLOCK:g * gcols + (r + 1) * Q_BLOCK]
        pcs.append(acc)
    hi, lo = _split_bf16(jnp.concatenate(pcs, axis=1))
    ssum = _dot(ovt_ref[...], hi) + _dot(ovt_ref[...], lo)
    s_id = lax.broadcasted_iota(I32, (nsp, NSA_KV * Q_BLOCK), 0)
    cur = (q0 + lax.broadcasted_iota(I32, (1, NSA_KV * Q_BLOCK), 1) % Q_BLOCK) // SEL_BLK
    forced = (s_id == 0) | (s_id == cur) | (s_id == cur - 1)
    score = jnp.where(forced, FORCE_SCORE, jnp.where(s_id <= cur, ssum, -1.0))
    selm = _topk_mask_cols(score, min(TOP_N, n_sel)).T

    def with_mask(sel):
        u = ((sel - 1.0) * (-NEG)).astype(BF16)
        rows = [u[g * Q_BLOCK:(g + 1) * Q_BLOCK] for g in range(NSA_KV) for _ in range(NSA_GRP)]
        return jnp.concatenate([qz, jnp.concatenate(rows, axis=0)], axis=1)

    blk_id = lax.broadcasted_iota(I32, (NSA_KV * Q_BLOCK, nsp), 1)
    qz_near = with_mask(selm)
    qz_far = with_mask(jnp.where(blk_id < 2 * bi - 2, selm, 0.0))

    p0 = pl.multiple_of(jnp.maximum(q0 - Q_BLOCK, 0), Q_BLOCK)
    d0 = pl.multiple_of(q0, Q_BLOCK)
    kn = jnp.concatenate(
        [jnp.concatenate([selk_ref[0, pl.ds(k0, Q_BLOCK), :], ext_ref[pl.ds(k0, Q_BLOCK), :]], axis=1)
         for k0 in (p0, d0)], axis=0)
    vn = jnp.concatenate([selvt_ref[0, :, pl.ds(p0, Q_BLOCK)], selvt_ref[0, :, pl.ds(d0, Q_BLOCK)]], axis=1)
    krow = lax.broadcasted_iota(I32, (2 * Q_BLOCK, 1), 0)
    wrows = wb_ref.shape[0]
    z = _dot_nt(kn, qz_near) + wb_ref[wrows - 2 * Q_BLOCK:wrows, :]
    z = z + jnp.where((krow < Q_BLOCK) & (bi == 0), NEG, 0.0)
    def with_ones(vt):
        return jnp.concatenate([vt, jnp.ones((_ONES_ROWS, vt.shape[1]), BF16)], axis=0)

    m = col_max(z)
    carry = (m, _dot(with_ones(vn), jnp.exp2((z - m).astype(BF16))))

    far_blocks = _FAR_TK // Q_BLOCK
    n_far = (bi + far_blocks - 2) // far_blocks

    sub = _FAR_TK // _FAR_SPLIT

    def far_body(t, carry):
        m, acc = carry
        starts = [pl.multiple_of(t * _FAR_TK + s * sub, sub) for s in range(_FAR_SPLIT)]
        zs = [_dot_nt(jnp.concatenate([selk_ref[0, pl.ds(k0, sub), :], ext_ref[pl.ds(k0, sub), :]], axis=1), qz_far)
              for k0 in starts]
        for k0, z in zip(starts, zs):
            m_new = jnp.maximum(m, col_max(z))
            e = jnp.exp2((z - m_new).astype(BF16))
            acc = jnp.exp2(m - m_new) * acc + _dot(with_ones(selvt_ref[0, :, pl.ds(k0, sub)]), e)
            m = m_new
        return m, acc

    _, acc = lax.fori_loop(0, n_far, far_body, carry)
    osel = acc[:LANES] / acc[LANES:LANES + 1]

    ks, vs = [], []
    for t in range(_WIN_TILES):
        st = pl.multiple_of(jnp.maximum(q0 - (_WIN_TILES - 1 - t) * Q_BLOCK, 0), Q_BLOCK)
        ks.append(wink_ref[0, pl.ds(st, Q_BLOCK), :])
        vs.append(winvt_ref[0, :, pl.ds(st, Q_BLOCK)])
    wrow = lax.broadcasted_iota(I32, (wrows, 1), 0)
    before_start = jnp.where(wrow < (_WIN_TILES - 1 - bi) * Q_BLOCK, NEG, 0.0)
    z = _dot_nt(jnp.concatenate(ks, axis=0), qz) + wb_ref[...] + before_start
    ow = _dot(with_ones(jnp.concatenate(vs, axis=1)), jnp.exp2((z - col_max(z)).astype(BF16)))
    ow = ow[:LANES] / ow[LANES:LANES + 1]

    gates = gate_ref[0]

    def gate_row(br):
        rows = [g * LANES + r * N_BRANCH + br for g in range(NSA_KV) for r in range(NSA_GRP)]
        return jnp.concatenate([gates[c:c + 1, :] for c in rows], axis=1)

    mixed = gate_row(0) * oc + gate_row(1) * osel + gate_row(2) * ow
    out = []
    for j in range(NSA_HEADS // 2):
        g = (2 * j) // NSA_GRP
        pair = jnp.concatenate([mixed[g * NSA_HD:(g + 1) * NSA_HD, (2 * j) * Q_BLOCK:(2 * j + 1) * Q_BLOCK],
                                mixed[g * NSA_HD:(g + 1) * NSA_HD, (2 * j + 1) * Q_BLOCK:(2 * j + 2) * Q_BLOCK]],
                               axis=0)
        out.append(pair.T)
    o_ref[0] = jnp.concatenate(out, axis=1).astype(BF16)


def _nsa_prompt(q, kc, vct, selk, selvt, wink, winvt, gates_t, wb, cb, ovt, ext):
    b, t, hw = q.shape
    ncp = kc.shape[1]
    per_b = lambda shape: pl.BlockSpec((1,) + shape, lambda i, j: (i, 0, 0))
    const = lambda a: pl.BlockSpec(a.shape, lambda i, j: (0, 0), pipeline_mode=pl.Buffered(1))
    return pl.pallas_call(
        functools.partial(_nsa_prompt_kernel, n_sel=t // SEL_BLK),
        grid=(b, t // Q_BLOCK),
        in_specs=[pl.BlockSpec((1, Q_BLOCK, hw), lambda i, j: (i, j, 0)),
                  per_b((ncp, LANES)), per_b((LANES, ncp)),
                  per_b((t, LANES)), per_b((LANES, t)), per_b((t, LANES)), per_b((LANES, t)),
                  pl.BlockSpec((1, _NG_W, Q_BLOCK), lambda i, j: (i, 0, j)),
                  const(wb), const(cb), const(ovt), const(ext)],
        out_specs=pl.BlockSpec((1, Q_BLOCK, hw), lambda i, j: (i, j, 0)),
        out_shape=jax.ShapeDtypeStruct((b, t, hw), BF16),
        compiler_params=_cparams(("parallel", "parallel")),
    )(q, kc, vct, selk, selvt, wink, winvt, gates_t, wb, cb, ovt, ext)


def _mem_kv_kernel(m_ref, g_ref, w_ref, bd_ref, gk_ref, o_ref):
    x = m_ref[...]
    xn = (x * lax.rsqrt(jnp.mean(x * x, axis=-1, keepdims=True) + EPS) * g_ref[...]).astype(BF16)
    z = _dot(xn, w_ref[...])
    half = MEM_HEADS * MEM_HD
    kn = _group_rms(z[:, :half], bd_ref[...], gk_ref[...], MEM_HD)
    o_ref[...] = jnp.concatenate([kn, z[:, half:]], axis=1)


def _mem_kv(mem2d, prm):
    n = mem2d.shape[0]
    args = [mem2d, prm['norm_mem'], prm['w_mem_kv'], prm['bd_m'], prm['gmk']]
    w = 2 * MEM_HEADS * MEM_HD
    return pl.pallas_call(
        _mem_kv_kernel,
        grid=(1,),
        in_specs=[_full(a.shape) for a in args],
        out_specs=_full((n, w)),
        out_shape=jax.ShapeDtypeStruct((n, w), F32),
        compiler_params=_cparams(("arbitrary",)),
    )(*args)


def _mem_attn_kernel(q_ref, kv_ref, o_ref):
    half = MEM_HEADS * MEM_HD
    out = []
    for h in range(MEM_HEADS):
        sl = slice(h * MEM_HD, (h + 1) * MEM_HD)
        k = kv_ref[0, :, sl].astype(BF16)
        v = kv_ref[0, :, half + h * MEM_HD:half + (h + 1) * MEM_HD].astype(BF16)
        s = _dot_nt(q_ref[0, :, sl], k) * (MEM_HD ** -0.5)
        e = jnp.exp(s - jnp.max(s, axis=-1, keepdims=True))
        p = e / jnp.sum(e, axis=-1, keepdims=True)
        out.append(_dot(p.astype(BF16), v))
    o_ref[0] = jnp.concatenate(out, axis=1).astype(BF16)


def _mem_attn(mq, mkv, tq):
    b, t, w = mq.shape
    m = mkv.shape[1]
    return pl.pallas_call(
        _mem_attn_kernel,
        grid=(b, t // tq),
        in_specs=[pl.BlockSpec((1, tq, w), lambda i, j: (i, j, 0)),
                  pl.BlockSpec((1, m, 2 * w), lambda i, j: (i, 0, 0))],
        out_specs=pl.BlockSpec((1, tq, w), lambda i, j: (i, j, 0)),
        out_shape=jax.ShapeDtypeStruct((b, t, w), BF16),
        compiler_params=_cparams(("parallel", "parallel")),
    )(mq, mkv)


_SCORE_BATCH = 16
_ATTN_BATCH = 2


def _row_bias(dist, tab_ref):
    bucket = _t5_bucket(dist)
    acc = jnp.zeros((NSA_HEADS, dist.shape[1]), F32)
    for bk in range(REL_BUCKETS):
        acc = jnp.where(bucket == bk, tab_ref[:, bk:bk + 1], acc)
    return acc


def _sample_score_kernel(qz_ref, kc_ref, vct_ref, tab_ref, ov_ref, oc_ref, idx_ref, *, past, n_sel):
    sb = qz_ref.shape[0]
    ncp = kc_ref.shape[1]
    nsp = ov_ref.shape[1]
    cpos = lax.broadcasted_iota(I32, (1, ncp), 1) * CMP_STRIDE + (CMP_LEN - 1)
    bias = _row_bias(past - cpos, tab_ref)
    valid = jnp.broadcast_to(cpos <= past, (NSA_HEADS, ncp))
    hrow = lax.broadcasted_iota(I32, (NSA_HEADS, 1), 0)
    pcs = []
    for j in range(sb):
        pc = _masked_softmax(_dot_nt(qz_ref[j].astype(BF16), kc_ref[j]) + bias, valid)
        oc_ref[j] = _dot_nt(pc.astype(BF16), vct_ref[j])
        grp = [jnp.sum(jnp.where(hrow // NSA_GRP == g, pc, 0.0), axis=0, keepdims=True) for g in range(NSA_KV)]
        pcs.append(jnp.concatenate(grp + [jnp.zeros((NSA_HEADS - NSA_KV, ncp), F32)], axis=0))
    hi, lo = _split_bf16(jnp.concatenate(pcs, axis=0))
    ssum = _dot(hi, ov_ref[...]) + _dot(lo, ov_ref[...])
    blk_id = lax.broadcasted_iota(I32, ssum.shape, 1)
    cur = past // SEL_BLK
    forced = (blk_id == 0) | (blk_id == cur) | (blk_id == cur - 1)
    score = jnp.where(forced, FORCE_SCORE, jnp.where(blk_id <= cur, ssum, -1.0))
    score = jnp.where(blk_id < n_sel, score, -jnp.inf)
    idx = _topk_select(score, min(TOP_N, n_sel))
    idx_ref[...] = idx.reshape(sb, NSA_HEADS, LANES)


def _sample_score(qz, kc, vct, tab8, ov, past, n_sel):
    nb = qz.shape[0]
    ncp = kc.shape[1]
    sb = math.gcd(nb, _SCORE_BATCH)
    blk8 = pl.BlockSpec((sb, NSA_HEADS, LANES), lambda i: (i, 0, 0))
    blkc = pl.BlockSpec((sb, ncp, LANES), lambda i: (i, 0, 0))
    return pl.pallas_call(
        functools.partial(_sample_score_kernel, past=past, n_sel=n_sel),
        grid=(nb // sb,),
        in_specs=[blk8, blkc, pl.BlockSpec((sb, LANES, ncp), lambda i: (i, 0, 0)), _full(tab8.shape),
                  _full(ov.shape)],
        out_specs=[blk8, blk8],
        out_shape=[jax.ShapeDtypeStruct((nb, NSA_HEADS, LANES), F32)] * 2,
        compiler_params=_cparams(("parallel",)),
    )(qz, kc, vct, tab8, ov)


def _sample_attn_kernel(pt_ref, idx_ref, pool_ref, q8_ref, qz_ref, idxv_ref, ex_ref, oc_ref, gate_ref, selnew_ref,
                        winnew_ref, winnewt_ref, win_ref, mq_ref, mem_ref, tab_ref,
                        onsa_ref, omem_ref, wout_ref, buf, sem, *, past):
    b = pl.program_id(0)
    nb = pl.num_programs(0)
    sb = buf.shape[1]
    page_rows = pool_ref.shape[4]
    n_top = buf.shape[5] // page_rows
    cur = past // SEL_BLK
    blocks_pp = page_rows // SEL_BLK
    copies = [(u, g, n) for u in range(sb) for g in range(NSA_KV) for n in range(n_top)]

    def blk_copy(step, slot, u, g, n):
        seq = step * sb + u
        blk = jnp.minimum(idx_ref[seq, g * n_top + n], cur - 1)
        page = pt_ref[seq, blk // blocks_pp]
        return pltpu.make_async_copy(pool_ref.at[page, :, g],
                                     buf.at[slot, u, g, :, :, n * page_rows:(n + 1) * page_rows], sem.at[slot])

    def fetch(step, slot):
        for c in copies:
            blk_copy(step, slot, *c).start()

    slot = b % 2

    @pl.when(b == 0)
    def _():
        fetch(0, 0)

    @pl.when(b + 1 < nb)
    def _():
        fetch(b + 1, 1 - slot)

    for c in copies:
        blk_copy(b, slot, *c).wait()
    for u in range(sb):
        _sample_attn_one(u, b * sb + u, slot, past, n_top, page_rows, q8_ref, qz_ref, idxv_ref, ex_ref, oc_ref,
                         gate_ref, selnew_ref, winnew_ref, winnewt_ref, win_ref, mq_ref, mem_ref, tab_ref,
                         onsa_ref, omem_ref, wout_ref, buf)


def _sample_attn_one(u, seq, slot, past, n_top, page_rows, q8_ref, qz_ref, idxv_ref, ex_ref, oc_ref, gate_ref,
                     selnew_ref, winnew_ref, winnewt_ref, win_ref, mq_ref, mem_ref, tab_ref,
                     onsa_ref, omem_ref, wout_ref, buf):
    cur = past // SEL_BLK
    blocks_pp = page_rows // SEL_BLK
    hrow = lax.broadcasted_iota(I32, (NSA_HEADS, 1), 0)
    hgrp = hrow // NSA_GRP
    tab0 = tab_ref[:, 0:1]

    def half_of_group(x):
        return jnp.where(hgrp == 0, x[:, :NSA_HD], x[:, NSA_HD:])

    qz = qz_ref[u]
    xw = win_ref[u]
    wb_rows = xw.shape[1]
    wn = winnew_ref[u]
    j = lax.broadcasted_iota(I32, (1, wb_rows), 1)
    dw = wb_rows - j
    s = _dot(qz.astype(BF16), xw[:LANES].astype(BF16)) + _row_bias(dw, tab_ref)
    s_new = jnp.sum(qz * wn[:, :LANES], axis=-1, keepdims=True) + tab0
    mask = jnp.broadcast_to(dw < WINDOW, s.shape)
    zm = jnp.where(mask, s, NEG)
    m = jnp.maximum(jnp.max(zm, axis=-1, keepdims=True), s_new)
    e = jnp.exp(zm - m) * mask.astype(F32)
    e_new = jnp.exp(s_new - m)
    ow = (_dot_nt(e.astype(BF16), xw[LANES:].astype(BF16)) + e_new * wn[:, LANES:]) / jnp.maximum(
        jnp.sum(e, axis=-1, keepdims=True) + e_new, 1e-30)
    ow = half_of_group(ow)
    seq_lane = lax.broadcasted_iota(I32, winnewt_ref.shape, 1)
    new_col = jnp.sum(jnp.where(seq_lane == seq, winnewt_ref[...], 0.0), axis=-1, keepdims=True)
    out_lane = lax.broadcasted_iota(I32, xw.shape, 1)
    wout_ref[u] = jnp.where(out_lane == wb_rows - 1, new_col, pltpu.roll(xw, wb_rows - 1, axis=1))

    mq = mq_ref[u].astype(BF16)
    n_mem = mem_ref.shape[1] // (2 * MEM_HEADS)
    omem = jnp.zeros((NSA_HEADS, MEM_HD), F32)
    for h in range(MEM_HEADS):
        kh = mem_ref[u, pl.ds(h, n_mem, stride=2 * MEM_HEADS), :].astype(BF16)
        vh = mem_ref[u, pl.ds(MEM_HEADS + h, n_mem, stride=2 * MEM_HEADS), :].astype(BF16)
        sm = _dot_nt(mq, kh) * (MEM_HD ** -0.5)
        em = jnp.exp(sm - jnp.max(sm, axis=-1, keepdims=True))
        pm = em / jnp.sum(em, axis=-1, keepdims=True)
        omem = jnp.where(hrow == h, _dot(pm.astype(BF16), vh), omem)
    omem_ref[u] = omem

    q8 = q8_ref[u]
    q8b = q8.astype(BF16)
    sn = selnew_ref[u]
    ncols = n_top * page_rows
    col = lax.broadcasted_iota(I32, (1, ncols), 1)
    idxe = _dot(idxv_ref[u].astype(BF16), ex_ref[...])
    osel = jnp.zeros((NSA_HEADS, NSA_HD), F32)
    for g in range(NSA_KV):
        kt = buf[slot, u, g, 0].astype(BF16)
        vt = buf[slot, u, g, 1].astype(BF16)
        blk = idxe[g:g + 1, :].astype(I32)
        in_page = col % page_rows
        spos = blk * SEL_BLK + in_page % SEL_BLK
        mask = (blk < cur) & (in_page // SEL_BLK == blk % blocks_pp)
        sg = _dot(q8b, kt) + _row_bias(past - spos, tab_ref)
        s_new = jnp.sum(q8 * sn[:, g * NSA_HD:(g + 1) * NSA_HD], axis=-1, keepdims=True) + tab0
        new_sel = jnp.max(jnp.where(blk == cur, 1.0, 0.0), axis=-1, keepdims=True) > 0.5
        maskb = jnp.broadcast_to(mask, sg.shape)
        zm = jnp.where(maskb, sg, NEG)
        z_new = jnp.where(new_sel, s_new, NEG)
        m = jnp.maximum(jnp.max(zm, axis=-1, keepdims=True), z_new)
        e = jnp.exp(zm - m) * maskb.astype(F32)
        e_new = jnp.exp(z_new - m) * new_sel.astype(F32)
        v_new = sn[:, (NSA_KV + g) * NSA_HD:(NSA_KV + g + 1) * NSA_HD]
        og = (_dot_nt(e.astype(BF16), vt) + e_new * v_new) / jnp.maximum(
            jnp.sum(e, axis=-1, keepdims=True) + e_new, 1e-30)
        osel = jnp.where(hgrp == g, og, osel)

    gates = gate_ref[u]
    onsa_ref[u] = gates[:, 0:1] * half_of_group(oc_ref[u]) + gates[:, 1:2] * osel + gates[:, 2:3] * ow


def _sample_attn(table, idx2d, pool, q8, qz, idxv, ex, oc, gate8, selnew, winnew, winnewt, win, mq8, mem, tab8, past,
                 n_top):
    nb = q8.shape[0]
    sb = math.gcd(nb, _ATTN_BATCH)
    page_rows = pool.shape[4]
    blk3 = lambda a: pl.BlockSpec((sb,) + a.shape[1:], lambda i, pt, ix: (i, 0, 0))
    whole = lambda a: pl.BlockSpec(a.shape, lambda i, pt, ix: (0, 0))
    gs = pltpu.PrefetchScalarGridSpec(
        num_scalar_prefetch=2,
        grid=(nb // sb,),
        in_specs=[pl.BlockSpec(memory_space=pl.ANY), blk3(q8), blk3(qz), blk3(idxv), whole(ex), blk3(oc), blk3(gate8),
                  blk3(selnew), blk3(winnew), whole(winnewt), blk3(win), blk3(mq8), blk3(mem), whole(tab8)],
        out_specs=[pl.BlockSpec((sb, NSA_HEADS, NSA_HD), lambda i, pt, ix: (i, 0, 0)),
                   pl.BlockSpec((sb, NSA_HEADS, MEM_HD), lambda i, pt, ix: (i, 0, 0)),
                   pl.BlockSpec((sb,) + win.shape[1:], lambda i, pt, ix: (i, 0, 0))],
        scratch_shapes=[pltpu.VMEM((2, sb, NSA_KV, 2, NSA_HD, n_top * page_rows), F32),
                        pltpu.SemaphoreType.DMA((2,))],
    )
    return pl.pallas_call(
        functools.partial(_sample_attn_kernel, past=past),
        grid_spec=gs,
        out_shape=[jax.ShapeDtypeStruct((nb, NSA_HEADS, NSA_HD), F32),
                   jax.ShapeDtypeStruct((nb, NSA_HEADS, MEM_HD), F32),
                   jax.ShapeDtypeStruct(win.shape, F32)],
        compiler_params=_cparams(("arbitrary",)),
    )(table, idx2d, pool, q8, qz, idxv, ex, oc, gate8, selnew, winnew, winnewt, win, mq8, mem, tab8)


def _merge_ffn_kernel(x_ref, ol_ref, on_ref, om_ref, mg_ref, wa_ref, wb_ref, wm_ref, wo_ref, gf_ref, wg_ref,
                      wu_ref, wd_ref, y_ref):
    d = x_ref.shape[1]
    mg = mg_ref[...]
    z = (mg[:, 0:d] * _dot(ol_ref[...], wa_ref[...]) + mg[:, d:2 * d] * _dot(on_ref[...], wb_ref[...])
         + mg[:, 2 * d:3 * d] * _dot(om_ref[...], wm_ref[...]))
    h = x_ref[...] + _dot(z.astype(BF16), wo_ref[...])
    f = (h * lax.rsqrt(jnp.mean(h * h, axis=-1, keepdims=True) + EPS) * gf_ref[...]).astype(BF16)
    a = jax.nn.silu(_dot(f, wg_ref[...])) * _dot(f, wu_ref[...])
    y_ref[...] = h + _dot(a.astype(BF16), wd_ref[...])


def _merge_ffn(x2d, ol, on, om, mg, prm, tm):
    n, d = x2d.shape
    row = lambda a: pl.BlockSpec((tm, a.shape[1]), lambda i: (i, 0))
    consts = [prm['w_up_a'], prm['w_up_b'], prm['w_up_m'], prm['w_o'], prm['norm_ffn'], prm['w_ffn_gate'],
              prm['w_ffn_up'], prm['w_ffn_down']]
    acts = [x2d, ol, on, om, mg]
    return pl.pallas_call(
        _merge_ffn_kernel,
        grid=(n // tm,),
        in_specs=[row(a) for a in acts] + [pl.BlockSpec(c.shape, lambda i: (0, 0), pipeline_mode=pl.Buffered(1))
                                           for c in consts],
        out_specs=pl.BlockSpec((tm, d), lambda i: (i, 0)),
        out_shape=jax.ShapeDtypeStruct((n, d), F32),
        compiler_params=_cparams(("parallel",)),
    )(*acts, *consts)


def _round_up(n, m):
    return -(-n // m) * m


def _overlap(ncp, n_cmp, nsp):
    cs = jnp.arange(ncp)[:, None] * CMP_STRIDE
    ss = jnp.arange(nsp)[None, :] * SEL_BLK
    hit = (cs < ss + SEL_BLK) & (cs + CMP_LEN > ss) & (jnp.arange(ncp)[:, None] < n_cmp)
    return hit.astype(BF16)


def _prep_params(norm_mix, w_in, conv_w, conv_b, w_lru_a, b_lru_a, w_lru_i, b_lru_i, lru_lambda, g_nsa_q, g_nsa_k,
                 pe_cmp_k, w_cmp_k1, w_cmp_k2, pe_cmp_v, w_cmp_v1, w_cmp_v2, norm_mem, w_mem_kv, g_mem_q, g_mem_k,
                 w_up_a, w_up_b, w_up_m, w_o, norm_ffn, w_ffn_gate, w_ffn_up, w_ffn_down):
    row = lambda v: v.reshape(1, -1).astype(F32)
    n_gg = N_BRANCH * NSA_GRP
    w = w_in
    zpad = jnp.zeros((w.shape[0], LANES - n_gg), w.dtype)
    ng0 = _OFF_NG
    w_packed = jnp.concatenate(
        [w[:, :ng0], w[:, ng0:ng0 + n_gg], zpad, w[:, ng0 + n_gg:ng0 + 2 * n_gg], zpad, w[:, ng0 + 2 * n_gg:]],
        axis=1).astype(BF16)
    eye = jnp.eye(NSA_KV, dtype=F32)

    def w1_big(w1):
        wr = w1.reshape(2, CMP_STRIDE, NSA_HD, CMP_HID)
        big = jnp.einsum('hrdj,gk->rgdhkj', wr, eye)
        return big.reshape(CMP_STRIDE * NSA_KV * NSA_HD, 2 * NSA_KV * CMP_HID).astype(BF16)

    def w2_bd(w2):
        return jnp.einsum('jd,gk->gjkd', w2, eye).reshape(NSA_KV * CMP_HID, NSA_KV * NSA_HD).astype(BF16)

    pe8 = lambda pe: jnp.broadcast_to(pe.reshape(1, -1), (8, pe.size)).astype(BF16)
    return dict(
        norm_mix=row(norm_mix), w_in=w_packed,
        bd_q=_block_diag_ones(NSA_HEADS * NSA_HD, NSA_HD), bd_k=_block_diag_ones(NSA_KV * NSA_HD, NSA_HD),
        bd_m=_block_diag_ones(MEM_HEADS * MEM_HD, MEM_HD),
        gq=row(jnp.tile(g_nsa_q, NSA_HEADS)), gk_cmp=row(jnp.tile(g_nsa_k[0], NSA_KV)),
        gk_sel=row(jnp.tile(g_nsa_k[1], NSA_KV)), gk_win=row(jnp.tile(g_nsa_k[2], NSA_KV)),
        gmq=row(jnp.tile(g_mem_q, MEM_HEADS)), gmk=row(jnp.tile(g_mem_k, MEM_HEADS)),
        conv_w=conv_w.astype(F32), conv_b=row(conv_b), w_lru_a=w_lru_a.astype(BF16), w_lru_i=w_lru_i.astype(BF16),
        b_lru_a=row(b_lru_a), b_lru_i=row(b_lru_i), lru_lambda=row(lru_lambda),
        w1k_big=w1_big(w_cmp_k1), w1v_big=w1_big(w_cmp_v1), w1k=w_cmp_k1.astype(BF16), w1v=w_cmp_v1.astype(BF16),
        pe_k=pe8(pe_cmp_k), pe_v=pe8(pe_cmp_v), w2k_bd=w2_bd(w_cmp_k2), w2v_bd=w2_bd(w_cmp_v2),
        norm_mem=row(norm_mem), w_mem_kv=w_mem_kv.astype(BF16),
        w_up_a=w_up_a.astype(BF16), w_up_b=w_up_b.astype(BF16), w_up_m=w_up_m.astype(BF16), w_o=w_o.astype(BF16),
        norm_ffn=row(norm_ffn), w_ffn_gate=w_ffn_gate.astype(BF16), w_ffn_up=w_ffn_up.astype(BF16),
        w_ffn_down=w_ffn_down.astype(BF16))


def _feature_major(cache):
    return jnp.transpose(cache, (0, 2, 3, 4, 1))


def kernel(x_prompt, x_sample, mem_prompt, cache_cmp_kv, cache_sel_kv, page_table, cache_win_kv, cache_mem_kv,
           state_lru_h, state_conv, rel_bias, norm_mix, w_in, conv_w, conv_b, w_lru_a, b_lru_a, w_lru_i, b_lru_i,
           lru_lambda, g_nsa_q, g_nsa_k, pe_cmp_k, w_cmp_k1, w_cmp_k2, pe_cmp_v, w_cmp_v1, w_cmp_v2, norm_mem,
           w_mem_kv, g_mem_q, g_mem_k, w_up_a, w_up_b, w_up_m, w_o, norm_ffn, w_ffn_gate, w_ffn_up, w_ffn_down):
    assert norm_mix.shape[0] == 1 and x_sample.shape[1] == 1
    weights = (norm_mix, w_in, conv_w, conv_b, w_lru_a, b_lru_a, w_lru_i, b_lru_i, lru_lambda, g_nsa_q, g_nsa_k,
               pe_cmp_k, w_cmp_k1, w_cmp_k2, pe_cmp_v, w_cmp_v1, w_cmp_v2, norm_mem, w_mem_kv, g_mem_q, g_mem_k,
               w_up_a, w_up_b, w_up_m, w_o, norm_ffn, w_ffn_gate, w_ffn_up, w_ffn_down)
    prm = _prep_params(*[w[0] for w in weights])
    bsz, t, d = x_prompt.shape
    db = x_sample.shape[0]
    n_pages = page_table.shape[1]
    page_rows = cache_cmp_kv.shape[2]
    past = n_pages * page_rows
    n_sel = -(-(past + 1) // SEL_BLK)
    assert t % _FAR_TK == 0 and page_rows % LANES == 0 and n_sel <= 256

    xp2 = x_prompt.reshape(bsz * t, d)
    (lx, gg, q, cmp_p, selk, wink, mq, mg, cmpt_p, selt_p, wint_p, selvt, winvt, ngt) = _project(xp2, prm, 256, bsz)
    o_lru, h_p, cv_p = _lru_prompt(lx.reshape(bsz, t, d), gg.reshape(bsz, t, d), prm, 256)

    chunks_pp = page_rows // CMP_STRIDE
    pages_p = t // page_rows
    ncp_p = _round_up(t // CMP_STRIDE, LANES)
    pool_p = cmp_p.reshape(bsz * pages_p, chunks_pp, CHUNK_W)
    table_p = jnp.arange(bsz * pages_p, dtype=I32).reshape(bsz, pages_p)
    kc_p, vct_p = _compress(pool_p, table_p, prm, ncp_p, feature_major=False)

    nsp_p = _round_up(t // SEL_BLK, LANES)
    wb, cb = _bias_tiles(rel_bias.astype(F32), ncp_p)
    ovt_p = _overlap(ncp_p, t // CMP_STRIDE - 1, nsp_p).T
    ext_p = ((jnp.arange(t) // SEL_BLK)[:, None] == jnp.arange(nsp_p)[None, :]).astype(BF16)
    o_nsa = _nsa_prompt(q.reshape(bsz, t, -1), kc_p, vct_p, selk.reshape(bsz, t, LANES), selvt,
                        wink.reshape(bsz, t, LANES), winvt, ngt, wb, cb, ovt_p, ext_p)

    m_rows = mem_prompt.shape[1]
    mkv = _mem_kv(mem_prompt.reshape(bsz * m_rows, d), prm)
    o_mem = _mem_attn(mq.reshape(bsz, t, -1), mkv.reshape(bsz, m_rows, -1), 512)

    y_p = _merge_ffn(xp2, o_lru.reshape(bsz * t, d), o_nsa.reshape(bsz * t, -1), o_mem.reshape(bsz * t, -1), mg,
                     prm, 256)

    xs2 = x_sample.reshape(db, d)
    (lx_s, gg_s, q_s, _, _, _, mq_s, mg_s, cmpt_s, selt_s, wint_s, _, _, ngt_s) = _project(xs2, prm, db, 1)
    sel_s, win_s = selt_s[0].T, wint_s[0].T
    cv0 = jnp.transpose(state_conv[0], (1, 0, 2))
    o_lru_s, h_s, cv_s = _lru_sample(lx_s, gg_s, cv0, state_lru_h[0], prm)

    ncp_s = _round_up(past // CMP_STRIDE, LANES)
    pool_c = _feature_major(cache_cmp_kv[0]).reshape(-1, KV_W, page_rows)
    kc_s, vct_s = _compress(pool_c, page_table, prm, ncp_s, feature_major=True)

    nsp_s = _round_up(n_sel, LANES)
    n_top = min(TOP_N, n_sel)
    ov_s = _overlap(ncp_s, (past + 1) // CMP_STRIDE - 1, nsp_s)
    tab8 = jnp.zeros((NSA_HEADS, LANES), F32).at[:, :REL_BUCKETS].set(rel_bias.astype(F32).T)
    q8 = q_s.reshape(db, NSA_HEADS, NSA_HD)
    qz = jnp.concatenate([jnp.where(jnp.arange(NSA_HEADS)[None, :, None] // NSA_GRP == gi, q8, 0.0)
                          for gi in range(NSA_KV)], axis=2)
    oc_s, idx = _sample_score(qz, kc_s, vct_s, tab8, ov_s, past, n_sel)
    idx2d = idx[:, :NSA_KV, :n_top].astype(I32).reshape(db, NSA_KV * n_top)

    n_gg = N_BRANCH * NSA_GRP
    gate8 = jnp.concatenate([ngt_s[0, gi * LANES:gi * LANES + n_gg, :] for gi in range(NSA_KV)], axis=0).T
    gate8 = jnp.pad(gate8.reshape(db, NSA_HEADS, N_BRANCH), ((0, 0), (0, 0), (0, LANES - N_BRANCH)))
    mq8 = jnp.pad(mq_s.astype(F32).reshape(db, MEM_HEADS, MEM_HD), ((0, 0), (0, NSA_HEADS - MEM_HEADS), (0, 0)))
    ex_s = (jnp.arange(LANES)[:, None] == (jnp.arange(n_top * page_rows) // page_rows)[None, :]).astype(BF16)
    pool_s = _feature_major(cache_sel_kv[0])
    win_t = _feature_major(cache_win_kv[0])
    wb_rows = win_t.shape[-1]
    mem_s = cache_mem_kv[0]
    n_mem = mem_s.shape[1]
    o_nsa8, o_mem8, win_new = _sample_attn(
        page_table, idx2d, pool_s, q8, qz, idx, ex_s, oc_s, gate8, sel_s.reshape(db, 1, KV_W),
        win_s.reshape(db, 1, KV_W), win_s.T, win_t.reshape(db, KV_W, wb_rows), mq8,
        mem_s.reshape(db, n_mem * 2 * MEM_HEADS, MEM_HD), tab8, past, n_top)
    o_nsa_s = o_nsa8.reshape(db, NSA_HEADS * NSA_HD)
    o_mem_s = o_mem8[:, :MEM_HEADS].reshape(db, MEM_HEADS * MEM_HD)
    y_s = _merge_ffn(xs2, o_lru_s, o_nsa_s.astype(BF16), o_mem_s.astype(BF16), mg_s, prm, db)

    def rows_major(a):
        n, _, rows = a.shape
        return jnp.transpose(a.reshape(n, 2, NSA_KV, NSA_HD, rows), (0, 4, 1, 2, 3))[None]

    def new_rows(a):
        return jnp.transpose(a.reshape(2, NSA_KV, NSA_HD, db), (3, 0, 1, 2))[None, :, None]

    w_keep = min(WINDOW, t)
    return (y_p.reshape(bsz, t, d), y_s.reshape(db, 1, d),
            rows_major(cmpt_p), new_rows(cmpt_s), rows_major(selt_p), new_rows(selt_s),
            rows_major(wint_p[:, :, t - w_keep:]), rows_major(win_new),
            mkv.reshape(1, bsz, m_rows, 2, MEM_HEADS, MEM_HD),
            h_p.reshape(1, bsz, d), h_s.reshape(1, db, d),
            cv_p.reshape(1, bsz, CONV_W - 1, d), jnp.transpose(cv_s, (1, 0, 2))[None])
```

```python
import functools
import math

import jax
import jax.numpy as jnp
import numpy as np
from jax import lax
from jax.experimental import pallas as pl
from jax.experimental.pallas import tpu as pltpu

F32 = jnp.float32
BF16 = jnp.bfloat16
I32 = jnp.int32

EPS = 1e-6
NEG = -1e30
FORCE_SCORE = 1e6
LOG2E = math.log2(math.e)
LRU_C = 8.0
LRU_BLOCKS = 4
CONV_W = 4
NSA_HEADS = 8
NSA_KV = 2
NSA_GRP = NSA_HEADS // NSA_KV
NSA_HD = 64
CMP_STRIDE = 16
CMP_LEN = 2 * CMP_STRIDE
CMP_HID = 2 * NSA_HD
SEL_BLK = 64
TOP_N = 16
WINDOW = 512
Q_BLOCK = 128
MEM_HEADS = 4
MEM_HD = 128
REL_BUCKETS = 32
REL_MAX_DIST = 128
N_BRANCH = 3

LANES = 128
KV_W = 2 * NSA_KV * NSA_HD
CHUNK_W = CMP_STRIDE * KV_W
VMEM_LIMIT = 56 * 1024 * 1024


def _cparams(sem):
    return pltpu.CompilerParams(dimension_semantics=sem, vmem_limit_bytes=VMEM_LIMIT)


def _full(shape):
    n = len(shape)
    return pl.BlockSpec(shape, lambda *_: (0,) * n)


def _dot(a, b):
    return jnp.dot(a, b, preferred_element_type=F32)


def _dot_nt(a, b):
    return lax.dot_general(a, b, (((1,), (1,)), ((), ())), preferred_element_type=F32)


def _group_rms(z, ones_bd, gain, width):
    ss = _dot((z * z).astype(BF16), ones_bd)
    return z * lax.rsqrt(ss * (1.0 / width) + EPS) * gain


def _masked_softmax(z, mask):
    zm = jnp.where(mask, z, NEG)
    e = jnp.exp(zm - jnp.max(zm, axis=-1, keepdims=True)) * mask.astype(F32)
    return e / jnp.maximum(jnp.sum(e, axis=-1, keepdims=True), 1e-30)


def _t5_thresholds():
    exact = REL_BUCKETS // 2
    n = np.arange(1, 2 * REL_MAX_DIST + 1, dtype=np.float32)
    scaled = np.log(n / np.float32(exact)) / np.float32(math.log(REL_MAX_DIST / exact)) * np.float32(REL_BUCKETS - exact)
    bucket = np.minimum(exact + scaled.astype(np.int32), REL_BUCKETS - 1)
    assert np.all(np.diff(bucket[exact - 1:]) >= 0) and bucket[-1] == REL_BUCKETS - 1
    return [int(np.argmax(bucket >= b)) + 1 for b in range(exact + 1, REL_BUCKETS)]


_T5_THRESHOLDS = _t5_thresholds()


def _t5_bucket(dist):
    n = jnp.maximum(dist, 0)
    exact = REL_BUCKETS // 2
    large = exact + sum((n >= t).astype(I32) for t in _T5_THRESHOLDS)
    return jnp.where(n < exact, n, large)


def _block_diag_ones(n, width):
    i = jnp.arange(n) // width
    return (i[:, None] == i[None, :]).astype(BF16)


def _split_bf16(x):
    hi = x.astype(BF16)
    return hi, (x - hi.astype(F32)).astype(BF16)


_D = 1024
_NG_W = NSA_KV * LANES
_OFF_LX, _OFF_LG, _OFF_Q, _OFF_KV, _OFF_NG, _OFF_MQ, _OFF_MG, _OFF_END = (
    0, 1024, 2048, 2560, 3328, 3584, 4096, 7168)


def _proj_kernel(x_ref, g_ref, w_ref, bdq_ref, bdk_ref, bdm_ref, gq_ref, gks_ref, gkw_ref, gmq_ref,
                 lx_ref, lg_ref, q_ref, cmp_ref, selk_ref, wink_ref, mq_ref, mg_ref,
                 cmpt_ref, selt_ref, wint_ref, selvt_ref, winvt_ref, ngt_ref):
    x = x_ref[...]
    xn = (x * lax.rsqrt(jnp.mean(x * x, axis=-1, keepdims=True) + EPS) * g_ref[...]).astype(BF16)

    def seg(a, b):
        return _dot(xn, w_ref[:, a:b])

    lx_ref[...] = seg(_OFF_LX, _OFF_LG)
    lg_ref[...] = jax.nn.gelu(seg(_OFF_LG, _OFF_Q))
    zq = seg(_OFF_Q, _OFF_KV)
    q_ref[...] = _group_rms(zq, bdq_ref[...], gq_ref[...], NSA_HD) * (NSA_HD ** -0.5)
    zc = seg(_OFF_KV, _OFF_KV + KV_W)
    cmp_ref[...] = zc
    cmpt_ref[0] = zc.T
    for off, gain_ref, t_ref, k_ref, vt_ref in ((_OFF_KV + KV_W, gks_ref, selt_ref, selk_ref, selvt_ref),
                                                (_OFF_KV + 2 * KV_W, gkw_ref, wint_ref, wink_ref, winvt_ref)):
        z = seg(off, off + KV_W)
        kn = _group_rms(z[:, :LANES], bdk_ref[...], gain_ref[...], NSA_HD)
        vt = z[:, LANES:].T
        t_ref[0, :LANES, :] = kn.T
        t_ref[0, LANES:, :] = vt
        k_ref[...] = kn.astype(BF16)
        vt_ref[0] = vt.astype(BF16)
    ngt_ref[0] = jax.nn.sigmoid(seg(_OFF_NG, _OFF_MQ)).T
    zm = seg(_OFF_MQ, _OFF_MG)
    mq_ref[...] = _group_rms(zm, bdm_ref[...], gmq_ref[...], MEM_HD).astype(BF16)
    mg_ref[...] = jax.nn.sigmoid(seg(_OFF_MG, _OFF_END))


def _project(x2d, prm, tm, bsz):
    n = x2d.shape[0]
    t = n // bsz
    per_b = t // tm
    row = lambda w: pl.BlockSpec((tm, w), lambda i: (i, 0))
    fmaj = lambda w: pl.BlockSpec((1, w, tm), lambda i: (i // per_b, 0, i % per_b))
    consts = [prm['norm_mix'], prm['w_in'], prm['bd_q'], prm['bd_k'], prm['bd_m'],
              prm['gq'], prm['gk_sel'], prm['gk_win'], prm['gmq']]
    widths = [(_D, F32), (_D, F32), (512, F32), (KV_W, F32), (LANES, BF16), (LANES, BF16), (512, BF16), (3 * _D, F32)]
    fwidths = [(KV_W, F32), (KV_W, F32), (KV_W, F32), (LANES, BF16), (LANES, BF16), (_NG_W, F32)]
    return pl.pallas_call(
        _proj_kernel,
        grid=(n // tm,),
        in_specs=[row(_D)] + [_full(c.shape) for c in consts],
        out_specs=[row(w) for w, _ in widths] + [fmaj(w) for w, _ in fwidths],
        out_shape=[jax.ShapeDtypeStruct((n, w), dt) for w, dt in widths]
                  + [jax.ShapeDtypeStruct((bsz, w, t), dt) for w, dt in fwidths],
        compiler_params=_cparams(("parallel",)),
    )(x2d, *consts)


def _lru_gates(xc, wa_ref, wi_ref, ba_ref, bi_ref, lam_ref):
    bw = xc.shape[1] // LRU_BLOCKS
    ra, ri = [], []
    for n in range(LRU_BLOCKS):
        xr = xc[:, n * bw:(n + 1) * bw].astype(BF16)
        ra.append(_dot(xr, wa_ref[n]))
        ri.append(_dot(xr, wi_ref[n]))
    r = jax.nn.sigmoid(jnp.concatenate(ra, axis=1) + ba_ref[...])
    i = jax.nn.sigmoid(jnp.concatenate(ri, axis=1) + bi_ref[...])
    lam = -lam_ref[...]
    softplus = jnp.maximum(lam, 0.0) + jnp.log1p(jnp.exp(-jnp.abs(lam)))
    log_a = -LRU_C * r * softplus
    a = jnp.exp(log_a)
    b = jnp.sqrt(jnp.tanh(-log_a) * (a * a + 1.0)) * (i * xc)
    return a, b


def _lru_scan(a, b):
    tt = a.shape[0]
    row = lax.broadcasted_iota(I32, a.shape, 0)
    k = 1
    while k < tt:
        keep = row >= k
        b = b + a * jnp.where(keep, pltpu.roll(b, k, axis=0), 0.0)
        a = a * jnp.where(keep, pltpu.roll(a, k, axis=0), 1.0)
        k *= 2
    return a, b


def _lru_prompt_kernel(x_ref, gg_ref, cw_ref, cb_ref, wa_ref, wi_ref, ba_ref, bi_ref, lam_ref,
                       o_ref, h_ref, cv_ref, xbuf, hcar):
    t = pl.program_id(1)
    tt = x_ref.shape[1]

    @pl.when(t == 0)
    def _():
        xbuf[0:8, :] = jnp.zeros((8, xbuf.shape[1]), F32)
        hcar[...] = jnp.zeros(hcar.shape, F32)

    x = x_ref[0]
    xbuf[8:8 + tt, :] = x
    xc = cb_ref[...] + xbuf[5:5 + tt, :] * cw_ref[0:1, :]
    for k in range(1, CONV_W):
        xc = xc + xbuf[5 + k:5 + k + tt, :] * cw_ref[k:k + 1, :]
    a, b = _lru_gates(xc, wa_ref, wi_ref, ba_ref, bi_ref, lam_ref)
    ap, hs = _lru_scan(a, b)
    h = hs + ap * hcar[...]
    o_ref[0] = (h * gg_ref[0]).astype(BF16)
    hcar[...] = h[tt - 1:tt, :]
    xbuf[0:8, :] = x[tt - 8:tt, :]

    @pl.when(t == pl.num_programs(1) - 1)
    def _():
        h_ref[0] = h[tt - 1:tt, :]
        cv_ref[0] = x[tt - (CONV_W - 1):tt, :]


def _lru_prompt(lx, gg, prm, tt):
    b, t, w = lx.shape
    blk = pl.BlockSpec((1, tt, w), lambda i, j: (i, j, 0))
    consts = [prm['conv_w'], prm['conv_b'], prm['w_lru_a'], prm['w_lru_i'], prm['b_lru_a'], prm['b_lru_i'],
              prm['lru_lambda']]
    return pl.pallas_call(
        _lru_prompt_kernel,
        grid=(b, t // tt),
        in_specs=[blk, blk] + [_full(c.shape) for c in consts],
        out_specs=[blk, pl.BlockSpec((1, 1, w), lambda i, j: (i, 0, 0)),
                   pl.BlockSpec((1, CONV_W - 1, w), lambda i, j: (i, 0, 0))],
        out_shape=[jax.ShapeDtypeStruct((b, t, w), BF16), jax.ShapeDtypeStruct((b, 1, w), F32),
                   jax.ShapeDtypeStruct((b, CONV_W - 1, w), F32)],
        scratch_shapes=[pltpu.VMEM((tt + 8, w), F32), pltpu.VMEM((1, w), F32)],
        compiler_params=_cparams(("parallel", "arbitrary")),
    )(lx, gg, *consts)


def _lru_sample_kernel(x_ref, gg_ref, cv0_ref, h0_ref, cw_ref, cb_ref, wa_ref, wi_ref, ba_ref, bi_ref, lam_ref,
                       o_ref, h_ref, cv_ref):
    x = x_ref[...]
    xc = cb_ref[...] + x * cw_ref[CONV_W - 1:CONV_W, :]
    for k in range(CONV_W - 1):
        xc = xc + cv0_ref[k] * cw_ref[k:k + 1, :]
    a, b = _lru_gates(xc, wa_ref, wi_ref, ba_ref, bi_ref, lam_ref)
    h = a * h0_ref[...] + b
    o_ref[...] = (h * gg_ref[...]).astype(BF16)
    h_ref[...] = h
    for k in range(CONV_W - 2):
        cv_ref[k] = cv0_ref[k + 1]
    cv_ref[CONV_W - 2] = x


def _lru_sample(lx, gg, cv0, h0, prm):
    n, w = lx.shape
    consts = [prm['conv_w'], prm['conv_b'], prm['w_lru_a'], prm['w_lru_i'], prm['b_lru_a'], prm['b_lru_i'],
              prm['lru_lambda']]
    args = [lx, gg, cv0, h0] + consts
    return pl.pallas_call(
        _lru_sample_kernel,
        grid=(1,),
        in_specs=[_full(a.shape) for a in args],
        out_specs=[_full((n, w)), _full((n, w)), _full((CONV_W - 1, n, w))],
        out_shape=[jax.ShapeDtypeStruct((n, w), BF16), jax.ShapeDtypeStruct((n, w), F32),
                   jax.ShapeDtypeStruct((CONV_W - 1, n, w), F32)],
        compiler_params=_cparams(("arbitrary",)),
    )(*args)


_PAGE_SECTIONS = 2


def _compress_kernel(pt_ref, pool_ref, w1k_ref, w1v_ref, w1kp_ref, w1vp_ref, pek_ref, pev_ref, w2k_ref, w2v_ref,
                     bdk_ref, gk_ref, kc_ref, vct_ref, buf, sem, *scratch, n_valid, feature_major):
    b = pl.program_id(0)
    nb = pl.num_programs(0)
    n_pages = pt_ref.shape[1]
    ncp = kc_ref.shape[1]

    def page_copy(step, slot, p):
        return pltpu.make_async_copy(pool_ref.at[pt_ref[step, p]], buf.at[slot, p], sem.at[slot])

    def fetch(step, slot):
        def body(p, c):
            page_copy(step, slot, p).start()
            return c
        lax.fori_loop(0, n_pages, body, 0)

    slot = b % 2

    @pl.when(b == 0)
    def _():
        fetch(0, 0)

    @pl.when(b + 1 < nb)
    def _():
        fetch(b + 1, 1 - slot)

    def wait_body(p, c):
        page_copy(b, slot, p).wait()
        return c

    lax.fori_loop(0, n_pages, wait_body, 0)

    n_sec = math.gcd(n_pages, _PAGE_SECTIONS)
    sec_pages = n_pages // n_sec
    if feature_major:
        zbuf, = scratch
        page_rows = buf.shape[3]
        n_ch = n_pages * page_rows // CMP_STRIDE
        sec_ch = n_ch // n_sec

        def chunk_rows(kv, sec):
            for p in range(sec * sec_pages, (sec + 1) * sec_pages):
                zbuf[kv, p * page_rows:(p + 1) * page_rows, :] = buf[slot, p, kv * LANES:(kv + 1) * LANES, :].T
            return jnp.concatenate(
                [zbuf[kv, pl.ds(sec * sec_ch * CMP_STRIDE + r, sec_ch, stride=CMP_STRIDE), :].astype(BF16)
                 for r in range(CMP_STRIDE)], axis=1)
    else:
        rows_pp = buf.shape[2]
        n_ch = n_pages * rows_pp
        sec_ch = n_ch // n_sec

        def chunk_rows(kv, sec):
            pages = slice(sec * sec_pages, (sec + 1) * sec_pages)
            return jnp.concatenate(
                [buf[slot, pages, :, r * KV_W + kv * LANES:r * KV_W + (kv + 1) * LANES]
                 .reshape(sec_ch, LANES).astype(BF16) for r in range(CMP_STRIDE)], axis=1)

    row = lax.broadcasted_iota(I32, (n_ch, 1), 0)
    keep = row < n_valid
    outs = []
    for kv, w1_ref, w1p_ref, pe_ref, w2_ref in ((0, w1k_ref, w1kp_ref, pek_ref, w2k_ref),
                                                 (1, w1v_ref, w1vp_ref, pev_ref, w2v_ref)):
        hh = jnp.concatenate([_dot(chunk_rows(kv, sec), w1_ref[...]) for sec in range(n_sec)], axis=0)
        pos = _dot(pe_ref[...], w1p_ref[...])[0:1, :]
        pos = jnp.concatenate([pos, pos], axis=1)
        nh = NSA_KV * CMP_HID
        h = hh[:, :nh] + pltpu.roll(hh[:, nh:], n_ch - 1, axis=0) + pos
        outs.append(_dot(jax.nn.gelu(h).astype(BF16), w2_ref[...]))
    kc = _group_rms(outs[0], bdk_ref[...], gk_ref[...], NSA_HD)
    kc = jnp.where(keep, kc, 0.0)
    vc = jnp.where(keep, outs[1], 0.0)
    if ncp > n_ch:
        pad = jnp.zeros((ncp - n_ch, kc.shape[1]), F32)
        kc = jnp.concatenate([kc, pad], axis=0)
        vc = jnp.concatenate([vc, pad], axis=0)
    kc_ref[0] = kc.astype(BF16)
    vct_ref[0] = vc.T.astype(BF16)


def _compress(pool, table, prm, ncp, feature_major):
    nb, n_pages = table.shape
    if feature_major:
        page_rows = pool.shape[2]
        n_ch = n_pages * page_rows // CMP_STRIDE
        scratch = [pltpu.VMEM((2, n_pages * page_rows, LANES), F32)]
    else:
        n_ch = n_pages * pool.shape[1]
        scratch = []
    consts = [prm['w1k_big'], prm['w1v_big'], prm['w1k'], prm['w1v'], prm['pe_k'], prm['pe_v'],
              prm['w2k_bd'], prm['w2v_bd'], prm['bd_k'], prm['gk_cmp']]
    out_blk = pl.BlockSpec((1, ncp, LANES), lambda i, pt: (i, 0, 0))
    gs = pltpu.PrefetchScalarGridSpec(
        num_scalar_prefetch=1,
        grid=(nb,),
        in_specs=[pl.BlockSpec(memory_space=pl.ANY)] + [pl.BlockSpec(c.shape, lambda i, pt, _n=len(c.shape): (0,) * _n)
                                                         for c in consts],
        out_specs=[out_blk, pl.BlockSpec((1, LANES, ncp), lambda i, pt: (i, 0, 0))],
        scratch_shapes=[pltpu.VMEM((2, n_pages) + pool.shape[1:], F32), pltpu.SemaphoreType.DMA((2,))] + scratch,
    )
    return pl.pallas_call(
        functools.partial(_compress_kernel, n_valid=n_ch - 1, feature_major=feature_major),
        grid_spec=gs,
        out_shape=[jax.ShapeDtypeStruct((nb, ncp, LANES), BF16), jax.ShapeDtypeStruct((nb, LANES, ncp), BF16)],
        compiler_params=_cparams(("arbitrary",)),
    )(table, pool, *consts)


_WIN_TILES = WINDOW // Q_BLOCK + 1
_QCOLS = NSA_HEADS * Q_BLOCK


def _bias_tiles_kernel(tab_ref, wb_ref, cb_ref):
    g = pl.program_id(0)
    wrows = wb_ref.shape[0]
    ncp = cb_ref.shape[0] // 2
    c_w = lax.broadcasted_iota(I32, (wrows, Q_BLOCK), 0)
    i_w = lax.broadcasted_iota(I32, (wrows, Q_BLOCK), 1)
    d_w = (_WIN_TILES - 1) * Q_BLOCK + i_w - c_w
    k_c = lax.broadcasted_iota(I32, (2 * ncp, Q_BLOCK), 0)
    i_c = lax.broadcasted_iota(I32, (2 * ncp, Q_BLOCK), 1)
    d_c = i_c - (CMP_LEN - 1) - CMP_STRIDE * (k_c - ncp)
    for d, ok, ref in ((d_w, (d_w >= 0) & (d_w < WINDOW), wb_ref), (d_c, d_c >= 0, cb_ref)):
        bucket = _t5_bucket(d)
        for r in range(NSA_GRP):
            h = g * NSA_GRP + r
            acc = jnp.zeros(d.shape, F32)
            for bk in range(REL_BUCKETS):
                acc = jnp.where(bucket == bk, tab_ref[bk, h], acc)
            far = tab_ref[REL_BUCKETS - 1, h]
            ref[:, r * Q_BLOCK:(r + 1) * Q_BLOCK] = jnp.where(ok, (acc - far) * LOG2E, NEG)


def _bias_tiles(rel_bias, ncp):
    wrows = _WIN_TILES * Q_BLOCK
    gcols = NSA_GRP * Q_BLOCK
    return pl.pallas_call(
        _bias_tiles_kernel,
        grid=(NSA_KV,),
        in_specs=[pl.BlockSpec(memory_space=pltpu.SMEM)],
        out_specs=[pl.BlockSpec((wrows, gcols), lambda g: (0, g)),
                   pl.BlockSpec((2 * ncp, gcols), lambda g: (0, g))],
        out_shape=[jax.ShapeDtypeStruct((wrows, _QCOLS), F32),
                   jax.ShapeDtypeStruct((2 * ncp, _QCOLS), F32)],
        compiler_params=_cparams(("arbitrary",)),
    )(rel_bias)


def _topk_select(score, n_top):
    lane = lax.broadcasted_iota(I32, score.shape, 1).astype(F32)
    big = float(score.shape[1])
    slot = lax.broadcasted_iota(I32, (score.shape[0], LANES), 1)
    idx = jnp.zeros((score.shape[0], LANES), F32)
    for it in range(n_top):
        m = jnp.max(score, axis=-1, keepdims=True)
        first = jnp.min(jnp.where(score == m, lane, big), axis=-1, keepdims=True)
        score = jnp.where(lane == first, -jnp.inf, score)
        idx = jnp.where(slot == it, first, idx)
    return idx


def _topk_mask_cols(score, n_top):
    sid = lax.broadcasted_iota(I32, score.shape, 0).astype(F32)
    big = float(score.shape[0])
    sel = jnp.zeros(score.shape, F32)
    for _ in range(n_top):
        m = jnp.max(score, axis=0, keepdims=True)
        first = jnp.min(jnp.where(score == m, sid, big), axis=0, keepdims=True)
        hit = sid == first
        sel = jnp.where(hit, 1.0, sel)
        score = jnp.where(hit, -jnp.inf, score)
    return sel


_FAR_TK = 1024
_FAR_SPLIT = 4
_ONES_ROWS = 16


def _nsa_prompt_kernel(q_ref, kc_ref, vct_ref, selk_ref, selvt_ref, wink_ref, winvt_ref, gate_ref, wb_ref, cb_ref,
                       ovt_ref, ext_ref, o_ref, *, n_sel):
    bi = pl.program_id(1)
    q0 = bi * Q_BLOCK
    ncp = kc_ref.shape[1]
    nsp = ovt_ref.shape[0]
    gcols = NSA_GRP * Q_BLOCK

    lane = lax.broadcasted_iota(I32, (Q_BLOCK, LANES), 1)
    qf = q_ref[0] * LOG2E
    qz = []
    for h in range(NSA_HEADS):
        g = h // NSA_GRP
        blk = qf[:, (h // 2) * LANES:(h // 2 + 1) * LANES]
        if h % 2 != g:
            blk = pltpu.roll(blk, NSA_HD, axis=1)
        qz.append(jnp.where((lane // NSA_HD) == g, blk, 0.0))
    qz = jnp.concatenate(qz, axis=0).astype(BF16)

    def col_max(z):
        while z.shape[0] > 64 and z.shape[0] % 16 == 0:
            half = z.shape[0] // 2
            z = jnp.maximum(z[:half], z[half:])
        return jnp.max(z, axis=0, keepdims=True)

    def col_sum(z):
        return jnp.sum(z, axis=0, keepdims=True)

    c0 = pl.multiple_of(ncp - bi * (Q_BLOCK // CMP_STRIDE), Q_BLOCK // CMP_STRIDE)
    z = _dot_nt(kc_ref[0], qz) + cb_ref[pl.ds(c0, ncp), :]
    m = col_max(z)
    e = jnp.exp2(z - m)
    pc = e * jnp.where(m > 0.5 * NEG, 1.0 / col_sum(e), 0.0)
    oc = _dot(vct_ref[0], pc.astype(BF16))

    pcs = []
    for g in range(NSA_KV):
        acc = pc[:, g * gcols:g * gcols + Q_BLOCK]
        for r in range(1, NSA_GRP):
            acc = acc + pc[:, g * gcols + r * Q_BLOCK:g * gcols + (r + 1) * Q_BLOCK]
        pcs.append(acc)
    hi, lo = _split_bf16(jnp.concatenate(pcs, axis=1))
    ssum = _dot(ovt_ref[...], hi) + _dot(ovt_ref[...], lo)
    s_id = lax.broadcasted_iota(I32, (nsp, NSA_KV * Q_BLOCK), 0)
    cur = (q0 + lax.broadcasted_iota(I32, (1, NSA_KV * Q_BLOCK), 1) % Q_BLOCK) // SEL_BLK
    forced = (s_id == 0) | (s_id == cur) | (s_id == cur - 1)
    score = jnp.where(forced, FORCE_SCORE, jnp.where(s_id <= cur, ssum, -1.0))
    selm = _topk_mask_cols(score, min(TOP_N, n_sel)).T

    def with_mask(sel):
        u = ((sel - 1.0) * (-NEG)).astype(BF16)
        rows = [u[g * Q_BLOCK:(g + 1) * Q_BLOCK] for g in range(NSA_KV) for _ in range(NSA_GRP)]
        return jnp.concatenate([qz, jnp.concatenate(rows, axis=0)], axis=1)

    blk_id = lax.broadcasted_iota(I32, (NSA_KV * Q_BLOCK, nsp), 1)
    qz_near = with_mask(selm)
    qz_far = with_mask(jnp.where(blk_id < 2 * bi - 2, selm, 0.0))

    p0 = pl.multiple_of(jnp.maximum(q0 - Q_BLOCK, 0), Q_BLOCK)
    d0 = pl.multiple_of(q0, Q_BLOCK)
    kn = jnp.concatenate(
        [jnp.concatenate([selk_ref[0, pl.ds(k0, Q_BLOCK), :], ext_ref[pl.ds(k0, Q_BLOCK), :]], axis=1)
         for k0 in (p0, d0)], axis=0)
    vn = jnp.concatenate([selvt_ref[0, :, pl.ds(p0, Q_BLOCK)], selvt_ref[0, :, pl.ds(d0, Q_BLOCK)]], axis=1)
    krow = lax.broadcasted_iota(I32, (2 * Q_BLOCK, 1), 0)
    wrows = wb_ref.shape[0]
    z = _dot_nt(kn, qz_near) + wb_ref[wrows - 2 * Q_BLOCK:wrows, :]
    z = z + jnp.where((krow < Q_BLOCK) & (bi == 0), NEG, 0.0)
    def with_ones(vt):
        return jnp.concatenate([vt, jnp.ones((_ONES_ROWS, vt.shape[1]), BF16)], axis=0)

    m = col_max(z)
    carry = (m, _dot(with_ones(vn), jnp.exp2((z - m).astype(BF16))))

    far_blocks = _FAR_TK // Q_BLOCK
    n_far = (bi + far_blocks - 2) // far_blocks

    sub = _FAR_TK // _FAR_SPLIT

    def far_body(t, carry):
        m, acc = carry
        starts = [pl.multiple_of(t * _FAR_TK + s * sub, sub) for s in range(_FAR_SPLIT)]
        zs = [_dot_nt(jnp.concatenate([selk_ref[0, pl.ds(k0, sub), :], ext_ref[pl.ds(k0, sub), :]], axis=1), qz_far)
              for k0 in starts]
        for k0, z in zip(starts, zs):
            m_new = jnp.maximum(m, col_max(z))
            e = jnp.exp2((z - m_new).astype(BF16))
            acc = jnp.exp2(m - m_new) * acc + _dot(with_ones(selvt_ref[0, :, pl.ds(k0, sub)]), e)
            m = m_new
        return m, acc

    _, acc = lax.fori_loop(0, n_far, far_body, carry)
    osel = acc[:LANES] / acc[LANES:LANES + 1]

    ks, vs = [], []
    for t in range(_WIN_TILES):
        st = pl.multiple_of(jnp.maximum(q0 - (_WIN_TILES - 1 - t) * Q_BLOCK, 0), Q_BLOCK)
        ks.append(wink_ref[0, pl.ds(st, Q_BLOCK), :])
        vs.append(winvt_ref[0, :, pl.ds(st, Q_BLOCK)])
    wrow = lax.broadcasted_iota(I32, (wrows, 1), 0)
    before_start = jnp.where(wrow < (_WIN_TILES - 1 - bi) * Q_BLOCK, NEG, 0.0)
    z = _dot_nt(jnp.concatenate(ks, axis=0), qz) + wb_ref[...] + before_start
    ow = _dot(with_ones(jnp.concatenate(vs, axis=1)), jnp.exp2((z - col_max(z)).astype(BF16)))
    ow = ow[:LANES] / ow[LANES:LANES + 1]

    gates = gate_ref[0]

    def gate_row(br):
        rows = [g * LANES + r * N_BRANCH + br for g in range(NSA_KV) for r in range(NSA_GRP)]
        return jnp.concatenate([gates[c:c + 1, :] for c in rows], axis=1)

    mixed = gate_row(0) * oc + gate_row(1) * osel + gate_row(2) * ow
    out = []
    for j in range(NSA_HEADS // 2):
        g = (2 * j) // NSA_GRP
        pair = jnp.concatenate([mixed[g * NSA_HD:(g + 1) * NSA_HD, (2 * j) * Q_BLOCK:(2 * j + 1) * Q_BLOCK],
                                mixed[g * NSA_HD:(g + 1) * NSA_HD, (2 * j + 1) * Q_BLOCK:(2 * j + 2) * Q_BLOCK]],
                               axis=0)
        out.append(pair.T)
    o_ref[0] = jnp.concatenate(out, axis=1).astype(BF16)


def _nsa_prompt(q, kc, vct, selk, selvt, wink, winvt, gates_t, wb, cb, ovt, ext):
    b, t, hw = q.shape
    ncp = kc.shape[1]
    per_b = lambda shape: pl.BlockSpec((1,) + shape, lambda i, j: (i, 0, 0))
    const = lambda a: pl.BlockSpec(a.shape, lambda i, j: (0, 0), pipeline_mode=pl.Buffered(1))
    return pl.pallas_call(
        functools.partial(_nsa_prompt_kernel, n_sel=t // SEL_BLK),
        grid=(b, t // Q_BLOCK),
        in_specs=[pl.BlockSpec((1, Q_BLOCK, hw), lambda i, j: (i, j, 0)),
                  per_b((ncp, LANES)), per_b((LANES, ncp)),
                  per_b((t, LANES)), per_b((LANES, t)), per_b((t, LANES)), per_b((LANES, t)),
                  pl.BlockSpec((1, _NG_W, Q_BLOCK), lambda i, j: (i, 0, j)),
                  const(wb), const(cb), const(ovt), const(ext)],
        out_specs=pl.BlockSpec((1, Q_BLOCK, hw), lambda i, j: (i, j, 0)),
        out_shape=jax.ShapeDtypeStruct((b, t, hw), BF16),
        compiler_params=_cparams(("parallel", "parallel")),
    )(q, kc, vct, selk, selvt, wink, winvt, gates_t, wb, cb, ovt, ext)


def _mem_kv_kernel(m_ref, g_ref, w_ref, bd_ref, gk_ref, o_ref):
    x = m_ref[...]
    xn = (x * lax.rsqrt(jnp.mean(x * x, axis=-1, keepdims=True) + EPS) * g_ref[...]).astype(BF16)
    z = _dot(xn, w_ref[...])
    half = MEM_HEADS * MEM_HD
    kn = _group_rms(z[:, :half], bd_ref[...], gk_ref[...], MEM_HD)
    o_ref[...] = jnp.concatenate([kn, z[:, half:]], axis=1)


def _mem_kv(mem2d, prm):
    n = mem2d.shape[0]
    args = [mem2d, prm['norm_mem'], prm['w_mem_kv'], prm['bd_m'], prm['gmk']]
    w = 2 * MEM_HEADS * MEM_HD
    return pl.pallas_call(
        _mem_kv_kernel,
        grid=(1,),
        in_specs=[_full(a.shape) for a in args],
        out_specs=_full((n, w)),
        out_shape=jax.ShapeDtypeStruct((n, w), F32),
        compiler_params=_cparams(("arbitrary",)),
    )(*args)


def _mem_attn_kernel(q_ref, kv_ref, o_ref):
    half = MEM_HEADS * MEM_HD
    out = []
    for h in range(MEM_HEADS):
        sl = slice(h * MEM_HD, (h + 1) * MEM_HD)
        k = kv_ref[0, :, sl].astype(BF16)
        v = kv_ref[0, :, half + h * MEM_HD:half + (h + 1) * MEM_HD].astype(BF16)
        s = _dot_nt(q_ref[0, :, sl], k) * (MEM_HD ** -0.5)
        e = jnp.exp(s - jnp.max(s, axis=-1, keepdims=True))
        p = e / jnp.sum(e, axis=-1, keepdims=True)
        out.append(_dot(p.astype(BF16), v))
    o_ref[0] = jnp.concatenate(out, axis=1).astype(BF16)


def _mem_attn(mq, mkv, tq):
    b, t, w = mq.shape
    m = mkv.shape[1]
    return pl.pallas_call(
        _mem_attn_kernel,
        grid=(b, t // tq),
        in_specs=[pl.BlockSpec((1, tq, w), lambda i, j: (i, j, 0)),
                  pl.BlockSpec((1, m, 2 * w), lambda i, j: (i, 0, 0))],
        out_specs=pl.BlockSpec((1, tq, w), lambda i, j: (i, j, 0)),
        out_shape=jax.ShapeDtypeStruct((b, t, w), BF16),
        compiler_params=_cparams(("parallel", "parallel")),
    )(mq, mkv)


_SCORE_BATCH = 16
_ATTN_BATCH = 2


def _row_bias(dist, tab_ref):
    bucket = _t5_bucket(dist)
    acc = jnp.zeros((NSA_HEADS, dist.shape[1]), F32)
    for bk in range(REL_BUCKETS):
        acc = jnp.where(bucket == bk, tab_ref[:, bk:bk + 1], acc)
    return acc


def _sample_score_kernel(qz_ref, kc_ref, vct_ref, tab_ref, ov_ref, oc_ref, idx_ref, *, past, n_sel):
    sb = qz_ref.shape[0]
    ncp = kc_ref.shape[1]
    nsp = ov_ref.shape[1]
    cpos = lax.broadcasted_iota(I32, (1, ncp), 1) * CMP_STRIDE + (CMP_LEN - 1)
    bias = _row_bias(past - cpos, tab_ref)
    valid = jnp.broadcast_to(cpos <= past, (NSA_HEADS, ncp))
    hrow = lax.broadcasted_iota(I32, (NSA_HEADS, 1), 0)
    pcs = []
    for j in range(sb):
        pc = _masked_softmax(_dot_nt(qz_ref[j].astype(BF16), kc_ref[j]) + bias, valid)
        oc_ref[j] = _dot_nt(pc.astype(BF16), vct_ref[j])
        grp = [jnp.sum(jnp.where(hrow // NSA_GRP == g, pc, 0.0), axis=0, keepdims=True) for g in range(NSA_KV)]
        pcs.append(jnp.concatenate(grp + [jnp.zeros((NSA_HEADS - NSA_KV, ncp), F32)], axis=0))
    hi, lo = _split_bf16(jnp.concatenate(pcs, axis=0))
    ssum = _dot(hi, ov_ref[...]) + _dot(lo, ov_ref[...])
    blk_id = lax.broadcasted_iota(I32, ssum.shape, 1)
    cur = past // SEL_BLK
    forced = (blk_id == 0) | (blk_id == cur) | (blk_id == cur - 1)
    score = jnp.where(forced, FORCE_SCORE, jnp.where(blk_id <= cur, ssum, -1.0))
    score = jnp.where(blk_id < n_sel, score, -jnp.inf)
    idx = _topk_select(score, min(TOP_N, n_sel))
    idx_ref[...] = idx.reshape(sb, NSA_HEADS, LANES)


def _sample_score(qz, kc, vct, tab8, ov, past, n_sel):
    nb = qz.shape[0]
    ncp = kc.shape[1]
    sb = math.gcd(nb, _SCORE_BATCH)
    blk8 = pl.BlockSpec((sb, NSA_HEADS, LANES), lambda i: (i, 0, 0))
    blkc = pl.BlockSpec((sb, ncp, LANES), lambda i: (i, 0, 0))
    return pl.pallas_call(
        functools.partial(_sample_score_kernel, past=past, n_sel=n_sel),
        grid=(nb // sb,),
        in_specs=[blk8, blkc, pl.BlockSpec((sb, LANES, ncp), lambda i: (i, 0, 0)), _full(tab8.shape),
                  _full(ov.shape)],
        out_specs=[blk8, blk8],
        out_shape=[jax.ShapeDtypeStruct((nb, NSA_HEADS, LANES), F32)] * 2,
        compiler_params=_cparams(("parallel",)),
    )(qz, kc, vct, tab8, ov)


def _sample_attn_kernel(pt_ref, idx_ref, pool_ref, q8_ref, qz_ref, idxv_ref, ex_ref, oc_ref, gate_ref, selnew_ref,
                        winnew_ref, winnewt_ref, win_ref, mq_ref, mem_ref, tab_ref,
                        onsa_ref, omem_ref, wout_ref, buf, sem, *, past):
    b = pl.program_id(0)
    nb = pl.num_programs(0)
    sb = buf.shape[1]
    page_rows = pool_ref.shape[4]
    n_top = buf.shape[5] // page_rows
    cur = past // SEL_BLK
    blocks_pp = page_rows // SEL_BLK
    copies = [(u, g, n) for u in range(sb) for g in range(NSA_KV) for n in range(n_top)]

    def blk_copy(step, slot, u, g, n):
        seq = step * sb + u
        blk = jnp.minimum(idx_ref[seq, g * n_top + n], cur - 1)
        page = pt_ref[seq, blk // blocks_pp]
        return pltpu.make_async_copy(pool_ref.at[page, :, g],
                                     buf.at[slot, u, g, :, :, n * page_rows:(n + 1) * page_rows], sem.at[slot])

    def fetch(step, slot):
        for c in copies:
            blk_copy(step, slot, *c).start()

    slot = b % 2

    @pl.when(b == 0)
    def _():
        fetch(0, 0)

    @pl.when(b + 1 < nb)
    def _():
        fetch(b + 1, 1 - slot)

    for c in copies:
        blk_copy(b, slot, *c).wait()
    for u in range(sb):
        _sample_attn_one(u, b * sb + u, slot, past, n_top, page_rows, q8_ref, qz_ref, idxv_ref, ex_ref, oc_ref,
                         gate_ref, selnew_ref, winnew_ref, winnewt_ref, win_ref, mq_ref, mem_ref, tab_ref,
                         onsa_ref, omem_ref, wout_ref, buf)


def _sample_attn_one(u, seq, slot, past, n_top, page_rows, q8_ref, qz_ref, idxv_ref, ex_ref, oc_ref, gate_ref,
                     selnew_ref, winnew_ref, winnewt_ref, win_ref, mq_ref, mem_ref, tab_ref,
                     onsa_ref, omem_ref, wout_ref, buf):
    cur = past // SEL_BLK
    blocks_pp = page_rows // SEL_BLK
    hrow = lax.broadcasted_iota(I32, (NSA_HEADS, 1), 0)
    hgrp = hrow // NSA_GRP
    tab0 = tab_ref[:, 0:1]

    def half_of_group(x):
        return jnp.where(hgrp == 0, x[:, :NSA_HD], x[:, NSA_HD:])

    qz = qz_ref[u]
    xw = win_ref[u]
    wb_rows = xw.shape[1]
    wn = winnew_ref[u]
    j = lax.broadcasted_iota(I32, (1, wb_rows), 1)
    dw = wb_rows - j
    s = _dot(qz.astype(BF16), xw[:LANES].astype(BF16)) + _row_bias(dw, tab_ref)
    s_new = jnp.sum(qz * wn[:, :LANES], axis=-1, keepdims=True) + tab0
    mask = jnp.broadcast_to(dw < WINDOW, s.shape)
    zm = jnp.where(mask, s, NEG)
    m = jnp.maximum(jnp.max(zm, axis=-1, keepdims=True), s_new)
    e = jnp.exp(zm - m) * mask.astype(F32)
    e_new = jnp.exp(s_new - m)
    ow = (_dot_nt(e.astype(BF16), xw[LANES:].astype(BF16)) + e_new * wn[:, LANES:]) / jnp.maximum(
        jnp.sum(e, axis=-1, keepdims=True) + e_new, 1e-30)
    ow = half_of_group(ow)
    seq_lane = lax.broadcasted_iota(I32, winnewt_ref.shape, 1)
    new_col = jnp.sum(jnp.where(seq_lane == seq, winnewt_ref[...], 0.0), axis=-1, keepdims=True)
    out_lane = lax.broadcasted_iota(I32, xw.shape, 1)
    wout_ref[u] = jnp.where(out_lane == wb_rows - 1, new_col, pltpu.roll(xw, wb_rows - 1, axis=1))

    mq = mq_ref[u].astype(BF16)
    n_mem = mem_ref.shape[1] // (2 * MEM_HEADS)
    omem = jnp.zeros((NSA_HEADS, MEM_HD), F32)
    for h in range(MEM_HEADS):
        kh = mem_ref[u, pl.ds(h, n_mem, stride=2 * MEM_HEADS), :].astype(BF16)
        vh = mem_ref[u, pl.ds(MEM_HEADS + h, n_mem, stride=2 * MEM_HEADS), :].astype(BF16)
        sm = _dot_nt(mq, kh) * (MEM_HD ** -0.5)
        em = jnp.exp(sm - jnp.max(sm, axis=-1, keepdims=True))
        pm = em / jnp.sum(em, axis=-1, keepdims=True)
        omem = jnp.where(hrow == h, _dot(pm.astype(BF16), vh), omem)
    omem_ref[u] = omem

    q8 = q8_ref[u]
    q8b = q8.astype(BF16)
    sn = selnew_ref[u]
    ncols = n_top * page_rows
    col = lax.broadcasted_iota(I32, (1, ncols), 1)
    idxe = _dot(idxv_ref[u].astype(BF16), ex_ref[...])
    osel = jnp.zeros((NSA_HEADS, NSA_HD), F32)
    for g in range(NSA_KV):
        kt = buf[slot, u, g, 0].astype(BF16)
        vt = buf[slot, u, g, 1].astype(BF16)
        blk = idxe[g:g + 1, :].astype(I32)
        in_page = col % page_rows
        spos = blk * SEL_BLK + in_page % SEL_BLK
        mask = (blk < cur) & (in_page // SEL_BLK == blk % blocks_pp)
        sg = _dot(q8b, kt) + _row_bias(past - spos, tab_ref)
        s_new = jnp.sum(q8 * sn[:, g * NSA_HD:(g + 1) * NSA_HD], axis=-1, keepdims=True) + tab0
        new_sel = jnp.max(jnp.where(blk == cur, 1.0, 0.0), axis=-1, keepdims=True) > 0.5
        maskb = jnp.broadcast_to(mask, sg.shape)
        zm = jnp.where(maskb, sg, NEG)
        z_new = jnp.where(new_sel, s_new, NEG)
        m = jnp.maximum(jnp.max(zm, axis=-1, keepdims=True), z_new)
        e = jnp.exp(zm - m) * maskb.astype(F32)
        e_new = jnp.exp(z_new - m) * new_sel.astype(F32)
        v_new = sn[:, (NSA_KV + g) * NSA_HD:(NSA_KV + g + 1) * NSA_HD]
        og = (_dot_nt(e.astype(BF16), vt) + e_new * v_new) / jnp.maximum(
            jnp.sum(e, axis=-1, keepdims=True) + e_new, 1e-30)
        osel = jnp.where(hgrp == g, og, osel)

    gates = gate_ref[u]
    onsa_ref[u] = gates[:, 0:1] * half_of_group(oc_ref[u]) + gates[:, 1:2] * osel + gates[:, 2:3] * ow


def _sample_attn(table, idx2d, pool, q8, qz, idxv, ex, oc, gate8, selnew, winnew, winnewt, win, mq8, mem, tab8, past,
                 n_top):
    nb = q8.shape[0]
    sb = math.gcd(nb, _ATTN_BATCH)
    page_rows = pool.shape[4]
    blk3 = lambda a: pl.BlockSpec((sb,) + a.shape[1:], lambda i, pt, ix: (i, 0, 0))
    whole = lambda a: pl.BlockSpec(a.shape, lambda i, pt, ix: (0, 0))
    gs = pltpu.PrefetchScalarGridSpec(
        num_scalar_prefetch=2,
        grid=(nb // sb,),
        in_specs=[pl.BlockSpec(memory_space=pl.ANY), blk3(q8), blk3(qz), blk3(idxv), whole(ex), blk3(oc), blk3(gate8),
                  blk3(selnew), blk3(winnew), whole(winnewt), blk3(win), blk3(mq8), blk3(mem), whole(tab8)],
        out_specs=[pl.BlockSpec((sb, NSA_HEADS, NSA_HD), lambda i, pt, ix: (i, 0, 0)),
                   pl.BlockSpec((sb, NSA_HEADS, MEM_HD), lambda i, pt, ix: (i, 0, 0)),
                   pl.BlockSpec((sb,) + win.shape[1:], lambda i, pt, ix: (i, 0, 0))],
        scratch_shapes=[pltpu.VMEM((2, sb, NSA_KV, 2, NSA_HD, n_top * page_rows), F32),
                        pltpu.SemaphoreType.DMA((2,))],
    )
    return pl.pallas_call(
        functools.partial(_sample_attn_kernel, past=past),
        grid_spec=gs,
        out_shape=[jax.ShapeDtypeStruct((nb, NSA_HEADS, NSA_HD), F32),
                   jax.ShapeDtypeStruct((nb, NSA_HEADS, MEM_HD), F32),
                   jax.ShapeDtypeStruct(win.shape, F32)],
        compiler_params=_cparams(("arbitrary",)),
    )(table, idx2d, pool, q8, qz, idxv, ex, oc, gate8, selnew, winnew, winnewt, win, mq8, mem, tab8)


def _merge_ffn_kernel(x_ref, ol_ref, on_ref, om_ref, mg_ref, *rest, mem_fused):
    if mem_fused:
        mkv_ref, rest = rest[0], rest[1:]
        half = MEM_HEADS * MEM_HD
        heads = []
        for h in range(MEM_HEADS):
            sl = slice(h * MEM_HD, (h + 1) * MEM_HD)
            k = mkv_ref[0, :, sl].astype(BF16)
            v = mkv_ref[0, :, half + h * MEM_HD:half + (h + 1) * MEM_HD].astype(BF16)
            s = _dot_nt(om_ref[:, sl], k) * (MEM_HD ** -0.5)
            e = jnp.exp(s - jnp.max(s, axis=-1, keepdims=True))
            heads.append(_dot((e / jnp.sum(e, axis=-1, keepdims=True)).astype(BF16), v))
        om = jnp.concatenate(heads, axis=1).astype(BF16)
    else:
        om = om_ref[...]
    wa_ref, wb_ref, wm_ref, wo_ref, gf_ref, wg_ref, wu_ref, wd_ref, y_ref = rest
    d = x_ref.shape[1]
    mg = mg_ref[...]
    z = (mg[:, 0:d] * _dot(ol_ref[...], wa_ref[...]) + mg[:, d:2 * d] * _dot(on_ref[...], wb_ref[...])
         + mg[:, 2 * d:3 * d] * _dot(om, wm_ref[...]))
    h = x_ref[...] + _dot(z.astype(BF16), wo_ref[...])
    f = (h * lax.rsqrt(jnp.mean(h * h, axis=-1, keepdims=True) + EPS) * gf_ref[...]).astype(BF16)
    a = jax.nn.silu(_dot(f, wg_ref[...])) * _dot(f, wu_ref[...])
    y_ref[...] = h + _dot(a.astype(BF16), wd_ref[...])


def _merge_ffn(x2d, ol, on, om, mg, prm, tm, mem_kv=None):
    n, d = x2d.shape
    row = lambda a: pl.BlockSpec((tm, a.shape[1]), lambda i: (i, 0))
    consts = [prm['w_up_a'], prm['w_up_b'], prm['w_up_m'], prm['w_o'], prm['norm_ffn'], prm['w_ffn_gate'],
              prm['w_ffn_up'], prm['w_ffn_down']]
    acts = [x2d, ol, on, om, mg]
    extra, extra_specs = [], []
    if mem_kv is not None:
        per_b = n // mem_kv.shape[0] // tm
        extra = [mem_kv]
        extra_specs = [pl.BlockSpec((1,) + mem_kv.shape[1:], lambda i: (i // per_b, 0, 0))]
    return pl.pallas_call(
        functools.partial(_merge_ffn_kernel, mem_fused=mem_kv is not None),
        grid=(n // tm,),
        in_specs=[row(a) for a in acts] + extra_specs
                 + [pl.BlockSpec(c.shape, lambda i: (0, 0), pipeline_mode=pl.Buffered(1)) for c in consts],
        out_specs=pl.BlockSpec((tm, d), lambda i: (i, 0)),
        out_shape=jax.ShapeDtypeStruct((n, d), F32),
        compiler_params=_cparams(("parallel",)),
    )(*acts, *extra, *consts)


def _round_up(n, m):
    return -(-n // m) * m


def _overlap(ncp, n_cmp, nsp):
    cs = jnp.arange(ncp)[:, None] * CMP_STRIDE
    ss = jnp.arange(nsp)[None, :] * SEL_BLK
    hit = (cs < ss + SEL_BLK) & (cs + CMP_LEN > ss) & (jnp.arange(ncp)[:, None] < n_cmp)
    return hit.astype(BF16)


def _prep_params(norm_mix, w_in, conv_w, conv_b, w_lru_a, b_lru_a, w_lru_i, b_lru_i, lru_lambda, g_nsa_q, g_nsa_k,
                 pe_cmp_k, w_cmp_k1, w_cmp_k2, pe_cmp_v, w_cmp_v1, w_cmp_v2, norm_mem, w_mem_kv, g_mem_q, g_mem_k,
                 w_up_a, w_up_b, w_up_m, w_o, norm_ffn, w_ffn_gate, w_ffn_up, w_ffn_down):
    row = lambda v: v.reshape(1, -1).astype(F32)
    n_gg = N_BRANCH * NSA_GRP
    w = w_in
    zpad = jnp.zeros((w.shape[0], LANES - n_gg), w.dtype)
    ng0 = _OFF_NG
    w_packed = jnp.concatenate(
        [w[:, :ng0], w[:, ng0:ng0 + n_gg], zpad, w[:, ng0 + n_gg:ng0 + 2 * n_gg], zpad, w[:, ng0 + 2 * n_gg:]],
        axis=1).astype(BF16)
    eye = jnp.eye(NSA_KV, dtype=F32)

    def w1_big(w1):
        wr = w1.reshape(2, CMP_STRIDE, NSA_HD, CMP_HID)
        big = jnp.einsum('hrdj,gk->rgdhkj', wr, eye)
        return big.reshape(CMP_STRIDE * NSA_KV * NSA_HD, 2 * NSA_KV * CMP_HID).astype(BF16)

    def w2_bd(w2):
        return jnp.einsum('jd,gk->gjkd', w2, eye).reshape(NSA_KV * CMP_HID, NSA_KV * NSA_HD).astype(BF16)

    pe8 = lambda pe: jnp.broadcast_to(pe.reshape(1, -1), (8, pe.size)).astype(BF16)
    return dict(
        norm_mix=row(norm_mix), w_in=w_packed,
        bd_q=_block_diag_ones(NSA_HEADS * NSA_HD, NSA_HD), bd_k=_block_diag_ones(NSA_KV * NSA_HD, NSA_HD),
        bd_m=_block_diag_ones(MEM_HEADS * MEM_HD, MEM_HD),
        gq=row(jnp.tile(g_nsa_q, NSA_HEADS)), gk_cmp=row(jnp.tile(g_nsa_k[0], NSA_KV)),
        gk_sel=row(jnp.tile(g_nsa_k[1], NSA_KV)), gk_win=row(jnp.tile(g_nsa_k[2], NSA_KV)),
        gmq=row(jnp.tile(g_mem_q, MEM_HEADS)), gmk=row(jnp.tile(g_mem_k, MEM_HEADS)),
        conv_w=conv_w.astype(F32), conv_b=row(conv_b), w_lru_a=w_lru_a.astype(BF16), w_lru_i=w_lru_i.astype(BF16),
        b_lru_a=row(b_lru_a), b_lru_i=row(b_lru_i), lru_lambda=row(lru_lambda),
        w1k_big=w1_big(w_cmp_k1), w1v_big=w1_big(w_cmp_v1), w1k=w_cmp_k1.astype(BF16), w1v=w_cmp_v1.astype(BF16),
        pe_k=pe8(pe_cmp_k), pe_v=pe8(pe_cmp_v), w2k_bd=w2_bd(w_cmp_k2), w2v_bd=w2_bd(w_cmp_v2),
        norm_mem=row(norm_mem), w_mem_kv=w_mem_kv.astype(BF16),
        w_up_a=w_up_a.astype(BF16), w_up_b=w_up_b.astype(BF16), w_up_m=w_up_m.astype(BF16), w_o=w_o.astype(BF16),
        norm_ffn=row(norm_ffn), w_ffn_gate=w_ffn_gate.astype(BF16), w_ffn_up=w_ffn_up.astype(BF16),
        w_ffn_down=w_ffn_down.astype(BF16))


def _feature_major(cache):
    return jnp.transpose(cache, (0, 2, 3, 4, 1))


def kernel(x_prompt, x_sample, mem_prompt, cache_cmp_kv, cache_sel_kv, page_table, cache_win_kv, cache_mem_kv,
           state_lru_h, state_conv, rel_bias, norm_mix, w_in, conv_w, conv_b, w_lru_a, b_lru_a, w_lru_i, b_lru_i,
           lru_lambda, g_nsa_q, g_nsa_k, pe_cmp_k, w_cmp_k1, w_cmp_k2, pe_cmp_v, w_cmp_v1, w_cmp_v2, norm_mem,
           w_mem_kv, g_mem_q, g_mem_k, w_up_a, w_up_b, w_up_m, w_o, norm_ffn, w_ffn_gate, w_ffn_up, w_ffn_down):
    assert norm_mix.shape[0] == 1 and x_sample.shape[1] == 1
    weights = (norm_mix, w_in, conv_w, conv_b, w_lru_a, b_lru_a, w_lru_i, b_lru_i, lru_lambda, g_nsa_q, g_nsa_k,
               pe_cmp_k, w_cmp_k1, w_cmp_k2, pe_cmp_v, w_cmp_v1, w_cmp_v2, norm_mem, w_mem_kv, g_mem_q, g_mem_k,
               w_up_a, w_up_b, w_up_m, w_o, norm_ffn, w_ffn_gate, w_ffn_up, w_ffn_down)
    prm = _prep_params(*[w[0] for w in weights])
    bsz, t, d = x_prompt.shape
    db = x_sample.shape[0]
    n_pages = page_table.shape[1]
    page_rows = cache_cmp_kv.shape[2]
    past = n_pages * page_rows
    n_sel = -(-(past + 1) // SEL_BLK)
    assert t % _FAR_TK == 0 and page_rows % LANES == 0 and n_sel <= 256

    xp2 = x_prompt.reshape(bsz * t, d)
    (lx, gg, q, cmp_p, selk, wink, mq, mg, cmpt_p, selt_p, wint_p, selvt, winvt, ngt) = _project(xp2, prm, 256, bsz)
    o_lru, h_p, cv_p = _lru_prompt(lx.reshape(bsz, t, d), gg.reshape(bsz, t, d), prm, 256)

    chunks_pp = page_rows // CMP_STRIDE
    pages_p = t // page_rows
    ncp_p = _round_up(t // CMP_STRIDE, LANES)
    pool_p = cmp_p.reshape(bsz * pages_p, chunks_pp, CHUNK_W)
    table_p = jnp.arange(bsz * pages_p, dtype=I32).reshape(bsz, pages_p)
    kc_p, vct_p = _compress(pool_p, table_p, prm, ncp_p, feature_major=False)

    nsp_p = _round_up(t // SEL_BLK, LANES)
    wb, cb = _bias_tiles(rel_bias.astype(F32), ncp_p)
    ovt_p = _overlap(ncp_p, t // CMP_STRIDE - 1, nsp_p).T
    ext_p = ((jnp.arange(t) // SEL_BLK)[:, None] == jnp.arange(nsp_p)[None, :]).astype(BF16)
    o_nsa = _nsa_prompt(q.reshape(bsz, t, -1), kc_p, vct_p, selk.reshape(bsz, t, LANES), selvt,
                        wink.reshape(bsz, t, LANES), winvt, ngt, wb, cb, ovt_p, ext_p)

    m_rows = mem_prompt.shape[1]
    mkv = _mem_kv(mem_prompt.reshape(bsz * m_rows, d), prm)
    y_p = _merge_ffn(xp2, o_lru.reshape(bsz * t, d), o_nsa.reshape(bsz * t, -1), mq, mg, prm, 256,
                     mem_kv=mkv.reshape(bsz, m_rows, -1))

    xs2 = x_sample.reshape(db, d)
    (lx_s, gg_s, q_s, _, _, _, mq_s, mg_s, cmpt_s, selt_s, wint_s, _, _, ngt_s) = _project(xs2, prm, db, 1)
    sel_s, win_s = selt_s[0].T, wint_s[0].T
    cv0 = jnp.transpose(state_conv[0], (1, 0, 2))
    o_lru_s, h_s, cv_s = _lru_sample(lx_s, gg_s, cv0, state_lru_h[0], prm)

    ncp_s = _round_up(past // CMP_STRIDE, LANES)
    pool_c = _feature_major(cache_cmp_kv[0]).reshape(-1, KV_W, page_rows)
    kc_s, vct_s = _compress(pool_c, page_table, prm, ncp_s, feature_major=True)

    nsp_s = _round_up(n_sel, LANES)
    n_top = min(TOP_N, n_sel)
    ov_s = _overlap(ncp_s, (past + 1) // CMP_STRIDE - 1, nsp_s)
    tab8 = jnp.zeros((NSA_HEADS, LANES), F32).at[:, :REL_BUCKETS].set(rel_bias.astype(F32).T)
    q8 = q_s.reshape(db, NSA_HEADS, NSA_HD)
    qz = jnp.concatenate([jnp.where(jnp.arange(NSA_HEADS)[None, :, None] // NSA_GRP == gi, q8, 0.0)
                          for gi in range(NSA_KV)], axis=2)
    oc_s, idx = _sample_score(qz, kc_s, vct_s, tab8, ov_s, past, n_sel)
    idx2d = idx[:, :NSA_KV, :n_top].astype(I32).reshape(db, NSA_KV * n_top)

    n_gg = N_BRANCH * NSA_GRP
    gate8 = jnp.concatenate([ngt_s[0, gi * LANES:gi * LANES + n_gg, :] for gi in range(NSA_KV)], axis=0).T
    gate8 = jnp.pad(gate8.reshape(db, NSA_HEADS, N_BRANCH), ((0, 0), (0, 0), (0, LANES - N_BRANCH)))
    mq8 = jnp.pad(mq_s.astype(F32).reshape(db, MEM_HEADS, MEM_HD), ((0, 0), (0, NSA_HEADS - MEM_HEADS), (0, 0)))
    ex_s = (jnp.arange(LANES)[:, None] == (jnp.arange(n_top * page_rows) // page_rows)[None, :]).astype(BF16)
    pool_s = _feature_major(cache_sel_kv[0])
    win_t = _feature_major(cache_win_kv[0])
    wb_rows = win_t.shape[-1]
    mem_s = cache_mem_kv[0]
    n_mem = mem_s.shape[1]
    o_nsa8, o_mem8, win_new = _sample_attn(
        page_table, idx2d, pool_s, q8, qz, idx, ex_s, oc_s, gate8, sel_s.reshape(db, 1, KV_W),
        win_s.reshape(db, 1, KV_W), win_s.T, win_t.reshape(db, KV_W, wb_rows), mq8,
        mem_s.reshape(db, n_mem * 2 * MEM_HEADS, MEM_HD), tab8, past, n_top)
    o_nsa_s = o_nsa8.reshape(db, NSA_HEADS * NSA_HD)
    o_mem_s = o_mem8[:, :MEM_HEADS].reshape(db, MEM_HEADS * MEM_HD)
    y_s = _merge_ffn(xs2, o_lru_s, o_nsa_s.astype(BF16), o_mem_s.astype(BF16), mg_s, prm, db)

    def rows_major(a):
        n, _, rows = a.shape
        return jnp.transpose(a.reshape(n, 2, NSA_KV, NSA_HD, rows), (0, 4, 1, 2, 3))[None]

    def new_rows(a):
        return jnp.transpose(a.reshape(2, NSA_KV, NSA_HD, db), (3, 0, 1, 2))[None, :, None]

    w_keep = min(WINDOW, t)
    return (y_p.reshape(bsz, t, d), y_s.reshape(db, 1, d),
            rows_major(cmpt_p), new_rows(cmpt_s), rows_major(selt_p), new_rows(selt_s),
            rows_major(wint_p[:, :, t - w_keep:]), rows_major(win_new),
            mkv.reshape(1, bsz, m_rows, 2, MEM_HEADS, MEM_HD),
            h_p.reshape(1, bsz, d), h_s.reshape(1, db, d),
            cv_p.reshape(1, bsz, CONV_W - 1, d), jnp.transpose(cv_s, (1, 0, 2))[None])
```
